```python
import jax
import jax.numpy as jnp
from jax import lax
import numpy as np

D_MODEL = 1024
BATCH = 8
SEQ = 2048
DEPTH = 4

N_MIXERS = 3
N_MLA_LAYERS = (DEPTH + 2) // 3
N_SWA_LAYERS = (DEPTH + 1) // 3
N_FOX_LAYERS = DEPTH // 3
BLOCK = 128
NEG_INF = -1e30
RMS_EPS = 1e-6
LN_EPS = 1e-5
DEEPNORM_ALPHA = (2 * DEPTH) ** 0.25
DEEPNORM_BETA = (8 * DEPTH) ** -0.25

MLA_HEADS = 16
MLA_NOPE_DIM = 64
MLA_ROPE_DIM = 32
MLA_V_DIM = 64
MLA_Q_RANK = 384
MLA_KV_RANK = 256
ROPE_THETA = 10000.0
MLA_WIDTH = MLA_HEADS * MLA_V_DIM
MLA_IN = MLA_Q_RANK + MLA_KV_RANK + MLA_ROPE_DIM + MLA_WIDTH

SWA_Q_HEADS = 16
SWA_KV_HEADS = 4
SWA_GROUP = SWA_Q_HEADS // SWA_KV_HEADS
SWA_HEAD_DIM = 64
WINDOW = 128
SWA_WIDTH = SWA_Q_HEADS * SWA_HEAD_DIM
SWA_KV_WIDTH = SWA_KV_HEADS * SWA_HEAD_DIM
SWA_IN = 2 * SWA_WIDTH + 2 * SWA_KV_WIDTH

FOX_HEADS = 16
FOX_HEAD_DIM = 64
FOX_WIDTH = FOX_HEADS * FOX_HEAD_DIM
FOX_IN = 4 * FOX_WIDTH + FOX_HEADS

kernel_name = "hybrid_mla_swa_fox_deepnorm"


def _rmsnorm(x, g):
    xf = x.astype(jnp.float32)
    y = xf * lax.rsqrt(jnp.mean(xf * xf, axis=-1, keepdims=True) + RMS_EPS)
    return (y * g.astype(jnp.float32)).astype(x.dtype)


def _layernorm(x, g, b):
    xf = x.astype(jnp.float32)
    mu = jnp.mean(xf, axis=-1, keepdims=True)
    var = jnp.mean(jnp.square(xf - mu), axis=-1, keepdims=True)
    y = (xf - mu) * lax.rsqrt(var + LN_EPS) * g.astype(jnp.float32) + b.astype(jnp.float32)
    return y.astype(x.dtype)


def _rope_tables(seq, dim):
    inv = ROPE_THETA ** (-jnp.arange(0, dim, 2, dtype=jnp.float32) / dim)
    ang = jnp.arange(seq, dtype=jnp.float32)[:, None] * inv[None, :]
    return jnp.cos(ang), jnp.sin(ang)


def _apply_rope(x, cos, sin):
    x1, x2 = jnp.split(x.astype(jnp.float32), 2, axis=-1)
    return jnp.concatenate([x1 * cos - x2 * sin, x2 * cos + x1 * sin], axis=-1).astype(x.dtype)


def _alibi_slopes(n_heads):
    return 2.0 ** (-8.0 * jnp.arange(1, n_heads + 1, dtype=jnp.float32) / n_heads)


def _dense_causal(q, k, v, scale, cum=None):
    B, S, H, _ = q.shape
    dv = v.shape[-1]
    nb = S // BLOCK
    k_pos = jnp.arange(S)
    cum_t = None if cum is None else jnp.transpose(cum, (0, 2, 1))

    def one_block(n):
        start = n * BLOCK
        q_blk = lax.dynamic_slice_in_dim(q, start, BLOCK, axis=1)
        q_pos = start + jnp.arange(BLOCK)
        s = jnp.einsum('bqhd,bkhd->bhqk', q_blk, k, preferred_element_type=jnp.float32) * scale
        if cum_t is not None:
            c_q = lax.dynamic_slice_in_dim(cum_t, start, BLOCK, axis=2)
            s = s + c_q[..., :, None] - cum_t[:, :, None, :]
        s = jnp.where(k_pos[None, :] <= q_pos[:, None], s, NEG_INF)
        p = jax.nn.softmax(s, axis=-1).astype(v.dtype)
        return jnp.einsum('bhqk,bkhd->bqhd', p, v)

    out = lax.map(one_block, jnp.arange(nb))
    return jnp.transpose(out, (1, 0, 2, 3, 4)).reshape(B, S, H, dv)


def _mla_branch(x, w_in, q_norm, w_q_up, kv_norm, w_kv_up, w_out):
    B, S, _ = x.shape
    h = x @ w_in
    c1 = MLA_Q_RANK
    c2 = c1 + MLA_KV_RANK
    c3 = c2 + MLA_ROPE_DIM
    cq, ckv, k_rope, gate = h[..., :c1], h[..., c1:c2], h[..., c2:c3], h[..., c3:]
    q = (_rmsnorm(cq, q_norm) @ w_q_up).reshape(B, S, MLA_HEADS, MLA_NOPE_DIM + MLA_ROPE_DIM)
    kv = (_rmsnorm(ckv, kv_norm) @ w_kv_up).reshape(B, S, MLA_HEADS, MLA_NOPE_DIM + MLA_V_DIM)
    cos, sin = _rope_tables(S, MLA_ROPE_DIM)
    q_rope = _apply_rope(q[..., MLA_NOPE_DIM:], cos[:, None, :], sin[:, None, :])
    k_rope = _apply_rope(k_rope, cos, sin)
    q_full = jnp.concatenate([q[..., :MLA_NOPE_DIM], q_rope], axis=-1)
    k_full = jnp.concatenate(
        [kv[..., :MLA_NOPE_DIM],
         jnp.broadcast_to(k_rope[:, :, None, :], (B, S, MLA_HEADS, MLA_ROPE_DIM))], axis=-1)
    v = kv[..., MLA_NOPE_DIM:]
    scale = (MLA_NOPE_DIM + MLA_ROPE_DIM) ** -0.5
    o = _dense_causal(q_full, k_full, v, scale).reshape(B, S, MLA_WIDTH)
    return (o * jax.nn.silu(gate)) @ w_out


def _swa_branch(x, w_in, sinks, w_out):
    B, S, _ = x.shape
    nb = S // BLOCK
    h = x @ w_in
    c1 = SWA_WIDTH
    c2 = c1 + SWA_KV_WIDTH
    c3 = c2 + SWA_KV_WIDTH
    q = h[..., :c1].reshape(B, nb, BLOCK, SWA_KV_HEADS, SWA_GROUP, SWA_HEAD_DIM)
    k = h[..., c1:c2].reshape(B, nb, BLOCK, SWA_KV_HEADS, SWA_HEAD_DIM)
    v = h[..., c2:c3].reshape(B, nb, BLOCK, SWA_KV_HEADS, SWA_HEAD_DIM)
    gate = h[..., c3:]

    def band(t):
        prev = jnp.concatenate([jnp.zeros_like(t[:, :1]), t[:, :-1]], axis=1)
        return jnp.concatenate([prev, t], axis=2)

    kb, vb = band(k), band(v)
    scale = SWA_HEAD_DIM ** -0.5
    s = jnp.einsum('bnqkgd,bnskd->bnkgqs', q, kb, preferred_element_type=jnp.float32) * scale
    qi = jnp.arange(BLOCK)[:, None]
    kj = jnp.arange(2 * BLOCK)[None, :]
    dist = qi + BLOCK - kj
    key_pos = jnp.arange(nb)[:, None, None] * BLOCK - BLOCK + kj[None]
    valid = (dist >= 0) & (dist < WINDOW) & (key_pos >= 0)
    slopes = _alibi_slopes(SWA_Q_HEADS).reshape(SWA_KV_HEADS, SWA_GROUP)
    s = s - slopes[:, :, None, None] * dist.astype(jnp.float32)
    s = jnp.where(valid[None, :, None, None], s, NEG_INF)
    sink = sinks.astype(jnp.float32).reshape(SWA_KV_HEADS, SWA_GROUP)[:, :, None, None]
    m = jnp.maximum(jnp.max(s, axis=-1, keepdims=True), sink)
    p = jnp.exp(s - m)
    p = (p / (jnp.sum(p, axis=-1, keepdims=True) + jnp.exp(sink - m))).astype(vb.dtype)
    o = jnp.einsum('bnkgqs,bnskd->bnqkgd', p, vb).reshape(B, S, SWA_WIDTH)
    return (o * jax.nn.silu(gate)) @ w_out


def _fox_branch(x, w_in, b_f, w_out):
    B, S, _ = x.shape
    h = x @ w_in
    W = FOX_WIDTH
    q = h[..., :W].reshape(B, S, FOX_HEADS, FOX_HEAD_DIM)
    k = h[..., W:2 * W].reshape(B, S, FOX_HEADS, FOX_HEAD_DIM)
    v = h[..., 2 * W:3 * W].reshape(B, S, FOX_HEADS, FOX_HEAD_DIM)
    f_logit = h[..., 3 * W:3 * W + FOX_HEADS]
    gate = h[..., 3 * W + FOX_HEADS:]
    log_f = jax.nn.log_sigmoid(f_logit.astype(jnp.float32) + b_f.astype(jnp.float32))
    cum = jnp.cumsum(log_f, axis=1)
    o = _dense_causal(q, k, v, FOX_HEAD_DIM ** -0.5, cum).reshape(B, S, W)
    return (o * jax.nn.silu(gate)) @ w_out


def setup_inputs(seed: int = 0) -> dict:
    key = jax.random.key(seed)
    ks = jax.random.split(key, 16)

    def nrm(k, shape, scale):
        return jax.random.normal(k, shape, jnp.float32) * scale

    return {
        "x": nrm(ks[0], (BATCH, SEQ, D_MODEL), 1.0),
        "ln_g": 1.0 + nrm(ks[1], (DEPTH, D_MODEL), 0.02),
        "ln_b": nrm(ks[2], (DEPTH, D_MODEL), 0.02),
        "mla_w_in": nrm(ks[3], (N_MLA_LAYERS, D_MODEL, MLA_IN), D_MODEL ** -0.5),
        "mla_q_norm": 1.0 + nrm(ks[4], (N_MLA_LAYERS, MLA_Q_RANK), 0.02),
        "mla_w_q_up": nrm(ks[5], (N_MLA_LAYERS, MLA_Q_RANK, MLA_HEADS * (MLA_NOPE_DIM + MLA_ROPE_DIM)), MLA_Q_RANK ** -0.5),
        "mla_kv_norm": 1.0 + nrm(ks[6], (N_MLA_LAYERS, MLA_KV_RANK), 0.02),
        "mla_w_kv_up": nrm(ks[7], (N_MLA_LAYERS, MLA_KV_RANK, MLA_HEADS * (MLA_NOPE_DIM + MLA_V_DIM)), MLA_KV_RANK ** -0.5),
        "mla_w_out": nrm(ks[8], (N_MLA_LAYERS, MLA_WIDTH, D_MODEL), DEEPNORM_BETA * MLA_WIDTH ** -0.5),
        "swa_w_in": nrm(ks[9], (N_SWA_LAYERS, D_MODEL, SWA_IN), D_MODEL ** -0.5),
        "swa_sinks": nrm(ks[10], (N_SWA_LAYERS, SWA_Q_HEADS), 0.5),
        "swa_w_out": nrm(ks[11], (N_SWA_LAYERS, SWA_WIDTH, D_MODEL), DEEPNORM_BETA * SWA_WIDTH ** -0.5),
        "fox_w_in": nrm(ks[12], (N_FOX_LAYERS, D_MODEL, FOX_IN), D_MODEL ** -0.5),
        "fox_b_f": jax.random.uniform(ks[13], (N_FOX_LAYERS, FOX_HEADS), jnp.float32, 1.0, 4.0),
        "fox_w_out": nrm(ks[14], (N_FOX_LAYERS, FOX_WIDTH, D_MODEL), DEEPNORM_BETA * FOX_WIDTH ** -0.5),
    }


def reference(x, ln_g, ln_b, mla_w_in, mla_q_norm, mla_w_q_up, mla_kv_norm, mla_w_kv_up, mla_w_out,
              swa_w_in, swa_sinks, swa_w_out, fox_w_in, fox_b_f, fox_w_out):
    for i in range(DEPTH):
        j = i // N_MIXERS
        kind = i % N_MIXERS
        if kind == 0:
            y = _mla_branch(x, mla_w_in[j], mla_q_norm[j], mla_w_q_up[j], mla_kv_norm[j],
                            mla_w_kv_up[j], mla_w_out[j])
        elif kind == 1:
            y = _swa_branch(x, swa_w_in[j], swa_sinks[j], swa_w_out[j])
        else:
            y = _fox_branch(x, fox_w_in[j], fox_b_f[j], fox_w_out[j])
        x = _layernorm(DEEPNORM_ALPHA * x + y, ln_g[i], ln_b[i])
    return x
```

```python
import functools
import math

import jax
import jax.numpy as jnp
import numpy as np
from jax import lax
from jax.experimental import pallas as pl
from jax.experimental.pallas import tpu as pltpu

D_MODEL = 1024
DEPTH = 4
N_MIXERS = 3
BLOCK = 128
NEG_INF = -1e30
RMS_EPS = 1e-6
LN_EPS = 1e-5
DEEPNORM_ALPHA = (2 * DEPTH) ** 0.25
LOG2E = math.log2(math.e)

MLA_HEADS = 16
MLA_NOPE_DIM = 64
MLA_ROPE_DIM = 32
MLA_V_DIM = 64
MLA_Q_RANK = 384
MLA_KV_RANK = 256
ROPE_THETA = 10000.0
MLA_WIDTH = MLA_HEADS * MLA_V_DIM

SWA_Q_HEADS = 16
SWA_KV_HEADS = 4
SWA_GROUP = SWA_Q_HEADS // SWA_KV_HEADS
SWA_HEAD_DIM = 64
WINDOW = 128
SWA_WIDTH = SWA_Q_HEADS * SWA_HEAD_DIM
SWA_KV_WIDTH = SWA_KV_HEADS * SWA_HEAD_DIM

FOX_HEADS = 16
FOX_HEAD_DIM = 64
FOX_WIDTH = FOX_HEADS * FOX_HEAD_DIM

LANES = 128
HALF = LANES // 2
HEADS = 16
PADW = HEADS * LANES

ROW_TILE = 256
TQ = 256
TK = 256
VMEM_LIMIT = 48 * 1024 * 1024

F32 = jnp.float32
BF16 = jnp.bfloat16


def _params(sem):
    return pltpu.CompilerParams(dimension_semantics=sem, vmem_limit_bytes=VMEM_LIMIT)


def _lane_lt_half(shape):
    return lax.broadcasted_iota(jnp.int32, shape, len(shape) - 1) < HALF


def _silu(g):
    return g / (1.0 + jnp.exp(-g))


def _mla_proj_kernel(x_ref, win_ref, qg_ref, kg_ref, wq_ref, wk_ref, wv_ref, ones_ref,
                     aq_ref, bq_ref, ak_ref, bk_ref, q_out, k_out, v_out, g_out):
    xb = x_ref[...].astype(BF16)
    h = jnp.dot(xb, win_ref[...], preferred_element_type=F32)
    c1 = MLA_Q_RANK
    c2 = c1 + MLA_KV_RANK
    c3 = c2 + LANES
    cq, ckv, krg, gate = h[:, :c1], h[:, c1:c2], h[:, c2:c3], h[:, c3:]
    g_out[...] = _silu(gate).astype(BF16)

    qn = cq * lax.rsqrt(jnp.mean(cq * cq, axis=-1, keepdims=True) + RMS_EPS) * qg_ref[...]
    kn = ckv * lax.rsqrt(jnp.mean(ckv * ckv, axis=-1, keepdims=True) + RMS_EPS) * kg_ref[...]
    qn = qn.astype(BF16)
    kn = kn.astype(BF16)
    q = jnp.dot(qn, wq_ref[...], preferred_element_type=F32)
    kf = jnp.dot(kn, wk_ref[...], preferred_element_type=F32)
    vv = jnp.dot(kn, wv_ref[...], preferred_element_type=F32)
    v_out[...] = (vv + ones_ref[...]).astype(BF16)

    aq, bq = aq_ref[...], bq_ref[...]
    kr = krg * ak_ref[...] + pltpu.roll(krg, LANES - MLA_ROPE_DIM, 1) * bk_ref[...]
    for hh in range(HEADS):
        sl = slice(hh * LANES, (hh + 1) * LANES)
        qh = q[:, sl]
        q_out[:, sl] = (qh * aq + pltpu.roll(qh, LANES - MLA_ROPE_DIM, 1) * bq).astype(BF16)
        k_out[:, sl] = (kf[:, sl] + kr).astype(BF16)


def _mla_proj(x, win, qg, kg, wq, wk, wv, ones, aq, bq, ak, bk):
    B, S, _ = x.shape
    tm = ROW_TILE
    row = lambda w: pl.BlockSpec((None, tm, w), lambda b, i: (b, i, 0))
    full = lambda a: pl.BlockSpec(a.shape, lambda b, i: (0,) * a.ndim)
    tab = pl.BlockSpec((tm, LANES), lambda b, i: (i, 0))
    out_shape = [jax.ShapeDtypeStruct((B, S, PADW), BF16)] * 3 + [jax.ShapeDtypeStruct((B, S, MLA_WIDTH), BF16)]
    return pl.pallas_call(
        _mla_proj_kernel,
        grid=(B, S // tm),
        in_specs=[row(D_MODEL), full(win), full(qg), full(kg), full(wq), full(wk), full(wv), full(ones),
                  tab, tab, tab, tab],
        out_specs=[row(PADW), row(PADW), row(PADW), row(MLA_WIDTH)],
        out_shape=out_shape,
        compiler_params=_params(("parallel", "parallel")),
        name="mla_proj",
    )(x, win, qg, kg, wq, wk, wv, ones, aq, bq, ak, bk)


def _split_pairs(nat, aug, out_ref):
    lo = _lane_lt_half((nat.shape[0], LANES))
    for p in range(HEADS // 2):
        blk = nat[:, p * LANES:(p + 1) * LANES]
        e = slice((2 * p) * LANES, (2 * p + 1) * LANES)
        o = slice((2 * p + 1) * LANES, (2 * p + 2) * LANES)
        fill_e = 1.0 if aug is None else aug[:, e]
        fill_o = 1.0 if aug is None else aug[:, o]
        out_ref[:, e] = jnp.where(lo, blk, fill_e).astype(BF16)
        out_ref[:, o] = jnp.where(lo, fill_o, blk).astype(BF16)


def _fox_proj_kernel(x_ref, win_ref, bf_ref, scat_ref, onesq_ref, onesk_ref,
                     q_out, k_out, v_out, g_out, carry_ref):
    tm = x_ref.shape[0]

    @pl.when(pl.program_id(1) == 0)
    def _():
        carry_ref[...] = jnp.zeros_like(carry_ref)

    xb = x_ref[...].astype(BF16)
    h = jnp.dot(xb, win_ref[...], preferred_element_type=F32)
    W = FOX_WIDTH
    g_out[...] = _silu(h[:, 3 * W:4 * W]).astype(BF16)

    f = h[:, 4 * W:4 * W + LANES] + bf_ref[...]
    logf = jnp.minimum(f, 0.0) - jnp.log(1.0 + jnp.exp(-jnp.abs(f)))

    def split3(a):
        hi = a.astype(BF16)
        r = a - hi.astype(F32)
        mid = r.astype(BF16)
        lo = (r - mid.astype(F32)).astype(BF16)
        return hi, mid, lo

    ri = lax.broadcasted_iota(jnp.int32, (tm, tm), 0)
    ci = lax.broadcasted_iota(jnp.int32, (tm, tm), 1)
    tri = (ci <= ri).astype(BF16)
    l_hi, l_mid, l_lo = split3(logf)
    cum = (jnp.dot(tri, l_hi, preferred_element_type=F32) + jnp.dot(tri, l_mid, preferred_element_type=F32)
           + jnp.dot(tri, l_lo, preferred_element_type=F32)) + carry_ref[...]
    carry_ref[...] = cum[tm - 1:tm, :]

    c_hi, c_mid, c_lo = split3(cum * LOG2E)
    lane = lax.broadcasted_iota(jnp.int32, (tm, LANES), 1)
    packed = jnp.where(lane < HEADS, c_hi.astype(F32),
                       jnp.where(lane < 2 * HEADS, pltpu.roll(c_mid.astype(F32), HEADS, 1),
                                 pltpu.roll(c_lo.astype(F32), 2 * HEADS, 1)))
    packed = jnp.where(lane < 3 * HEADS, packed, 0.0).astype(BF16)
    aug = jnp.dot(packed, scat_ref[...], preferred_element_type=F32)
    augq = aug[:, :PADW] + onesq_ref[...]
    augk = aug[:, PADW:] + onesk_ref[...]

    _split_pairs(h[:, :W] * (FOX_HEAD_DIM ** -0.5 * LOG2E), augq, q_out)
    _split_pairs(h[:, W:2 * W], augk, k_out)
    _split_pairs(h[:, 2 * W:3 * W], None, v_out)


def _fox_proj(x, win, bf, scat, onesq, onesk):
    B, S, _ = x.shape
    tm = ROW_TILE
    row = lambda w: pl.BlockSpec((None, tm, w), lambda b, i: (b, i, 0))
    full = lambda a: pl.BlockSpec(a.shape, lambda b, i: (0,) * a.ndim)
    out_shape = [jax.ShapeDtypeStruct((B, S, PADW), BF16)] * 3 + [jax.ShapeDtypeStruct((B, S, FOX_WIDTH), BF16)]
    return pl.pallas_call(
        _fox_proj_kernel,
        grid=(B, S // tm),
        in_specs=[row(D_MODEL), full(win), full(bf), full(scat), full(onesq), full(onesk)],
        out_specs=[row(PADW), row(PADW), row(PADW), row(FOX_WIDTH)],
        out_shape=out_shape,
        scratch_shapes=[pltpu.VMEM((1, LANES), F32)],
        compiler_params=_params(("parallel", "arbitrary")),
        name="fox_proj",
    )(x, win, bf, scat, onesq, onesk)


def _swa_proj_kernel(x_ref, win_ref, ones_ref, q_out, k_out, v_out, g_out):
    xb = x_ref[...].astype(BF16)
    h = jnp.dot(xb, win_ref[...], preferred_element_type=F32)
    c1 = SWA_WIDTH
    c2 = c1 + SWA_KV_HEADS * LANES
    c3 = c2 + SWA_KV_HEADS * 2 * LANES
    q_out[...] = (h[:, :c1] * (SWA_HEAD_DIM ** -0.5 * LOG2E)).astype(BF16)
    k_out[...] = h[:, c1:c2].astype(BF16)
    v_out[...] = (h[:, c2:c3] + ones_ref[...]).astype(BF16)
    g_out[...] = _silu(h[:, c3:]).astype(BF16)


def _swa_proj(x, win, ones):
    B, S, _ = x.shape
    tm = ROW_TILE
    row = lambda w: pl.BlockSpec((None, tm, w), lambda b, i: (b, i, 0))
    full = lambda a: pl.BlockSpec(a.shape, lambda b, i: (0,) * a.ndim)
    kw = SWA_KV_HEADS * LANES
    vw = SWA_KV_HEADS * 2 * LANES
    out_shape = [jax.ShapeDtypeStruct((B, S, SWA_WIDTH), BF16), jax.ShapeDtypeStruct((B, S, kw), BF16),
                 jax.ShapeDtypeStruct((B, S, vw), BF16), jax.ShapeDtypeStruct((B, S, SWA_WIDTH), BF16)]
    return pl.pallas_call(
        _swa_proj_kernel,
        grid=(B, S // tm),
        in_specs=[row(D_MODEL), full(win), full(ones)],
        out_specs=[row(SWA_WIDTH), row(kw), row(vw), row(SWA_WIDTH)],
        out_shape=out_shape,
        compiler_params=_params(("parallel", "parallel")),
        name="swa_proj",
    )(x, win, ones)


def _combine_pair(ext_even, ext_odd, extra_even=None, extra_odd=None):
    lo = _lane_lt_half(ext_even.shape)
    num = jnp.where(lo, ext_even, ext_odd)
    den = pltpu.roll(jnp.where(lo, ext_odd, ext_even), HALF, 1)
    if extra_even is not None:
        den = den + jnp.where(lo, extra_even, extra_odd)
    return num / den


def _causal_attn_kernel(q_ref, k_ref, v_ref, o_ref, ext_ref):
    S = q_ref.shape[0]
    row = lax.broadcasted_iota(jnp.int32, (TQ, TK), 0)
    col = lax.broadcasted_iota(jnp.int32, (TQ, TK), 1)
    causal = col <= row

    for par in range(2):
        lanes = slice(par * LANES, (par + 1) * LANES)

        def q_body(qi, _, par=par, lanes=lanes):
            q0 = pl.multiple_of(qi * TQ, TQ)
            q = q_ref[pl.ds(q0, TQ), lanes]

            def step(k0, carry, masked):
                m, acc = carry
                k = k_ref[pl.ds(k0, TK), lanes]
                v = v_ref[pl.ds(k0, TK), lanes]
                s = lax.dot_general(q, k, (((1,), (1,)), ((), ())), preferred_element_type=F32)
                if masked:
                    s = jnp.where(causal, s, NEG_INF)
                m_new = jnp.maximum(m, jnp.max(s, axis=-1, keepdims=True))
                alpha = jnp.exp2(m - m_new)
                p = jnp.exp2(s - m_new).astype(BF16)
                acc = acc * alpha + jnp.dot(p, v, preferred_element_type=F32)
                return m_new, acc

            carry = (jnp.full((TQ, 1), NEG_INF, F32), jnp.zeros((TQ, LANES), F32))
            carry = lax.fori_loop(0, qi, lambda kj, c: step(pl.multiple_of(kj * TK, TK), c, False), carry)
            _, acc = step(q0, carry, True)
            if par == 0:
                ext_ref[pl.ds(q0, TQ), :] = acc
            else:
                o_ref[pl.ds(q0, TQ), :] = _combine_pair(ext_ref[pl.ds(q0, TQ), :], acc).astype(BF16)
            return 0

        lax.fori_loop(0, S // TQ, q_body, 0)


def _causal_attn(q, k, v):
    B, S, _ = q.shape
    spec = pl.BlockSpec((None, S, 2 * LANES), lambda b, p: (b, 0, p))
    return pl.pallas_call(
        _causal_attn_kernel,
        grid=(B, HEADS // 2),
        in_specs=[spec, spec, spec],
        out_specs=pl.BlockSpec((None, S, LANES), lambda b, p: (b, 0, p)),
        out_shape=jax.ShapeDtypeStruct((B, S, HEADS * HALF), BF16),
        scratch_shapes=[pltpu.VMEM((S, LANES), F32)],
        compiler_params=_params(("parallel", "parallel")),
        name="causal_attn",
    )(q, k, v)


def _swa_attn_kernel(sink_ref, q_ref, kp_ref, kc_ref, vp_ref, vc_ref, bias_ref, o_ref):
    lo = _lane_lt_half((BLOCK, LANES))
    zero = jnp.zeros((), BF16)
    for g in range(SWA_KV_HEADS):
        ks = slice(g * LANES, (g + 1) * LANES)
        kb = jnp.concatenate([kp_ref[:, ks], kc_ref[:, ks]], axis=0)
        qa = q_ref[:, (2 * g) * LANES:(2 * g + 1) * LANES]
        qb = q_ref[:, (2 * g + 1) * LANES:(2 * g + 2) * LANES]
        ext, extra = [], []
        for par in range(2):
            keep = lo if par == 0 else jnp.logical_not(lo)
            stack = jnp.concatenate([jnp.where(keep, qa, zero), jnp.where(keep, qb, zero)], axis=0)
            ha, hb = 4 * g + par, 4 * g + 2 + par
            s = lax.dot_general(stack, kb, (((1,), (1,)), ((), ())), preferred_element_type=F32)
            s = s + jnp.concatenate([bias_ref[ha], bias_ref[hb]], axis=0)
            sink = jnp.concatenate([jnp.full((BLOCK, 1), sink_ref[ha], F32),
                                    jnp.full((BLOCK, 1), sink_ref[hb], F32)], axis=0)
            m = jnp.maximum(jnp.max(s, axis=-1, keepdims=True), sink)
            p = jnp.exp2(s - m).astype(BF16)
            vs = slice((2 * g + par) * LANES, (2 * g + par + 1) * LANES)
            vb = jnp.concatenate([vp_ref[:, vs], vc_ref[:, vs]], axis=0)
            ext.append(jnp.dot(p, vb, preferred_element_type=F32))
            extra.append(jnp.broadcast_to(jnp.exp2(sink - m), (2 * BLOCK, LANES)))
        for j in range(2):
            rows = slice(j * BLOCK, (j + 1) * BLOCK)
            o_ref[:, (2 * g + j) * LANES:(2 * g + j + 1) * LANES] = _combine_pair(
                ext[0][rows], ext[1][rows], extra[0][rows], extra[1][rows]).astype(BF16)


def _swa_attn(sinks2, q, k2, vx2, bias):
    B, S, _ = q.shape
    nb = S // BLOCK
    kw, vw = k2.shape[-1], vx2.shape[-1]
    cur = lambda w: pl.BlockSpec((None, BLOCK, w), lambda b, n: (b, n, 0))
    prev = lambda w: pl.BlockSpec((None, BLOCK, w), lambda b, n: (b, jnp.maximum(n - 1, 0), 0))
    return pl.pallas_call(
        _swa_attn_kernel,
        grid=(B, nb),
        in_specs=[pl.BlockSpec(memory_space=pltpu.SMEM), cur(SWA_WIDTH), prev(kw), cur(kw), prev(vw), cur(vw),
                  pl.BlockSpec((None, SWA_Q_HEADS, BLOCK, 2 * BLOCK), lambda b, n: (jnp.minimum(n, 1), 0, 0, 0))],
        out_specs=cur(SWA_WIDTH),
        out_shape=jax.ShapeDtypeStruct((B, S, SWA_WIDTH), BF16),
        compiler_params=_params(("parallel", "parallel")),
        name="swa_attn",
    )(sinks2, q, k2, k2, vx2, vx2, bias)


def _out_kernel(o_ref, g_ref, x_ref, w_ref, lg_ref, lb_ref, y_ref):
    a = (o_ref[...].astype(F32) * g_ref[...].astype(F32)).astype(BF16)
    y = jnp.dot(a, w_ref[...], preferred_element_type=F32)
    z = DEEPNORM_ALPHA * x_ref[...] + y
    mu = jnp.mean(z, axis=-1, keepdims=True)
    zc = z - mu
    var = jnp.mean(zc * zc, axis=-1, keepdims=True)
    y_ref[...] = zc * lax.rsqrt(var + LN_EPS) * lg_ref[...] + lb_ref[...]


def _out_proj(o, sg, x, w, lg, lb):
    B, S, _ = x.shape
    tm = ROW_TILE
    row = pl.BlockSpec((None, tm, D_MODEL), lambda b, i: (b, i, 0))
    full = lambda a: pl.BlockSpec(a.shape, lambda b, i: (0,) * a.ndim)
    return pl.pallas_call(
        _out_kernel,
        grid=(B, S // tm),
        in_specs=[row, row, row, full(w), full(lg), full(lb)],
        out_specs=row,
        out_shape=jax.ShapeDtypeStruct((B, S, D_MODEL), F32),
        compiler_params=_params(("parallel", "parallel")),
        name="out_proj",
    )(o, sg, x, w, lg, lb)


def _parity_ones(n_groups, parity):
    lane_hi = (np.arange(LANES) >= HALF)[None, :]
    par = np.asarray(parity, dtype=bool)[:, None]
    return (lane_hi != par).astype(np.float32).reshape(1, n_groups * LANES)


def _rot(w):
    half = w.shape[-1] // 2
    return jnp.concatenate([-w[..., half:], w[..., :half]], axis=-1)


def _rope_tables(seq, c0):
    inv = ROPE_THETA ** (-np.arange(0, MLA_ROPE_DIM, 2, dtype=np.float64) / MLA_ROPE_DIM)
    ang = np.arange(seq, dtype=np.float64)[:, None] * inv[None, :]
    cos = np.concatenate([np.cos(ang)] * 2, axis=-1)
    sin = np.concatenate([np.sin(ang)] * 2, axis=-1)
    z32 = np.zeros((seq, LANES - HALF - MLA_ROPE_DIM))
    z64 = np.zeros((seq, HALF))
    aq = np.concatenate([np.full((seq, HALF), c0), c0 * cos, z32], axis=-1)
    bq = np.concatenate([z64, c0 * sin, z32], axis=-1)
    ak = np.concatenate([z64, cos, z32], axis=-1)
    bk = np.concatenate([z64, sin, z32], axis=-1)
    return tuple(t.astype(np.float32) for t in (aq, bq, ak, bk))


def _mla_layer_operands(w_in, q_norm, w_q_up, kv_norm, w_kv_up, seq):
    c1 = MLA_Q_RANK
    c2 = c1 + MLA_KV_RANK
    c3 = c2 + MLA_ROPE_DIM
    kr = w_in[:, c2:c3]
    krg = jnp.concatenate([jnp.zeros((D_MODEL, HALF), F32), kr, _rot(kr)], axis=-1)
    win = jnp.concatenate([w_in[:, :c2], krg, w_in[:, c3:]], axis=-1).astype(BF16)
    wq = w_q_up.reshape(MLA_Q_RANK, MLA_HEADS, MLA_NOPE_DIM + MLA_ROPE_DIM)
    nope, rope = wq[..., :MLA_NOPE_DIM], wq[..., MLA_NOPE_DIM:]
    wq = jnp.concatenate([nope, rope, _rot(rope)], axis=-1).reshape(MLA_Q_RANK, PADW).astype(BF16)
    wkv = w_kv_up.reshape(MLA_KV_RANK, MLA_HEADS, MLA_NOPE_DIM + MLA_V_DIM)
    knope, v = wkv[..., :MLA_NOPE_DIM], wkv[..., MLA_NOPE_DIM:]
    z = jnp.zeros_like(v)
    wk = jnp.concatenate([knope, z], axis=-1).reshape(MLA_KV_RANK, PADW).astype(BF16)
    odd = (jnp.arange(MLA_HEADS) % 2 == 1)[None, :, None]
    wv = jnp.where(odd, jnp.concatenate([z, v], axis=-1), jnp.concatenate([v, z], axis=-1))
    wv = wv.reshape(MLA_KV_RANK, PADW).astype(BF16)
    ones = _parity_ones(MLA_HEADS, [h % 2 for h in range(MLA_HEADS)])
    c0 = (MLA_NOPE_DIM + MLA_ROPE_DIM) ** -0.5 * LOG2E
    return (win, q_norm.reshape(1, -1), kv_norm.reshape(1, -1), wq, wk, wv, ones) + _rope_tables(seq, c0)


def _fox_layer_operands(w_in, b_f):
    W = FOX_WIDTH
    wf = jnp.concatenate([w_in[:, 3 * W:3 * W + FOX_HEADS], jnp.zeros((D_MODEL, LANES - FOX_HEADS), F32)], axis=-1)
    win = jnp.concatenate([w_in[:, :3 * W], w_in[:, 3 * W + FOX_HEADS:], wf], axis=-1).astype(BF16)
    bf = jnp.concatenate([b_f, jnp.zeros((LANES - FOX_HEADS,), F32)]).reshape(1, LANES)
    h = np.arange(FOX_HEADS)
    base = h * LANES + np.where(h % 2 == 0, HALF, 0)
    scat = np.zeros((LANES, 2 * PADW), np.float32)
    onesq = np.zeros((1, PADW), np.float32)
    onesk = np.zeros((1, PADW), np.float32)
    for j in range(3):
        scat[j * FOX_HEADS + h, base + j] = 1.0
        scat[j * FOX_HEADS + h, PADW + base + 3 + j] = -1.0
        onesq[0, base + 3 + j] = 1.0
        onesk[0, base + j] = 1.0
    return win, bf, jnp.asarray(scat, dtype=BF16), onesq, onesk


def _swa_layer_operands(w_in, sinks):
    c1 = SWA_WIDTH
    c2 = c1 + SWA_KV_WIDTH
    c3 = c2 + SWA_KV_WIDTH
    k = w_in[:, c1:c2].reshape(D_MODEL, SWA_KV_HEADS, SWA_HEAD_DIM)
    v = w_in[:, c2:c3].reshape(D_MODEL, SWA_KV_HEADS, SWA_HEAD_DIM)
    z = jnp.zeros_like(v)
    k2 = jnp.concatenate([k, k], axis=-1).reshape(D_MODEL, SWA_KV_HEADS * LANES)
    v2 = jnp.concatenate([v, z, z, v], axis=-1).reshape(D_MODEL, SWA_KV_HEADS * 2 * LANES)
    win = jnp.concatenate([w_in[:, :c1], k2, v2, w_in[:, c3:]], axis=-1).astype(BF16)
    ones = _parity_ones(2 * SWA_KV_HEADS, [i % 2 for i in range(2 * SWA_KV_HEADS)])
    qi = np.arange(BLOCK)[:, None]
    kj = np.arange(2 * BLOCK)[None, :]
    dist = qi + BLOCK - kj
    valid = (dist >= 0) & (dist < WINDOW)
    slopes = 2.0 ** (-8.0 * np.arange(1, SWA_Q_HEADS + 1, dtype=np.float64) / SWA_Q_HEADS)
    ali = -(slopes[:, None, None] * dist.astype(np.float64)[None]) * LOG2E
    bias_rest = np.where(valid[None], ali, NEG_INF)
    bias_first = np.where((valid & (kj >= BLOCK))[None], ali, NEG_INF)
    bias = np.stack([bias_first, bias_rest]).astype(np.float32)
    return win, ones, bias, sinks.astype(F32) * LOG2E


def kernel(x, ln_g, ln_b, mla_w_in, mla_q_norm, mla_w_q_up, mla_kv_norm, mla_w_kv_up, mla_w_out,
           swa_w_in, swa_sinks, swa_w_out, fox_w_in, fox_b_f, fox_w_out):
    seq = x.shape[1]
    for i in range(DEPTH):
        j = i // N_MIXERS
        kind = i % N_MIXERS
        if kind == 0:
            ops = _mla_layer_operands(mla_w_in[j], mla_q_norm[j], mla_w_q_up[j], mla_kv_norm[j], mla_w_kv_up[j], seq)
            q, k, v, sg = _mla_proj(x, *ops)
            o = _causal_attn(q, k, v)
            w_out = mla_w_out[j]
        elif kind == 1:
            win, ones, bias, sinks2 = _swa_layer_operands(swa_w_in[j], swa_sinks[j])
            q, k2, vx2, sg = _swa_proj(x, win, ones)
            o = _swa_attn(sinks2, q, k2, vx2, bias)
            w_out = swa_w_out[j]
        else:
            ops = _fox_layer_operands(fox_w_in[j], fox_b_f[j])
            q, k, v, sg = _fox_proj(x, *ops)
            o = _causal_attn(q, k, v)
            w_out = fox_w_out[j]
        x = _out_proj(o, sg, x, w_out.astype(BF16), ln_g[i].reshape(1, -1), ln_b[i].reshape(1, -1))
    return x
```

```python
import functools
import math

import jax
import jax.numpy as jnp
import numpy as np
from jax import lax
from jax.experimental import pallas as pl
from jax.experimental.pallas import tpu as pltpu

D_MODEL = 1024
DEPTH = 4
N_MIXERS = 3
BLOCK = 128
NEG_INF = -1e30
RMS_EPS = 1e-6
LN_EPS = 1e-5
DEEPNORM_ALPHA = (2 * DEPTH) ** 0.25
LOG2E = math.log2(math.e)

MLA_HEADS = 16
MLA_NOPE_DIM = 64
MLA_ROPE_DIM = 32
MLA_V_DIM = 64
MLA_Q_RANK = 384
MLA_KV_RANK = 256
ROPE_THETA = 10000.0
MLA_WIDTH = MLA_HEADS * MLA_V_DIM

SWA_Q_HEADS = 16
SWA_KV_HEADS = 4
SWA_GROUP = SWA_Q_HEADS // SWA_KV_HEADS
SWA_HEAD_DIM = 64
WINDOW = 128
SWA_WIDTH = SWA_Q_HEADS * SWA_HEAD_DIM
SWA_KV_WIDTH = SWA_KV_HEADS * SWA_HEAD_DIM

FOX_HEADS = 16
FOX_HEAD_DIM = 64
FOX_WIDTH = FOX_HEADS * FOX_HEAD_DIM

LANES = 128
HALF = LANES // 2
HEADS = 16
PADW = HEADS * LANES

ROW_TILE = 256
TQ = 512
TK = 512
ATT_HEADS = 4
SWA_QB = 4
VMEM_LIMIT = 48 * 1024 * 1024

F32 = jnp.float32
BF16 = jnp.bfloat16


def _params(sem):
    return pltpu.CompilerParams(dimension_semantics=sem, vmem_limit_bytes=VMEM_LIMIT)


def _lane_lt_half(shape):
    return lax.broadcasted_iota(jnp.int32, shape, len(shape) - 1) < HALF


def _silu(g):
    return g / (1.0 + jnp.exp(-g))


def _mla_proj_kernel(x_ref, win_ref, qg_ref, kg_ref, wq_ref, wk_ref, wv_ref, ones_ref,
                     aq_ref, bq_ref, ak_ref, bk_ref, q_out, k_out, v_out, g_out):
    xb = x_ref[...].astype(BF16)
    h = jnp.dot(xb, win_ref[...], preferred_element_type=F32)
    c1 = MLA_Q_RANK
    c2 = c1 + MLA_KV_RANK
    c3 = c2 + LANES
    cq, ckv, krg, gate = h[:, :c1], h[:, c1:c2], h[:, c2:c3], h[:, c3:]
    g_out[...] = _silu(gate).astype(BF16)

    qn = cq * lax.rsqrt(jnp.mean(cq * cq, axis=-1, keepdims=True) + RMS_EPS) * qg_ref[...]
    kn = ckv * lax.rsqrt(jnp.mean(ckv * ckv, axis=-1, keepdims=True) + RMS_EPS) * kg_ref[...]
    qn = qn.astype(BF16)
    kn = kn.astype(BF16)
    q = jnp.dot(qn, wq_ref[...], preferred_element_type=F32)
    kf = jnp.dot(kn, wk_ref[...], preferred_element_type=F32)
    vv = jnp.dot(kn, wv_ref[...], preferred_element_type=F32)
    v_out[...] = (vv + ones_ref[...]).astype(BF16)

    aq, bq = aq_ref[...], bq_ref[...]
    kr = krg * ak_ref[...] + pltpu.roll(krg, LANES - MLA_ROPE_DIM, 1) * bk_ref[...]
    for hh in range(HEADS):
        sl = slice(hh * LANES, (hh + 1) * LANES)
        qh = q[:, sl]
        q_out[:, sl] = (qh * aq + pltpu.roll(qh, LANES - MLA_ROPE_DIM, 1) * bq).astype(BF16)
        k_out[:, sl] = (kf[:, sl] + kr).astype(BF16)


def _mla_proj(x, win, qg, kg, wq, wk, wv, ones, aq, bq, ak, bk):
    B, S, _ = x.shape
    tm = ROW_TILE
    row = lambda w: pl.BlockSpec((None, tm, w), lambda b, i: (b, i, 0))
    full = lambda a: pl.BlockSpec(a.shape, lambda b, i: (0,) * a.ndim)
    tab = pl.BlockSpec((tm, LANES), lambda b, i: (i, 0))
    out_shape = [jax.ShapeDtypeStruct((B, S, PADW), BF16)] * 3 + [jax.ShapeDtypeStruct((B, S, MLA_WIDTH), BF16)]
    return pl.pallas_call(
        _mla_proj_kernel,
        grid=(B, S // tm),
        in_specs=[row(D_MODEL), full(win), full(qg), full(kg), full(wq), full(wk), full(wv), full(ones),
                  tab, tab, tab, tab],
        out_specs=[row(PADW), row(PADW), row(PADW), row(MLA_WIDTH)],
        out_shape=out_shape,
        compiler_params=_params(("parallel", "parallel")),
        name="mla_proj",
    )(x, win, qg, kg, wq, wk, wv, ones, aq, bq, ak, bk)


def _split_pairs(nat, aug, out_ref):
    lo = _lane_lt_half((nat.shape[0], LANES))
    for p in range(HEADS // 2):
        blk = nat[:, p * LANES:(p + 1) * LANES]
        e = slice((2 * p) * LANES, (2 * p + 1) * LANES)
        o = slice((2 * p + 1) * LANES, (2 * p + 2) * LANES)
        fill_e = 1.0 if aug is None else aug[:, e]
        fill_o = 1.0 if aug is None else aug[:, o]
        out_ref[:, e] = jnp.where(lo, blk, fill_e).astype(BF16)
        out_ref[:, o] = jnp.where(lo, fill_o, blk).astype(BF16)


def _fox_proj_kernel(x_ref, win_ref, bf_ref, scat_ref, onesq_ref, onesk_ref,
                     q_out, k_out, v_out, g_out, carry_ref):
    tm = x_ref.shape[0]

    @pl.when(pl.program_id(1) == 0)
    def _():
        carry_ref[...] = jnp.zeros_like(carry_ref)

    xb = x_ref[...].astype(BF16)
    h = jnp.dot(xb, win_ref[...], preferred_element_type=F32)
    W = FOX_WIDTH
    g_out[...] = _silu(h[:, 3 * W:4 * W]).astype(BF16)

    f = h[:, 4 * W:4 * W + LANES] + bf_ref[...]
    logf = jnp.minimum(f, 0.0) - jnp.log(1.0 + jnp.exp(-jnp.abs(f)))

    def split3(a):
        hi = a.astype(BF16)
        r = a - hi.astype(F32)
        mid = r.astype(BF16)
        lo = (r - mid.astype(F32)).astype(BF16)
        return hi, mid, lo

    ri = lax.broadcasted_iota(jnp.int32, (tm, tm), 0)
    ci = lax.broadcasted_iota(jnp.int32, (tm, tm), 1)
    tri = (ci <= ri).astype(BF16)
    l_hi, l_mid, l_lo = split3(logf)
    cum = (jnp.dot(tri, l_hi, preferred_element_type=F32) + jnp.dot(tri, l_mid, preferred_element_type=F32)
           + jnp.dot(tri, l_lo, preferred_element_type=F32)) + carry_ref[...]
    carry_ref[...] = cum[tm - 1:tm, :]

    c_hi, c_mid, c_lo = split3(cum * LOG2E)
    lane = lax.broadcasted_iota(jnp.int32, (tm, LANES), 1)
    packed = jnp.where(lane < HEADS, c_hi.astype(F32),
                       jnp.where(lane < 2 * HEADS, pltpu.roll(c_mid.astype(F32), HEADS, 1),
                                 pltpu.roll(c_lo.astype(F32), 2 * HEADS, 1)))
    packed = jnp.where(lane < 3 * HEADS, packed, 0.0).astype(BF16)
    aug = jnp.dot(packed, scat_ref[...], preferred_element_type=F32)
    augq = aug[:, :PADW] + onesq_ref[...]
    augk = aug[:, PADW:] + onesk_ref[...]

    _split_pairs(h[:, :W] * (FOX_HEAD_DIM ** -0.5 * LOG2E), augq, q_out)
    _split_pairs(h[:, W:2 * W], augk, k_out)
    _split_pairs(h[:, 2 * W:3 * W], None, v_out)


def _fox_proj(x, win, bf, scat, onesq, onesk):
    B, S, _ = x.shape
    tm = ROW_TILE
    row = lambda w: pl.BlockSpec((None, tm, w), lambda b, i: (b, i, 0))
    full = lambda a: pl.BlockSpec(a.shape, lambda b, i: (0,) * a.ndim)
    out_shape = [jax.ShapeDtypeStruct((B, S, PADW), BF16)] * 3 + [jax.ShapeDtypeStruct((B, S, FOX_WIDTH), BF16)]
    return pl.pallas_call(
        _fox_proj_kernel,
        grid=(B, S // tm),
        in_specs=[row(D_MODEL), full(win), full(bf), full(scat), full(onesq), full(onesk)],
        out_specs=[row(PADW), row(PADW), row(PADW), row(FOX_WIDTH)],
        out_shape=out_shape,
        scratch_shapes=[pltpu.VMEM((1, LANES), F32)],
        compiler_params=_params(("parallel", "arbitrary")),
        name="fox_proj",
    )(x, win, bf, scat, onesq, onesk)


def _swa_proj_kernel(x_ref, win_ref, ones_ref, q_out, k_out, v_out, g_out):
    xb = x_ref[...].astype(BF16)
    h = jnp.dot(xb, win_ref[...], preferred_element_type=F32)
    c1 = SWA_WIDTH
    c2 = c1 + SWA_KV_HEADS * LANES
    c3 = c2 + SWA_KV_HEADS * 2 * LANES
    q_out[...] = (h[:, :c1] * (SWA_HEAD_DIM ** -0.5 * LOG2E)).astype(BF16)
    k_out[...] = h[:, c1:c2].astype(BF16)
    v_out[...] = (h[:, c2:c3] + ones_ref[...]).astype(BF16)
    g_out[...] = _silu(h[:, c3:]).astype(BF16)


def _swa_proj(x, win, ones):
    B, S, _ = x.shape
    tm = ROW_TILE
    row = lambda w: pl.BlockSpec((None, tm, w), lambda b, i: (b, i, 0))
    full = lambda a: pl.BlockSpec(a.shape, lambda b, i: (0,) * a.ndim)
    kw = SWA_KV_HEADS * LANES
    vw = SWA_KV_HEADS * 2 * LANES
    out_shape = [jax.ShapeDtypeStruct((B, S, SWA_WIDTH), BF16), jax.ShapeDtypeStruct((B, S, kw), BF16),
                 jax.ShapeDtypeStruct((B, S, vw), BF16), jax.ShapeDtypeStruct((B, S, SWA_WIDTH), BF16)]
    return pl.pallas_call(
        _swa_proj_kernel,
        grid=(B, S // tm),
        in_specs=[row(D_MODEL), full(win), full(ones)],
        out_specs=[row(SWA_WIDTH), row(kw), row(vw), row(SWA_WIDTH)],
        out_shape=out_shape,
        compiler_params=_params(("parallel", "parallel")),
        name="swa_proj",
    )(x, win, ones)


def _combine_pair(ext_even, ext_odd, extra_even=None, extra_odd=None):
    lo = _lane_lt_half(ext_even.shape)
    num = jnp.where(lo, ext_even, ext_odd)
    den = pltpu.roll(jnp.where(lo, ext_odd, ext_even), HALF, 1)
    if extra_even is not None:
        den = den + jnp.where(lo, extra_even, extra_odd)
    return num / den


def _causal_attn_kernel(q_ref, k_ref, v_ref, o_ref, m_ref, acc_ref):
    S = q_ref.shape[0]
    heads = range(ATT_HEADS)
    lanes = [slice(h * LANES, (h + 1) * LANES) for h in heads]

    def tile(q0, k0, jobs, diagonal):
        chains = [(h,) + job for job in jobs for h in heads]
        s_all = [lax.dot_general(q_ref[pl.ds(q0 + roff, nq), lanes[h]], k_ref[pl.ds(k0, nk), lanes[h]],
                                 (((1,), (1,)), ((), ())), preferred_element_type=F32)
                 for h, roff, nq, nk in chains]
        p_all, alpha_all = [], []
        for (h, roff, nq, nk), s in zip(chains, s_all):
            st = slice(roff, roff + nq)
            blocks = [s[:, j * LANES:(j + 1) * LANES] for j in range(nk // LANES)]
            if diagonal:
                row = lax.broadcasted_iota(jnp.int32, (nq, LANES), 0) + roff
                col = lax.broadcasted_iota(jnp.int32, (nq, LANES), 1)
                blocks = [b if (j + 1) * LANES - 1 <= roff else jnp.where(col + j * LANES <= row, b, NEG_INF)
                          for j, b in enumerate(blocks)]
            m_old = m_ref[h, st, :]
            m_new = jnp.maximum(m_old, jnp.max(functools.reduce(jnp.maximum, blocks), axis=-1, keepdims=True))
            p_all.append(jnp.concatenate([jnp.exp2(b - m_new) for b in blocks], axis=1).astype(BF16))
            alpha_all.append(jnp.exp2(m_old - m_new))
            m_ref[h, st, :] = m_new
        for (h, roff, nq, nk), p, alpha in zip(chains, p_all, alpha_all):
            st = slice(roff, roff + nq)
            pv = jnp.dot(p, v_ref[pl.ds(k0, nk), lanes[h]], preferred_element_type=F32)
            acc_ref[h, st, :] = acc_ref[h, st, :] * alpha + pv

    def q_body(qi, _):
        q0 = pl.multiple_of(qi * TQ, TQ)
        m_ref[...] = jnp.full(m_ref.shape, NEG_INF, F32)
        acc_ref[...] = jnp.zeros(acc_ref.shape, F32)

        def kv_body(kj, _):
            tile(q0, pl.multiple_of(kj * TK, TK), [(0, TQ, TK)], False)
            return 0

        lax.fori_loop(0, qi, kv_body, 0)
        half = TQ // 2
        tile(q0, q0, [(0, half, half), (half, half, TQ)], True)
        for p in range(ATT_HEADS // 2):
            o_ref[pl.ds(q0, TQ), p * LANES:(p + 1) * LANES] = _combine_pair(
                acc_ref[2 * p], acc_ref[2 * p + 1]).astype(BF16)
        return 0

    lax.fori_loop(0, S // TQ, q_body, 0)


def _causal_attn(q, k, v):
    B, S, _ = q.shape
    spec = pl.BlockSpec((None, S, ATT_HEADS * LANES), lambda b, p: (b, 0, p))
    return pl.pallas_call(
        _causal_attn_kernel,
        grid=(B, HEADS // ATT_HEADS),
        in_specs=[spec, spec, spec],
        out_specs=pl.BlockSpec((None, S, ATT_HEADS * HALF), lambda b, p: (b, 0, p)),
        out_shape=jax.ShapeDtypeStruct((B, S, HEADS * HALF), BF16),
        scratch_shapes=[pltpu.VMEM((ATT_HEADS, TQ, LANES), F32), pltpu.VMEM((ATT_HEADS, TQ, LANES), F32)],
        compiler_params=_params(("parallel", "parallel")),
        name="causal_attn",
    )(q, k, v)


def _swa_attn_kernel(sink_ref, q_ref, kp_ref, kc_ref, vp_ref, vc_ref, bias0_ref, bias_ref, o_ref):
    nqb = q_ref.shape[0] // BLOCK
    lo = _lane_lt_half((BLOCK, LANES))
    zero = jnp.zeros((), BF16)
    chains = [(g, par) for g in range(SWA_KV_HEADS) for par in range(2)]

    def band(prev_ref, cur_ref, i, cols):
        if i == 0:
            return jnp.concatenate([prev_ref[:, cols], cur_ref[0:BLOCK, cols]], axis=0)
        return cur_ref[(i - 1) * BLOCK:(i + 1) * BLOCK, cols]

    def qk(i):
        rows = slice(i * BLOCK, (i + 1) * BLOCK)
        out = []
        for g, par in chains:
            kb = band(kp_ref, kc_ref, i, slice(g * LANES, (g + 1) * LANES))
            qa = q_ref[rows, (2 * g) * LANES:(2 * g + 1) * LANES]
            qb = q_ref[rows, (2 * g + 1) * LANES:(2 * g + 2) * LANES]
            keep = lo if par == 0 else jnp.logical_not(lo)
            stack = jnp.concatenate([jnp.where(keep, qa, zero), jnp.where(keep, qb, zero)], axis=0)
            out.append(lax.dot_general(stack, kb, (((1,), (1,)), ((), ())), preferred_element_type=F32))
        return out

    def softmax(i, s_list):
        b_ref = bias0_ref if i == 0 else bias_ref
        out = []
        for (g, par), s in zip(chains, s_list):
            ha, hb = 4 * g + par, 4 * g + 2 + par
            blocks = [s[:, j * LANES:(j + 1) * LANES]
                      + jnp.concatenate([b_ref[ha, :, j * LANES:(j + 1) * LANES],
                                         b_ref[hb, :, j * LANES:(j + 1) * LANES]], axis=0) for j in range(2)]
            sink = jnp.concatenate([jnp.full((BLOCK, LANES), sink_ref[ha], F32),
                                    jnp.full((BLOCK, LANES), sink_ref[hb], F32)], axis=0)
            m = jnp.maximum(sink, jnp.max(jnp.maximum(blocks[0], blocks[1]), axis=-1, keepdims=True))
            p = jnp.concatenate([jnp.exp2(b - m) for b in blocks], axis=1).astype(BF16)
            out.append((p, jnp.exp2(sink - m)))
        return out

    def pv(i, sm):
        ext, extra = {}, {}
        for (g, par), (p, ex) in zip(chains, sm):
            vb = band(vp_ref, vc_ref, i, slice((2 * g + par) * LANES, (2 * g + par + 1) * LANES))
            ext[g, par] = jnp.dot(p, vb, preferred_element_type=F32)
            extra[g, par] = ex
        for g in range(SWA_KV_HEADS):
            for j in range(2):
                rows = slice(j * BLOCK, (j + 1) * BLOCK)
                o_ref[i * BLOCK:(i + 1) * BLOCK, (2 * g + j) * LANES:(2 * g + j + 1) * LANES] = _combine_pair(
                    ext[g, 0][rows], ext[g, 1][rows], extra[g, 0][rows], extra[g, 1][rows]).astype(BF16)

    s_cur = qk(0)
    for i in range(nqb):
        s_next = qk(i + 1) if i + 1 < nqb else None
        pv(i, softmax(i, s_cur))
        s_cur = s_next


def _swa_attn(sinks2, q, k2, vx2, bias):
    B, S, _ = q.shape
    rows = SWA_QB * BLOCK
    kw, vw = k2.shape[-1], vx2.shape[-1]
    cur = lambda w: pl.BlockSpec((None, rows, w), lambda b, n: (b, n, 0))
    prev = lambda w: pl.BlockSpec((None, BLOCK, w), lambda b, n: (b, jnp.maximum(n * SWA_QB - 1, 0), 0))
    bias_shape = (None, SWA_Q_HEADS, BLOCK, 2 * BLOCK)
    return pl.pallas_call(
        _swa_attn_kernel,
        grid=(B, S // rows),
        in_specs=[pl.BlockSpec(memory_space=pltpu.SMEM), cur(SWA_WIDTH), prev(kw), cur(kw), prev(vw), cur(vw),
                  pl.BlockSpec(bias_shape, lambda b, n: (jnp.minimum(n, 1), 0, 0, 0)),
                  pl.BlockSpec(bias_shape, lambda b, n: (1, 0, 0, 0))],
        out_specs=cur(SWA_WIDTH),
        out_shape=jax.ShapeDtypeStruct((B, S, SWA_WIDTH), BF16),
        compiler_params=_params(("parallel", "parallel")),
        name="swa_attn",
    )(sinks2, q, k2, k2, vx2, vx2, bias, bias)


def _out_kernel(o_ref, g_ref, x_ref, w_ref, lg_ref, lb_ref, y_ref):
    a = (o_ref[...].astype(F32) * g_ref[...].astype(F32)).astype(BF16)
    y = jnp.dot(a, w_ref[...], preferred_element_type=F32)
    z = DEEPNORM_ALPHA * x_ref[...] + y
    mu = jnp.mean(z, axis=-1, keepdims=True)
    zc = z - mu
    var = jnp.mean(zc * zc, axis=-1, keepdims=True)
    y_ref[...] = zc * lax.rsqrt(var + LN_EPS) * lg_ref[...] + lb_ref[...]


def _out_proj(o, sg, x, w, lg, lb):
    B, S, _ = x.shape
    tm = ROW_TILE
    row = pl.BlockSpec((None, tm, D_MODEL), lambda b, i: (b, i, 0))
    full = lambda a: pl.BlockSpec(a.shape, lambda b, i: (0,) * a.ndim)
    return pl.pallas_call(
        _out_kernel,
        grid=(B, S // tm),
        in_specs=[row, row, row, full(w), full(lg), full(lb)],
        out_specs=row,
        out_shape=jax.ShapeDtypeStruct((B, S, D_MODEL), F32),
        compiler_params=_params(("parallel", "parallel")),
        name="out_proj",
    )(o, sg, x, w, lg, lb)


def _parity_ones(n_groups, parity):
    lane_hi = (np.arange(LANES) >= HALF)[None, :]
    par = np.asarray(parity, dtype=bool)[:, None]
    return (lane_hi != par).astype(np.float32).reshape(1, n_groups * LANES)


def _rot(w):
    half = w.shape[-1] // 2
    return jnp.concatenate([-w[..., half:], w[..., :half]], axis=-1)


def _rope_tables(seq, c0):
    inv = ROPE_THETA ** (-np.arange(0, MLA_ROPE_DIM, 2, dtype=np.float64) / MLA_ROPE_DIM)
    ang = np.arange(seq, dtype=np.float64)[:, None] * inv[None, :]
    cos = np.concatenate([np.cos(ang)] * 2, axis=-1)
    sin = np.concatenate([np.sin(ang)] * 2, axis=-1)
    z32 = np.zeros((seq, LANES - HALF - MLA_ROPE_DIM))
    z64 = np.zeros((seq, HALF))
    aq = np.concatenate([np.full((seq, HALF), c0), c0 * cos, z32], axis=-1)
    bq = np.concatenate([z64, c0 * sin, z32], axis=-1)
    ak = np.concatenate([z64, cos, z32], axis=-1)
    bk = np.concatenate([z64, sin, z32], axis=-1)
    return tuple(t.astype(np.float32) for t in (aq, bq, ak, bk))


def _mla_layer_operands(w_in, q_norm, w_q_up, kv_norm, w_kv_up, seq):
    c1 = MLA_Q_RANK
    c2 = c1 + MLA_KV_RANK
    c3 = c2 + MLA_ROPE_DIM
    kr = w_in[:, c2:c3]
    krg = jnp.concatenate([jnp.zeros((D_MODEL, HALF), F32), kr, _rot(kr)], axis=-1)
    win = jnp.concatenate([w_in[:, :c2], krg, w_in[:, c3:]], axis=-1).astype(BF16)
    wq = w_q_up.reshape(MLA_Q_RANK, MLA_HEADS, MLA_NOPE_DIM + MLA_ROPE_DIM)
    nope, rope = wq[..., :MLA_NOPE_DIM], wq[..., MLA_NOPE_DIM:]
    wq = jnp.concatenate([nope, rope, _rot(rope)], axis=-1).reshape(MLA_Q_RANK, PADW).astype(BF16)
    wkv = w_kv_up.reshape(MLA_KV_RANK, MLA_HEADS, MLA_NOPE_DIM + MLA_V_DIM)
    knope, v = wkv[..., :MLA_NOPE_DIM], wkv[..., MLA_NOPE_DIM:]
    z = jnp.zeros_like(v)
    wk = jnp.concatenate([knope, z], axis=-1).reshape(MLA_KV_RANK, PADW).astype(BF16)
    odd = (jnp.arange(MLA_HEADS) % 2 == 1)[None, :, None]
    wv = jnp.where(odd, jnp.concatenate([z, v], axis=-1), jnp.concatenate([v, z], axis=-1))
    wv = wv.reshape(MLA_KV_RANK, PADW).astype(BF16)
    ones = _parity_ones(MLA_HEADS, [h % 2 for h in range(MLA_HEADS)])
    c0 = (MLA_NOPE_DIM + MLA_ROPE_DIM) ** -0.5 * LOG2E
    return (win, q_norm.reshape(1, -1), kv_norm.reshape(1, -1), wq, wk, wv, ones) + _rope_tables(seq, c0)


def _fox_layer_operands(w_in, b_f):
    W = FOX_WIDTH
    wf = jnp.concatenate([w_in[:, 3 * W:3 * W + FOX_HEADS], jnp.zeros((D_MODEL, LANES - FOX_HEADS), F32)], axis=-1)
    win = jnp.concatenate([w_in[:, :3 * W], w_in[:, 3 * W + FOX_HEADS:], wf], axis=-1).astype(BF16)
    bf = jnp.concatenate([b_f, jnp.zeros((LANES - FOX_HEADS,), F32)]).reshape(1, LANES)
    h = np.arange(FOX_HEADS)
    base = h * LANES + np.where(h % 2 == 0, HALF, 0)
    scat = np.zeros((LANES, 2 * PADW), np.float32)
    onesq = np.zeros((1, PADW), np.float32)
    onesk = np.zeros((1, PADW), np.float32)
    for j in range(3):
        scat[j * FOX_HEADS + h, base + j] = 1.0
        scat[j * FOX_HEADS + h, PADW + base + 3 + j] = -1.0
        onesq[0, base + 3 + j] = 1.0
        onesk[0, base + j] = 1.0
    return win, bf, jnp.asarray(scat, dtype=BF16), onesq, onesk


def _swa_layer_operands(w_in, sinks):
    c1 = SWA_WIDTH
    c2 = c1 + SWA_KV_WIDTH
    c3 = c2 + SWA_KV_WIDTH
    k = w_in[:, c1:c2].reshape(D_MODEL, SWA_KV_HEADS, SWA_HEAD_DIM)
    v = w_in[:, c2:c3].reshape(D_MODEL, SWA_KV_HEADS, SWA_HEAD_DIM)
    z = jnp.zeros_like(v)
    k2 = jnp.concatenate([k, k], axis=-1).reshape(D_MODEL, SWA_KV_HEADS * LANES)
    v2 = jnp.concatenate([v, z, z, v], axis=-1).reshape(D_MODEL, SWA_KV_HEADS * 2 * LANES)
    win = jnp.concatenate([w_in[:, :c1], k2, v2, w_in[:, c3:]], axis=-1).astype(BF16)
    ones = _parity_ones(2 * SWA_KV_HEADS, [i % 2 for i in range(2 * SWA_KV_HEADS)])
    qi = np.arange(BLOCK)[:, None]
    kj = np.arange(2 * BLOCK)[None, :]
    dist = qi + BLOCK - kj
    valid = (dist >= 0) & (dist < WINDOW)
    slopes = 2.0 ** (-8.0 * np.arange(1, SWA_Q_HEADS + 1, dtype=np.float64) / SWA_Q_HEADS)
    ali = -(slopes[:, None, None] * dist.astype(np.float64)[None]) * LOG2E
    bias_rest = np.where(valid[None], ali, NEG_INF)
    bias_first = np.where((valid & (kj >= BLOCK))[None], ali, NEG_INF)
    bias = np.stack([bias_first, bias_rest]).astype(np.float32)
    return win, ones, bias, sinks.astype(F32) * LOG2E


def kernel(x, ln_g, ln_b, mla_w_in, mla_q_norm, mla_w_q_up, mla_kv_norm, mla_w_kv_up, mla_w_out,
           swa_w_in, swa_sinks, swa_w_out, fox_w_in, fox_b_f, fox_w_out):
    seq = x.shape[1]
    for i in range(DEPTH):
        j = i // N_MIXERS
        kind = i % N_MIXERS
        if kind == 0:
            ops = _mla_layer_operands(mla_w_in[j], mla_q_norm[j], mla_w_q_up[j], mla_kv_norm[j], mla_w_kv_up[j], seq)
            q, k, v, sg = _mla_proj(x, *ops)
            o = _causal_attn(q, k, v)
            w_out = mla_w_out[j]
        elif kind == 1:
            win, ones, bias, sinks2 = _swa_layer_operands(swa_w_in[j], swa_sinks[j])
            q, k2, vx2, sg = _swa_proj(x, win, ones)
            o = _swa_attn(sinks2, q, k2, vx2, bias)
            w_out = swa_w_out[j]
        else:
            ops = _fox_layer_operands(fox_w_in[j], fox_b_f[j])
            q, k, v, sg = _fox_proj(x, *ops)
            o = _causal_attn(q, k, v)
            w_out = fox_w_out[j]
        x = _out_proj(o, sg, x, w_out.astype(BF16), ln_g[i].reshape(1, -1), ln_b[i].reshape(1, -1))
    return x
```

```python
import functools
import math

import jax
import jax.numpy as jnp
import numpy as np
from jax import lax
from jax.experimental import pallas as pl
from jax.experimental.pallas import tpu as pltpu

D_MODEL = 1024
DEPTH = 4
N_MIXERS = 3
BLOCK = 128
NEG_INF = -1e30
RMS_EPS = 1e-6
LN_EPS = 1e-5
DEEPNORM_ALPHA = (2 * DEPTH) ** 0.25
LOG2E = math.log2(math.e)

MLA_HEADS = 16
MLA_NOPE_DIM = 64
MLA_ROPE_DIM = 32
MLA_V_DIM = 64
MLA_Q_RANK = 384
MLA_KV_RANK = 256
ROPE_THETA = 10000.0
MLA_WIDTH = MLA_HEADS * MLA_V_DIM

SWA_Q_HEADS = 16
SWA_KV_HEADS = 4
SWA_HEAD_DIM = 64
WINDOW = 128
SWA_WIDTH = SWA_Q_HEADS * SWA_HEAD_DIM
SWA_KV_WIDTH = SWA_KV_HEADS * SWA_HEAD_DIM

FOX_HEADS = 16
FOX_HEAD_DIM = 64
FOX_WIDTH = FOX_HEADS * FOX_HEAD_DIM

LANES = 128
HALF = LANES // 2
HEADS = 16
PADW = HEADS * LANES

ROW_TILE = {"mla": 512, "swa": 512, "fox": 256, None: 512}
TQ = 512
TK = 512
ATT_HEADS = 8
SWA_QB = 4
VMEM_LIMIT = 56 * 1024 * 1024

F32 = jnp.float32
BF16 = jnp.bfloat16
NT_DIMS = (((1,), (1,)), ((), ()))


def _params(sem):
    return pltpu.CompilerParams(dimension_semantics=sem, vmem_limit_bytes=VMEM_LIMIT)


def _lane_lt_half(shape):
    return lax.broadcasted_iota(jnp.int32, shape, len(shape) - 1) < HALF


def _silu(g):
    return g / (1.0 + jnp.exp(-g))


def _mla_body(x, win_ref, qg_ref, kg_ref, wq_ref, wkv_ref, aqe_ref, bqe_ref, aqo_ref, bqo_ref, ak_ref, bk_ref,
              q_out, k_out, v_out, g_out):
    h = jnp.dot(x.astype(BF16), win_ref[...], preferred_element_type=F32)
    c1 = MLA_Q_RANK
    c2 = c1 + MLA_KV_RANK
    c3 = c2 + LANES
    cq, ckv, krg, gate = h[:, :c1], h[:, c1:c2], h[:, c2:c3], h[:, c3:]
    g_out[...] = _silu(gate).astype(BF16)

    qn = cq * lax.rsqrt(jnp.mean(cq * cq, axis=-1, keepdims=True) + RMS_EPS) * qg_ref[...]
    kn = ckv * lax.rsqrt(jnp.mean(ckv * ckv, axis=-1, keepdims=True) + RMS_EPS) * kg_ref[...]
    q = jnp.dot(qn.astype(BF16), wq_ref[...], preferred_element_type=F32)
    kv = jnp.dot(kn.astype(BF16), wkv_ref[...], preferred_element_type=F32)

    shift = LANES - MLA_ROPE_DIM
    aq = (aqe_ref[...], aqo_ref[...])
    bq = (bqe_ref[...], bqo_ref[...])
    kr_odd = krg * ak_ref[...] + pltpu.roll(krg, shift, 1) * bk_ref[...]
    kr = (pltpu.roll(kr_odd, HALF, 1), kr_odd)
    lo = _lane_lt_half((x.shape[0], LANES))
    for hh in range(HEADS):
        par = hh % 2
        sl = slice(hh * LANES, (hh + 1) * LANES)
        qh, kvh = q[:, sl], kv[:, sl]
        q_out[:, sl] = (qh * aq[par] + pltpu.roll(qh, shift, 1) * bq[par]).astype(BF16)
        if par == 0:
            k_out[:, sl] = jnp.where(lo, kr[0], kvh).astype(BF16)
            v_out[:, sl] = jnp.where(lo, kvh, 1.0).astype(BF16)
        else:
            k_out[:, sl] = jnp.where(lo, kvh, kr[1]).astype(BF16)
            v_out[:, sl] = jnp.where(lo, 1.0, kvh).astype(BF16)


def _split_pairs(nat, aug, out_ref):
    lo = _lane_lt_half((nat.shape[0], LANES))
    for p in range(HEADS // 2):
        blk = nat[:, p * LANES:(p + 1) * LANES]
        e = slice((2 * p) * LANES, (2 * p + 1) * LANES)
        o = slice((2 * p + 1) * LANES, (2 * p + 2) * LANES)
        fill_e = 1.0 if aug is None else aug[:, e]
        fill_o = 1.0 if aug is None else aug[:, o]
        out_ref[:, e] = jnp.where(lo, blk, fill_e).astype(BF16)
        out_ref[:, o] = jnp.where(lo, fill_o, blk).astype(BF16)


def _fox_body(x, win_ref, bf_ref, scat_ref, onesq_ref, onesk_ref, q_out, k_out, v_out, g_out, carry_ref):
    tm = x.shape[0]

    @pl.when(pl.program_id(1) == 0)
    def _():
        carry_ref[...] = jnp.zeros_like(carry_ref)

    h = jnp.dot(x.astype(BF16), win_ref[...], preferred_element_type=F32)
    W = FOX_WIDTH
    g_out[...] = _silu(h[:, 3 * W:4 * W]).astype(BF16)

    f = h[:, 4 * W:4 * W + LANES] + bf_ref[...]
    logf = jnp.minimum(f, 0.0) - jnp.log(1.0 + jnp.exp(-jnp.abs(f)))

    def split3(a):
        hi = a.astype(BF16)
        r = a - hi.astype(F32)
        mid = r.astype(BF16)
        lo = (r - mid.astype(F32)).astype(BF16)
        return hi, mid, lo

    ri = lax.broadcasted_iota(jnp.int32, (tm, tm), 0)
    ci = lax.broadcasted_iota(jnp.int32, (tm, tm), 1)
    tri = (ci <= ri).astype(BF16)
    l_hi, l_mid, l_lo = split3(logf)
    cum = (jnp.dot(tri, l_hi, preferred_element_type=F32) + jnp.dot(tri, l_mid, preferred_element_type=F32)
           + jnp.dot(tri, l_lo, preferred_element_type=F32)) + carry_ref[...]
    carry_ref[...] = cum[tm - 1:tm, :]

    c_hi, c_mid, c_lo = split3(cum * LOG2E)
    lane = lax.broadcasted_iota(jnp.int32, (tm, LANES), 1)
    packed = jnp.where(lane < HEADS, c_hi.astype(F32),
                       jnp.where(lane < 2 * HEADS, pltpu.roll(c_mid.astype(F32), HEADS, 1),
                                 pltpu.roll(c_lo.astype(F32), 2 * HEADS, 1)))
    packed = jnp.where(lane < 3 * HEADS, packed, 0.0).astype(BF16)
    aug = jnp.dot(packed, scat_ref[...], preferred_element_type=F32)
    augq = aug[:, :PADW] + onesq_ref[...]
    augk = aug[:, PADW:] + onesk_ref[...]

    _split_pairs(h[:, :W] * (FOX_HEAD_DIM ** -0.5 * LOG2E), augq, q_out)
    _split_pairs(h[:, W:2 * W], augk, k_out)
    _split_pairs(h[:, 2 * W:3 * W], None, v_out)


def _swa_body(x, win_ref, q_out, k_out, v_out, g_out):
    h = jnp.dot(x.astype(BF16), win_ref[...], preferred_element_type=F32)
    c1 = SWA_WIDTH
    c2 = c1 + SWA_KV_WIDTH
    c3 = c2 + SWA_KV_WIDTH
    q_out[...] = (h[:, :c1] * (SWA_HEAD_DIM ** -0.5 * LOG2E)).astype(BF16)
    g_out[...] = _silu(h[:, c3:]).astype(BF16)
    lo = _lane_lt_half((x.shape[0], LANES))
    for pair in range(SWA_KV_HEADS // 2):
        kp = h[:, c1 + pair * LANES:c1 + (pair + 1) * LANES]
        vp = h[:, c2 + pair * LANES:c2 + (pair + 1) * LANES]
        ks, vs = pltpu.roll(kp, HALF, 1), pltpu.roll(vp, HALF, 1)
        for t, (k_h, v_h) in enumerate(((jnp.where(lo, kp, ks), jnp.where(lo, vp, vs)),
                                        (jnp.where(lo, ks, kp), jnp.where(lo, vs, vp)))):
            g = 2 * pair + t
            k_out[:, g * LANES:(g + 1) * LANES] = k_h.astype(BF16)
            v_out[:, (2 * g) * LANES:(2 * g + 1) * LANES] = jnp.where(lo, v_h, 1.0).astype(BF16)
            v_out[:, (2 * g + 1) * LANES:(2 * g + 2) * LANES] = jnp.where(lo, 1.0, v_h).astype(BF16)


_PROJ = {
    "mla": (_mla_body, 11, (PADW, PADW, PADW, MLA_WIDTH), 6),
    "fox": (_fox_body, 5, (PADW, PADW, PADW, FOX_WIDTH), 0),
    "swa": (_swa_body, 1, (SWA_WIDTH, SWA_KV_HEADS * LANES, SWA_KV_HEADS * 2 * LANES, SWA_WIDTH), 0),
}


def _out_body(o_ref, g_ref, x_ref, w_ref, lg_ref, lb_ref):
    a = (o_ref[...].astype(F32) * g_ref[...].astype(F32)).astype(BF16)
    y = jnp.dot(a, w_ref[...], preferred_element_type=F32)
    z = DEEPNORM_ALPHA * x_ref[...] + y
    mu = jnp.mean(z, axis=-1, keepdims=True)
    zc = z - mu
    var = jnp.mean(zc * zc, axis=-1, keepdims=True)
    return zc * lax.rsqrt(var + LN_EPS) * lg_ref[...] + lb_ref[...]


def _boundary_kernel(*refs, with_out, kind):
    refs = list(refs)
    if with_out:
        out_in, refs = refs[:6], refs[6:]
    else:
        x_ref, refs = refs[0], refs[1:]
    n_ops = _PROJ[kind][1] if kind else 0
    proj_in, refs = refs[:n_ops], refs[n_ops:]
    if with_out:
        y_ref, refs = refs[0], refs[1:]
        x = _out_body(*out_in)
        y_ref[...] = x
    else:
        x = x_ref[...]
    if kind:
        _PROJ[kind][0](x, *proj_in, *refs)


def _boundary(x, out_ops, kind, proj_ops):
    B, S, _ = x.shape
    tm = ROW_TILE[kind]
    row = lambda w: pl.BlockSpec((None, tm, w), lambda b, i: (b, i, 0))
    full = lambda a: pl.BlockSpec(a.shape, lambda b, i: (0,) * a.ndim, pipeline_mode=pl.Buffered(1))
    tab = pl.BlockSpec((tm, LANES), lambda b, i: (i, 0))
    in_specs, args, out_specs, out_shape, scratch = [], [], [], [], []
    if out_ops is not None:
        o, sg, w, lg, lb = out_ops
        in_specs += [row(D_MODEL), row(D_MODEL), row(D_MODEL), full(w), full(lg), full(lb)]
        args += [o, sg, x, w, lg, lb]
        out_specs.append(row(D_MODEL))
        out_shape.append(jax.ShapeDtypeStruct((B, S, D_MODEL), F32))
    else:
        in_specs.append(row(D_MODEL))
        args.append(x)
    if kind:
        _, n_ops, widths, n_tab = _PROJ[kind]
        assert len(proj_ops) == n_ops
        in_specs += [full(a) for a in proj_ops[:n_ops - n_tab]] + [tab] * n_tab
        args += list(proj_ops)
        out_specs += [row(w) for w in widths]
        out_shape += [jax.ShapeDtypeStruct((B, S, w), BF16) for w in widths]
        if kind == "fox":
            scratch.append(pltpu.VMEM((1, LANES), F32))
    sem = ("parallel", "arbitrary" if kind == "fox" else "parallel")
    return pl.pallas_call(
        functools.partial(_boundary_kernel, with_out=out_ops is not None, kind=kind),
        grid=(B, S // tm),
        in_specs=in_specs,
        out_specs=out_specs,
        out_shape=out_shape,
        scratch_shapes=scratch,
        compiler_params=_params(sem),
        name="boundary_" + ("out_" if out_ops is not None else "") + (kind or "final"),
    )(*args)


def _normalize_pair(ext_even, ext_odd, extra_even=None, extra_odd=None):
    lo = _lane_lt_half(ext_even.shape)
    den = jnp.where(lo, ext_odd, ext_even)
    if extra_even is not None:
        den = den + jnp.where(lo, extra_odd, extra_even)
    return jnp.where(lo, ext_even, ext_odd) * pltpu.roll(1.0 / den, HALF, 1)


def _causal_attn_kernel(q_ref, k_ref, v_ref, o_ref, m_ref, acc_ref):
    S = q_ref.shape[0]
    heads = range(ATT_HEADS)
    lanes = [slice(h * LANES, (h + 1) * LANES) for h in heads]

    def tile(q0, k0, jobs, diagonal):
        chains = [(h,) + job for job in jobs for h in heads]
        s_all = [lax.dot_general(q_ref[pl.ds(q0 + roff, nq), lanes[h]], k_ref[pl.ds(k0, nk), lanes[h]],
                                 NT_DIMS, preferred_element_type=F32) for h, roff, nq, nk in chains]
        p_all, alpha_all = [], []
        for (h, roff, nq, nk), s in zip(chains, s_all):
            st = slice(roff, roff + nq)
            blocks = [s[:, j * LANES:(j + 1) * LANES] for j in range(nk // LANES)]
            if diagonal:
                row = lax.broadcasted_iota(jnp.int32, (nq, LANES), 0) + roff
                col = lax.broadcasted_iota(jnp.int32, (nq, LANES), 1)
                blocks = [b if (j + 1) * LANES - 1 <= roff else jnp.where(col + j * LANES <= row, b, NEG_INF)
                          for j, b in enumerate(blocks)]
            m_blk = jnp.max(functools.reduce(jnp.maximum, blocks), axis=-1, keepdims=True)
            if diagonal:
                m_new = jnp.broadcast_to(m_blk, (nq, LANES))
                alpha_all.append(None)
            else:
                m_old = m_ref[h, st, :]
                m_new = jnp.maximum(m_old, m_blk)
                alpha_all.append(jnp.exp2(m_old - m_new))
            p_all.append(jnp.concatenate([jnp.exp2(b - m_new) for b in blocks], axis=1).astype(BF16))
            m_ref[h, st, :] = m_new
        for (h, roff, nq, nk), p, alpha in zip(chains, p_all, alpha_all):
            st = slice(roff, roff + nq)
            pv = jnp.dot(p, v_ref[pl.ds(k0, nk), lanes[h]], preferred_element_type=F32)
            acc_ref[h, st, :] = pv if alpha is None else acc_ref[h, st, :] * alpha + pv

    def q_body(qi, _):
        q0 = pl.multiple_of(qi * TQ, TQ)
        half = TQ // 2
        tile(q0, q0, [(0, half, half), (half, half, TQ)], True)

        def kv_body(kj, _):
            tile(q0, pl.multiple_of(kj * TK, TK), [(0, TQ, TK)], False)
            return 0

        lax.fori_loop(0, qi, kv_body, 0)
        for p in range(ATT_HEADS // 2):
            o_ref[pl.ds(q0, TQ), p * LANES:(p + 1) * LANES] = _normalize_pair(
                acc_ref[2 * p], acc_ref[2 * p + 1]).astype(BF16)
        return 0

    lax.fori_loop(0, S // TQ, q_body, 0)


def _causal_attn(q, k, v):
    B, S, _ = q.shape
    spec = pl.BlockSpec((None, S, ATT_HEADS * LANES), lambda b, p: (b, 0, p))
    return pl.pallas_call(
        _causal_attn_kernel,
        grid=(B, HEADS // ATT_HEADS),
        in_specs=[spec, spec, spec],
        out_specs=pl.BlockSpec((None, S, ATT_HEADS * HALF), lambda b, p: (b, 0, p)),
        out_shape=jax.ShapeDtypeStruct((B, S, HEADS * HALF), BF16),
        scratch_shapes=[pltpu.VMEM((ATT_HEADS, TQ, LANES), F32), pltpu.VMEM((ATT_HEADS, TQ, LANES), F32)],
        compiler_params=_params(("parallel", "parallel")),
        name="causal_attn",
    )(q, k, v)


def _swa_attn_kernel(sink_ref, q_ref, kp_ref, kc_ref, vp_ref, vc_ref, bias0_ref, bias_ref, o_ref):
    nqb = q_ref.shape[0] // BLOCK
    lo = _lane_lt_half((BLOCK, LANES))
    zero = jnp.zeros((), BF16)
    chains = [(g, par) for g in range(SWA_KV_HEADS) for par in range(2)]

    def band(prev_ref, cur_ref, i, group):
        cols = slice(group * LANES, (group + 1) * LANES)
        if i == 0:
            return jnp.concatenate([prev_ref[:, cols], cur_ref[0:BLOCK, cols]], axis=0)
        return cur_ref[(i - 1) * BLOCK:(i + 1) * BLOCK, cols]

    def qk(i):
        rows = slice(i * BLOCK, (i + 1) * BLOCK)
        out = []
        for g in range(SWA_KV_HEADS):
            kb = band(kp_ref, kc_ref, i, g)
            qa = q_ref[rows, (2 * g) * LANES:(2 * g + 1) * LANES]
            qb = q_ref[rows, (2 * g + 1) * LANES:(2 * g + 2) * LANES]
            for par in range(2):
                keep = lo if par == 0 else jnp.logical_not(lo)
                stack = jnp.concatenate([jnp.where(keep, qa, zero), jnp.where(keep, qb, zero)], axis=0)
                out.append(lax.dot_general(stack, kb, NT_DIMS, preferred_element_type=F32))
        return out

    def softmax(i, s_list):
        b_ref = bias0_ref if i == 0 else bias_ref
        out = []
        for (g, par), s in zip(chains, s_list):
            ha, hb = 4 * g + par, 4 * g + 2 + par
            blocks = [s[:, j * LANES:(j + 1) * LANES]
                      + jnp.concatenate([b_ref[ha, :, j * LANES:(j + 1) * LANES],
                                         b_ref[hb, :, j * LANES:(j + 1) * LANES]], axis=0) for j in range(2)]
            sink = jnp.concatenate([jnp.full((BLOCK, LANES), sink_ref[ha], F32),
                                    jnp.full((BLOCK, LANES), sink_ref[hb], F32)], axis=0)
            m = jnp.maximum(sink, jnp.max(jnp.maximum(blocks[0], blocks[1]), axis=-1, keepdims=True))
            p = jnp.concatenate([jnp.exp2(b - m) for b in blocks], axis=1).astype(BF16)
            out.append((p, jnp.exp2(sink - m)))
        return out

    def pv(i, sm):
        ext, extra = {}, {}
        for (g, par), (p, ex) in zip(chains, sm):
            vb = band(vp_ref, vc_ref, i, 2 * g + par)
            ext[g, par] = jnp.dot(p, vb, preferred_element_type=F32)
            extra[g, par] = ex
        for g in range(SWA_KV_HEADS):
            for j in range(2):
                rows = slice(j * BLOCK, (j + 1) * BLOCK)
                o_ref[i * BLOCK:(i + 1) * BLOCK, (2 * g + j) * LANES:(2 * g + j + 1) * LANES] = _normalize_pair(
                    ext[g, 0][rows], ext[g, 1][rows], extra[g, 0][rows], extra[g, 1][rows]).astype(BF16)

    s_cur = qk(0)
    for i in range(nqb):
        s_next = qk(i + 1) if i + 1 < nqb else None
        pv(i, softmax(i, s_cur))
        s_cur = s_next


def _swa_attn(sinks2, q, k, v, bias):
    B, S, _ = q.shape
    rows = SWA_QB * BLOCK
    cur = lambda w: pl.BlockSpec((None, rows, w), lambda b, n: (b, n, 0))
    prev = lambda w: pl.BlockSpec((None, BLOCK, w), lambda b, n: (b, jnp.maximum(n * SWA_QB - 1, 0), 0))
    bias_shape = (None, SWA_Q_HEADS, BLOCK, 2 * BLOCK)
    kw, vw = k.shape[-1], v.shape[-1]
    return pl.pallas_call(
        _swa_attn_kernel,
        grid=(B, S // rows),
        in_specs=[pl.BlockSpec(memory_space=pltpu.SMEM), cur(SWA_WIDTH), prev(kw), cur(kw), prev(vw), cur(vw),
                  pl.BlockSpec(bias_shape, lambda b, n: (jnp.minimum(n, 1), 0, 0, 0)),
                  pl.BlockSpec(bias_shape, lambda b, n: (1, 0, 0, 0))],
        out_specs=cur(SWA_WIDTH),
        out_shape=jax.ShapeDtypeStruct((B, S, SWA_WIDTH), BF16),
        compiler_params=_params(("parallel", "parallel")),
        name="swa_attn",
    )(sinks2, q, k, k, v, v, bias, bias)


def _rot(w):
    half = w.shape[-1] // 2
    return jnp.concatenate([-w[..., half:], w[..., :half]], axis=-1)


def _rope_tables(seq, c0):
    inv = ROPE_THETA ** (-np.arange(0, MLA_ROPE_DIM, 2, dtype=np.float64) / MLA_ROPE_DIM)
    ang = np.arange(seq, dtype=np.float64)[:, None] * inv[None, :]
    cos = np.concatenate([np.cos(ang)] * 2, axis=-1)
    sin = np.concatenate([np.sin(ang)] * 2, axis=-1)
    z32 = np.zeros((seq, LANES - HALF - MLA_ROPE_DIM))
    z64 = np.zeros((seq, HALF))
    c64 = np.full((seq, HALF), c0)
    aq_even = np.concatenate([c0 * cos, z32, c64], axis=-1)
    bq_even = np.concatenate([c0 * sin, z32, z64], axis=-1)
    aq_odd = np.concatenate([c64, c0 * cos, z32], axis=-1)
    bq_odd = np.concatenate([z64, c0 * sin, z32], axis=-1)
    ak = np.concatenate([z64, cos, z32], axis=-1)
    bk = np.concatenate([z64, sin, z32], axis=-1)
    return tuple(t.astype(np.float32) for t in (aq_even, bq_even, aq_odd, bq_odd, ak, bk))


def _mla_layer_operands(w_in, q_norm, w_q_up, kv_norm, w_kv_up, seq):
    c2 = MLA_Q_RANK + MLA_KV_RANK
    c3 = c2 + MLA_ROPE_DIM
    kr = w_in[:, c2:c3]
    krg = jnp.concatenate([jnp.zeros((D_MODEL, HALF), F32), kr, _rot(kr)], axis=-1)
    win = jnp.concatenate([w_in[:, :c2], krg, w_in[:, c3:]], axis=-1).astype(BF16)
    odd = (jnp.arange(MLA_HEADS) % 2 == 1)[None, :, None]
    wq = w_q_up.reshape(MLA_Q_RANK, MLA_HEADS, MLA_NOPE_DIM + MLA_ROPE_DIM)
    nope, rope = wq[..., :MLA_NOPE_DIM], wq[..., MLA_NOPE_DIM:]
    wq = jnp.where(odd, jnp.concatenate([nope, rope, _rot(rope)], axis=-1),
                   jnp.concatenate([rope, _rot(rope), nope], axis=-1))
    wq = wq.reshape(MLA_Q_RANK, PADW).astype(BF16)
    wkv = w_kv_up.reshape(MLA_KV_RANK, MLA_HEADS, MLA_NOPE_DIM + MLA_V_DIM)
    wkv = jnp.where(odd, wkv, jnp.concatenate([wkv[..., MLA_NOPE_DIM:], wkv[..., :MLA_NOPE_DIM]], axis=-1))
    wkv = wkv.reshape(MLA_KV_RANK, PADW).astype(BF16)
    c0 = (MLA_NOPE_DIM + MLA_ROPE_DIM) ** -0.5 * LOG2E
    return (win, q_norm.reshape(1, -1), kv_norm.reshape(1, -1), wq, wkv) + _rope_tables(seq, c0)


def _fox_layer_operands(w_in, b_f):
    W = FOX_WIDTH
    wf = jnp.concatenate([w_in[:, 3 * W:3 * W + FOX_HEADS], jnp.zeros((D_MODEL, LANES - FOX_HEADS), F32)], axis=-1)
    win = jnp.concatenate([w_in[:, :3 * W], w_in[:, 3 * W + FOX_HEADS:], wf], axis=-1).astype(BF16)
    bf = jnp.concatenate([b_f, jnp.zeros((LANES - FOX_HEADS,), F32)]).reshape(1, LANES)
    h = np.arange(FOX_HEADS)
    base = h * LANES + np.where(h % 2 == 0, HALF, 0)
    scat = np.zeros((LANES, 2 * PADW), np.float32)
    onesq = np.zeros((1, PADW), np.float32)
    onesk = np.zeros((1, PADW), np.float32)
    for j in range(3):
        scat[j * FOX_HEADS + h, base + j] = 1.0
        scat[j * FOX_HEADS + h, PADW + base + 3 + j] = -1.0
        onesq[0, base + 3 + j] = 1.0
        onesk[0, base + j] = 1.0
    return win, bf, jnp.asarray(scat, dtype=BF16), onesq, onesk


def _swa_bias():
    qi = np.arange(BLOCK)[:, None]
    kj = np.arange(2 * BLOCK)[None, :]
    dist = qi + BLOCK - kj
    valid = (dist >= 0) & (dist < WINDOW)
    slopes = 2.0 ** (-8.0 * np.arange(1, SWA_Q_HEADS + 1, dtype=np.float64) / SWA_Q_HEADS)
    ali = -(slopes[:, None, None] * dist.astype(np.float64)[None]) * LOG2E
    bias_rest = np.where(valid[None], ali, NEG_INF)
    bias_first = np.where((valid & (kj >= BLOCK))[None], ali, NEG_INF)
    return np.stack([bias_first, bias_rest]).astype(np.float32)


def kernel(x, ln_g, ln_b, mla_w_in, mla_q_norm, mla_w_q_up, mla_kv_norm, mla_w_kv_up, mla_w_out,
           swa_w_in, swa_sinks, swa_w_out, fox_w_in, fox_b_f, fox_w_out):
    seq = x.shape[1]
    out_ops = None
    for i in range(DEPTH + 1):
        j = i // N_MIXERS
        kind = ("mla", "swa", "fox")[i % N_MIXERS] if i < DEPTH else None
        if kind == "mla":
            proj_ops = _mla_layer_operands(mla_w_in[j], mla_q_norm[j], mla_w_q_up[j], mla_kv_norm[j],
                                           mla_w_kv_up[j], seq)
        elif kind == "swa":
            proj_ops = (swa_w_in[j].astype(BF16),)
        elif kind == "fox":
            proj_ops = _fox_layer_operands(fox_w_in[j], fox_b_f[j])
        else:
            proj_ops = ()
        res = _boundary(x, out_ops, kind, proj_ops)
        if out_ops is not None:
            x, res = res[0], res[1:]
        if kind is None:
            return x
        q, k, v, sg = res
        if kind == "mla":
            o = _causal_attn(q, k, v)
            w_out = mla_w_out[j]
        elif kind == "swa":
            o = _swa_attn(swa_sinks[j].astype(F32) * LOG2E, q, k, v, _swa_bias())
            w_out = swa_w_out[j]
        else:
            o = _causal_attn(q, k, v)
            w_out = fox_w_out[j]
        out_ops = (o, sg, w_out.astype(BF16), ln_g[i].reshape(1, -1), ln_b[i].reshape(1, -1))
```

```python
import functools
import math

import jax
import jax.numpy as jnp
import numpy as np
from jax import lax
from jax.experimental import pallas as pl
from jax.experimental.pallas import tpu as pltpu

D_MODEL = 1024
DEPTH = 4
N_MIXERS = 3
BLOCK = 128
NEG_INF = -1e30
RMS_EPS = 1e-6
LN_EPS = 1e-5
DEEPNORM_ALPHA = (2 * DEPTH) ** 0.25
LOG2E = math.log2(math.e)

MLA_HEADS = 16
MLA_NOPE_DIM = 64
MLA_ROPE_DIM = 32
MLA_V_DIM = 64
MLA_Q_RANK = 384
MLA_KV_RANK = 256
ROPE_THETA = 10000.0
MLA_WIDTH = MLA_HEADS * MLA_V_DIM

SWA_Q_HEADS = 16
SWA_KV_HEADS = 4
SWA_HEAD_DIM = 64
WINDOW = 128
SWA_WIDTH = SWA_Q_HEADS * SWA_HEAD_DIM
SWA_KV_WIDTH = SWA_KV_HEADS * SWA_HEAD_DIM

FOX_HEADS = 16
FOX_HEAD_DIM = 64
FOX_WIDTH = FOX_HEADS * FOX_HEAD_DIM

LANES = 128
HALF = LANES // 2
HEADS = 16
PADW = HEADS * LANES

ROW_TILE = {"mla": 512, "swa": 512, "fox": 512, None: 512}
TQ = 512
TK = 512
ATT_HEADS = 8
SWA_QB = 4
VMEM_LIMIT = 56 * 1024 * 1024
ATT_FLAGS = None

F32 = jnp.float32
BF16 = jnp.bfloat16
NT_DIMS = (((1,), (1,)), ((), ()))


def _params(sem, flags=None):
    return pltpu.CompilerParams(dimension_semantics=sem, vmem_limit_bytes=VMEM_LIMIT, flags=flags)


def _lane_lt_half(shape):
    return lax.broadcasted_iota(jnp.int32, shape, len(shape) - 1) < HALF


def _silu(g):
    return g / (1.0 + jnp.exp(-g))


def _write_vt(vt, vt_out):
    top = lax.broadcasted_iota(jnp.int32, (LANES, vt.shape[1]), 0) < HALF
    for p in range(HEADS // 2):
        blk = vt[p * LANES:(p + 1) * LANES, :]
        vt_out[(2 * p) * LANES:(2 * p + 1) * LANES, :] = jnp.where(top, blk, 1.0).astype(BF16)
        vt_out[(2 * p + 1) * LANES:(2 * p + 2) * LANES, :] = jnp.where(top, 1.0, blk).astype(BF16)


def _mla_body(x, win_ref, qg_ref, kg_ref, wq_ref, wk_ref, wvt_ref, aqe_ref, bqe_ref, aqo_ref, bqo_ref,
              ak_ref, bk_ref, q_out, k_out, vt_out, g_out):
    h = jnp.dot(x.astype(BF16), win_ref[...], preferred_element_type=F32)
    c1 = MLA_Q_RANK
    c2 = c1 + MLA_KV_RANK
    c3 = c2 + LANES
    cq, ckv, krg, gate = h[:, :c1], h[:, c1:c2], h[:, c2:c3], h[:, c3:]
    g_out[...] = _silu(gate).astype(BF16)

    qn = cq * lax.rsqrt(jnp.mean(cq * cq, axis=-1, keepdims=True) + RMS_EPS) * qg_ref[...]
    kn = (ckv * lax.rsqrt(jnp.mean(ckv * ckv, axis=-1, keepdims=True) + RMS_EPS) * kg_ref[...]).astype(BF16)
    q = jnp.dot(qn.astype(BF16), wq_ref[...], preferred_element_type=F32)
    kc = jnp.dot(kn, wk_ref[...], preferred_element_type=F32)
    _write_vt(lax.dot_general(wvt_ref[...], kn, NT_DIMS, preferred_element_type=F32), vt_out)

    shift = LANES - MLA_ROPE_DIM
    aq = (aqe_ref[...], aqo_ref[...])
    bq = (bqe_ref[...], bqo_ref[...])
    kr_hi = krg * ak_ref[...] + pltpu.roll(krg, shift, 1) * bk_ref[...]
    kr_lo = pltpu.roll(kr_hi, HALF, 1)
    lo = _lane_lt_half((x.shape[0], LANES))
    for p in range(HEADS // 2):
        e = slice((2 * p) * LANES, (2 * p + 1) * LANES)
        o = slice((2 * p + 1) * LANES, (2 * p + 2) * LANES)
        kblk = kc[:, p * LANES:(p + 1) * LANES]
        q_out[:, e] = (q[:, e] * aq[0] + pltpu.roll(q[:, e], shift, 1) * bq[0]).astype(BF16)
        q_out[:, o] = (q[:, o] * aq[1] + pltpu.roll(q[:, o], shift, 1) * bq[1]).astype(BF16)
        k_out[:, e] = jnp.where(lo, kblk, kr_hi).astype(BF16)
        k_out[:, o] = jnp.where(lo, kr_lo, kblk).astype(BF16)


def _split_pairs(nat, aug, out_ref):
    lo = _lane_lt_half((nat.shape[0], LANES))
    for p in range(HEADS // 2):
        blk = nat[:, p * LANES:(p + 1) * LANES]
        fill = aug[:, p * LANES:(p + 1) * LANES]
        out_ref[:, (2 * p) * LANES:(2 * p + 1) * LANES] = jnp.where(lo, blk, fill).astype(BF16)
        out_ref[:, (2 * p + 1) * LANES:(2 * p + 2) * LANES] = jnp.where(lo, fill, blk).astype(BF16)


def _fox_body(x, win_ref, wvt_ref, bf_ref, scat_ref, onesq_ref, onesk_ref, q_out, k_out, vt_out, g_out, carry_ref):
    tm = x.shape[0]

    @pl.when(pl.program_id(1) == 0)
    def _():
        carry_ref[...] = jnp.zeros_like(carry_ref)

    xb = x.astype(BF16)
    h = jnp.dot(xb, win_ref[...], preferred_element_type=F32)
    W = FOX_WIDTH
    g_out[...] = _silu(h[:, 2 * W:3 * W]).astype(BF16)
    _write_vt(lax.dot_general(wvt_ref[...], xb, NT_DIMS, preferred_element_type=F32), vt_out)

    f = h[:, 3 * W:3 * W + LANES] + bf_ref[...]
    logf = jnp.minimum(f, 0.0) - jnp.log(1.0 + jnp.exp(-jnp.abs(f)))

    def split3(a):
        hi = a.astype(BF16)
        r = a - hi.astype(F32)
        mid = r.astype(BF16)
        lo = (r - mid.astype(F32)).astype(BF16)
        return hi, mid, lo

    ri = lax.broadcasted_iota(jnp.int32, (tm, tm), 0)
    ci = lax.broadcasted_iota(jnp.int32, (tm, tm), 1)
    tri = (ci <= ri).astype(BF16)
    l_hi, l_mid, l_lo = split3(logf)
    cum = (jnp.dot(tri, l_hi, preferred_element_type=F32) + jnp.dot(tri, l_mid, preferred_element_type=F32)
           + jnp.dot(tri, l_lo, preferred_element_type=F32)) + carry_ref[...]
    carry_ref[...] = cum[tm - 1:tm, :]

    c_hi, c_mid, c_lo = split3(cum * LOG2E)
    lane = lax.broadcasted_iota(jnp.int32, (tm, LANES), 1)
    packed = jnp.where(lane < HEADS, c_hi.astype(F32),
                       jnp.where(lane < 2 * HEADS, pltpu.roll(c_mid.astype(F32), HEADS, 1),
                                 pltpu.roll(c_lo.astype(F32), 2 * HEADS, 1)))
    packed = jnp.where(lane < 3 * HEADS, packed, 0.0).astype(BF16)
    aug = jnp.dot(packed, scat_ref[...], preferred_element_type=F32)
    augq = aug[:, :W] + onesq_ref[...]
    augk = aug[:, W:] + onesk_ref[...]

    _split_pairs(h[:, :W] * (FOX_HEAD_DIM ** -0.5 * LOG2E), augq, q_out)
    _split_pairs(h[:, W:2 * W], augk, k_out)


def _swa_body(x, win_ref, q_out, k_out, v_out, g_out):
    h = jnp.dot(x.astype(BF16), win_ref[...], preferred_element_type=F32)
    c1 = SWA_WIDTH
    c2 = c1 + SWA_KV_WIDTH
    c3 = c2 + SWA_KV_WIDTH
    q_out[...] = (h[:, :c1] * (SWA_HEAD_DIM ** -0.5 * LOG2E)).astype(BF16)
    g_out[...] = _silu(h[:, c3:]).astype(BF16)
    lo = _lane_lt_half((x.shape[0], LANES))
    for pair in range(SWA_KV_HEADS // 2):
        kp = h[:, c1 + pair * LANES:c1 + (pair + 1) * LANES]
        vp = h[:, c2 + pair * LANES:c2 + (pair + 1) * LANES]
        ks, vs = pltpu.roll(kp, HALF, 1), pltpu.roll(vp, HALF, 1)
        for t, (k_h, v_h) in enumerate(((jnp.where(lo, kp, ks), jnp.where(lo, vp, vs)),
                                        (jnp.where(lo, ks, kp), jnp.where(lo, vs, vp)))):
            g = 2 * pair + t
            k_out[:, g * LANES:(g + 1) * LANES] = k_h.astype(BF16)
            v_out[:, (2 * g) * LANES:(2 * g + 1) * LANES] = jnp.where(lo, v_h, 1.0).astype(BF16)
            v_out[:, (2 * g + 1) * LANES:(2 * g + 2) * LANES] = jnp.where(lo, 1.0, v_h).astype(BF16)


VT = "vt"
_PROJ = {
    "mla": (_mla_body, 12, (PADW, PADW, VT, MLA_WIDTH), 6),
    "fox": (_fox_body, 6, (PADW, PADW, VT, FOX_WIDTH), 0),
    "swa": (_swa_body, 1, (SWA_WIDTH, SWA_KV_HEADS * LANES, SWA_KV_HEADS * 2 * LANES, SWA_WIDTH), 0),
}


def _out_body(o_ref, g_ref, x_ref, w_ref, lg_ref, lb_ref):
    a = (o_ref[...].astype(F32) * g_ref[...].astype(F32)).astype(BF16)
    y = jnp.dot(a, w_ref[...], preferred_element_type=F32)
    z = DEEPNORM_ALPHA * x_ref[...] + y
    mu = jnp.mean(z, axis=-1, keepdims=True)
    zc = z - mu
    var = jnp.mean(zc * zc, axis=-1, keepdims=True)
    return zc * lax.rsqrt(var + LN_EPS) * lg_ref[...] + lb_ref[...]


def _boundary_kernel(*refs, with_out, kind):
    refs = list(refs)
    if with_out:
        out_in, refs = refs[:6], refs[6:]
    else:
        x_ref, refs = refs[0], refs[1:]
    n_ops = _PROJ[kind][1] if kind else 0
    proj_in, refs = refs[:n_ops], refs[n_ops:]
    if with_out:
        y_ref, refs = refs[0], refs[1:]
        x = _out_body(*out_in)
        y_ref[...] = x
    else:
        x = x_ref[...]
    if kind:
        _PROJ[kind][0](x, *proj_in, *refs)


def _boundary(x, out_ops, kind, proj_ops):
    B, S, _ = x.shape
    tm = ROW_TILE[kind]
    row = lambda w: pl.BlockSpec((None, tm, w), lambda b, i: (b, i, 0))
    full = lambda a: pl.BlockSpec(a.shape, lambda b, i: (0,) * a.ndim, pipeline_mode=pl.Buffered(1))
    tab = pl.BlockSpec((tm, LANES), lambda b, i: (i, 0))
    in_specs, args, out_specs, out_shape, scratch = [], [], [], [], []
    if out_ops is not None:
        o, sg, w, lg, lb = out_ops
        in_specs += [row(D_MODEL), row(D_MODEL), row(D_MODEL), full(w), full(lg), full(lb)]
        args += [o, sg, x, w, lg, lb]
        out_specs.append(row(D_MODEL))
        out_shape.append(jax.ShapeDtypeStruct((B, S, D_MODEL), F32))
    else:
        in_specs.append(row(D_MODEL))
        args.append(x)
    if kind:
        _, n_ops, widths, n_tab = _PROJ[kind]
        assert len(proj_ops) == n_ops
        in_specs += [full(a) for a in proj_ops[:n_ops - n_tab]] + [tab] * n_tab
        args += list(proj_ops)
        for w in widths:
            if w == VT:
                assert TK % tm == 0
                out_specs.append(pl.BlockSpec((None, None, PADW, tm),
                                              lambda b, i: (b, i * tm // TK, 0, i % (TK // tm))))
                out_shape.append(jax.ShapeDtypeStruct((B, S // TK, PADW, TK), BF16))
            else:
                out_specs.append(row(w))
                out_shape.append(jax.ShapeDtypeStruct((B, S, w), BF16))
        if kind == "fox":
            scratch.append(pltpu.VMEM((1, LANES), F32))
    sem = ("parallel", "arbitrary" if kind == "fox" else "parallel")
    return pl.pallas_call(
        functools.partial(_boundary_kernel, with_out=out_ops is not None, kind=kind),
        grid=(B, S // tm),
        in_specs=in_specs,
        out_specs=out_specs,
        out_shape=out_shape,
        scratch_shapes=scratch,
        compiler_params=_params(sem),
        name="boundary_" + ("out_" if out_ops is not None else "") + (kind or "final"),
    )(*args)


def _normalize_pair(ext_even, ext_odd, extra_even=None, extra_odd=None):
    lo = _lane_lt_half(ext_even.shape)
    den = jnp.where(lo, ext_odd, ext_even)
    if extra_even is not None:
        den = den + jnp.where(lo, extra_odd, extra_even)
    return jnp.where(lo, ext_even, ext_odd) * pltpu.roll(1.0 / den, HALF, 1)


def _causal_attn_kernel(q_ref, k_ref, vt_ref, o_ref, m_ref, acc_ref):
    S = q_ref.shape[0]
    heads = range(ATT_HEADS)
    lanes = [slice(h * LANES, (h + 1) * LANES) for h in heads]

    def tile(q0, kt, jobs, diagonal):
        chains = [(h,) + job for job in jobs for h in heads]
        k0 = pl.multiple_of(kt * TK, TK)
        st_all = [lax.dot_general(k_ref[pl.ds(k0, nk), lanes[h]], q_ref[pl.ds(q0 + coff, nq), lanes[h]],
                                  NT_DIMS, preferred_element_type=F32) for h, coff, nq, nk in chains]
        pt_all, alpha_all = [], []
        for (h, coff, nq, nk), st in zip(chains, st_all):
            cols = slice(coff, coff + nq)
            blocks = [st[:, j * LANES:(j + 1) * LANES] for j in range(nq // LANES)]
            if diagonal:
                key = lax.broadcasted_iota(jnp.int32, (nk, LANES), 0)
                qry = lax.broadcasted_iota(jnp.int32, (nk, LANES), 1) + coff
                blocks = [b if nk - 1 <= coff + j * LANES else jnp.where(key <= qry + j * LANES, b, NEG_INF)
                          for j, b in enumerate(blocks)]
            m_blk = jnp.concatenate([jnp.max(b, axis=0, keepdims=True) for b in blocks], axis=1)
            if diagonal:
                m_new = m_blk
                alpha_all.append(None)
            else:
                m_old = m_ref[h, :, cols]
                m_new = jnp.maximum(m_old, m_blk)
                alpha_all.append(jnp.exp2(m_old - m_new))
            pt_all.append(jnp.concatenate(
                [jnp.exp2(b - m_new[:, j * LANES:(j + 1) * LANES]) for j, b in enumerate(blocks)],
                axis=1).astype(BF16))
            m_ref[h, :, cols] = m_new
        for (h, coff, nq, nk), pt, alpha in zip(chains, pt_all, alpha_all):
            cols = slice(coff, coff + nq)
            pv = jnp.dot(vt_ref[kt, lanes[h], 0:nk], pt, preferred_element_type=F32)
            acc_ref[h, :, cols] = pv if alpha is None else acc_ref[h, :, cols] * alpha + pv

    def q_body(qi, _):
        q0 = pl.multiple_of(qi * TQ, TQ)
        half = TQ // 2
        tile(q0, qi, [(0, half, half), (half, half, TQ)], True)

        def kv_body(kj, _):
            tile(q0, kj, [(0, TQ, TK)], False)
            return 0

        lax.fori_loop(0, qi, kv_body, 0)
        sub = 8
        for p in range(ATT_HEADS // 2):
            acc_e, acc_o = acc_ref[2 * p], acc_ref[2 * p + 1]
            r_e = jnp.concatenate([1.0 / acc_e[HALF:HALF + sub]] * (HALF // sub), axis=0)
            r_o = jnp.concatenate([1.0 / acc_o[0:sub]] * (HALF // sub), axis=0)
            pair_t = jnp.concatenate([acc_e[:HALF] * r_e, acc_o[HALF:] * r_o], axis=0)
            o_ref[pl.ds(q0, TQ), p * LANES:(p + 1) * LANES] = pair_t.T.astype(BF16)
        return 0

    lax.fori_loop(0, S // TQ, q_body, 0)


def _causal_attn(q, k, vt):
    B, S, _ = q.shape
    spec = pl.BlockSpec((None, S, ATT_HEADS * LANES), lambda b, p: (b, 0, p))
    return pl.pallas_call(
        _causal_attn_kernel,
        grid=(B, HEADS // ATT_HEADS),
        in_specs=[spec, spec, pl.BlockSpec((None, S // TK, ATT_HEADS * LANES, TK), lambda b, p: (b, 0, p, 0))],
        out_specs=pl.BlockSpec((None, S, ATT_HEADS * HALF), lambda b, p: (b, 0, p)),
        out_shape=jax.ShapeDtypeStruct((B, S, HEADS * HALF), BF16),
        scratch_shapes=[pltpu.VMEM((ATT_HEADS, 1, TQ), F32), pltpu.VMEM((ATT_HEADS, LANES, TQ), F32)],
        compiler_params=_params(("parallel", "parallel"), ATT_FLAGS),
        name="causal_attn",
    )(q, k, vt)


def _swa_attn_kernel(sink_ref, q_ref, kp_ref, kc_ref, vp_ref, vc_ref, bias0_ref, bias_ref, o_ref):
    nqb = q_ref.shape[0] // BLOCK
    lo = _lane_lt_half((BLOCK, LANES))
    zero = jnp.zeros((), BF16)
    chains = [(g, par) for g in range(SWA_KV_HEADS) for par in range(2)]

    def band(prev_ref, cur_ref, i, group):
        cols = slice(group * LANES, (group + 1) * LANES)
        if i == 0:
            return jnp.concatenate([prev_ref[:, cols], cur_ref[0:BLOCK, cols]], axis=0)
        return cur_ref[(i - 1) * BLOCK:(i + 1) * BLOCK, cols]

    def qk(i):
        rows = slice(i * BLOCK, (i + 1) * BLOCK)
        out = []
        for g in range(SWA_KV_HEADS):
            kb = band(kp_ref, kc_ref, i, g)
            qa = q_ref[rows, (2 * g) * LANES:(2 * g + 1) * LANES]
            qb = q_ref[rows, (2 * g + 1) * LANES:(2 * g + 2) * LANES]
            for par in range(2):
                keep = lo if par == 0 else jnp.logical_not(lo)
                stack = jnp.concatenate([jnp.where(keep, qa, zero), jnp.where(keep, qb, zero)], axis=0)
                out.append(lax.dot_general(stack, kb, NT_DIMS, preferred_element_type=F32))
        return out

    def softmax(i, s_list):
        b_ref = bias0_ref if i == 0 else bias_ref
        out = []
        for (g, par), s in zip(chains, s_list):
            ha, hb = 4 * g + par, 4 * g + 2 + par
            blocks = [s[:, j * LANES:(j + 1) * LANES]
                      + jnp.concatenate([b_ref[ha, :, j * LANES:(j + 1) * LANES],
                                         b_ref[hb, :, j * LANES:(j + 1) * LANES]], axis=0) for j in range(2)]
            sink = jnp.concatenate([jnp.full((BLOCK, LANES), sink_ref[ha], F32),
                                    jnp.full((BLOCK, LANES), sink_ref[hb], F32)], axis=0)
            m = jnp.maximum(sink, jnp.max(jnp.maximum(blocks[0], blocks[1]), axis=-1, keepdims=True))
            p = jnp.concatenate([jnp.exp2(b - m) for b in blocks], axis=1).astype(BF16)
            out.append((p, jnp.exp2(sink - m)))
        return out

    def pv(i, sm):
        ext, extra = {}, {}
        for (g, par), (p, ex) in zip(chains, sm):
            vb = band(vp_ref, vc_ref, i, 2 * g + par)
            ext[g, par] = jnp.dot(p, vb, preferred_element_type=F32)
            extra[g, par] = ex
        for g in range(SWA_KV_HEADS):
            for j in range(2):
                rows = slice(j * BLOCK, (j + 1) * BLOCK)
                o_ref[i * BLOCK:(i + 1) * BLOCK, (2 * g + j) * LANES:(2 * g + j + 1) * LANES] = _normalize_pair(
                    ext[g, 0][rows], ext[g, 1][rows], extra[g, 0][rows], extra[g, 1][rows]).astype(BF16)

    s_cur = qk(0)
    for i in range(nqb):
        s_next = qk(i + 1) if i + 1 < nqb else None
        pv(i, softmax(i, s_cur))
        s_cur = s_next


def _swa_attn(sinks2, q, k, v, bias):
    B, S, _ = q.shape
    rows = SWA_QB * BLOCK
    cur = lambda w: pl.BlockSpec((None, rows, w), lambda b, n: (b, n, 0))
    prev = lambda w: pl.BlockSpec((None, BLOCK, w), lambda b, n: (b, jnp.maximum(n * SWA_QB - 1, 0), 0))
    bias_shape = (None, SWA_Q_HEADS, BLOCK, 2 * BLOCK)
    kw, vw = k.shape[-1], v.shape[-1]
    return pl.pallas_call(
        _swa_attn_kernel,
        grid=(B, S // rows),
        in_specs=[pl.BlockSpec(memory_space=pltpu.SMEM), cur(SWA_WIDTH), prev(kw), cur(kw), prev(vw), cur(vw),
                  pl.BlockSpec(bias_shape, lambda b, n: (jnp.minimum(n, 1), 0, 0, 0)),
                  pl.BlockSpec(bias_shape, lambda b, n: (1, 0, 0, 0))],
        out_specs=cur(SWA_WIDTH),
        out_shape=jax.ShapeDtypeStruct((B, S, SWA_WIDTH), BF16),
        compiler_params=_params(("parallel", "parallel")),
        name="swa_attn",
    )(sinks2, q, k, k, v, v, bias, bias)


def _rot(w):
    half = w.shape[-1] // 2
    return jnp.concatenate([-w[..., half:], w[..., :half]], axis=-1)


def _rope_tables(seq, c0):
    inv = ROPE_THETA ** (-np.arange(0, MLA_ROPE_DIM, 2, dtype=np.float64) / MLA_ROPE_DIM)
    ang = np.arange(seq, dtype=np.float64)[:, None] * inv[None, :]
    cos = np.concatenate([np.cos(ang)] * 2, axis=-1)
    sin = np.concatenate([np.sin(ang)] * 2, axis=-1)
    z32 = np.zeros((seq, LANES - HALF - MLA_ROPE_DIM))
    z64 = np.zeros((seq, HALF))
    c64 = np.full((seq, HALF), c0)
    aq_even = np.concatenate([c64, c0 * cos, z32], axis=-1)
    bq_even = np.concatenate([z64, c0 * sin, z32], axis=-1)
    aq_odd = np.concatenate([c0 * cos, z32, c64], axis=-1)
    bq_odd = np.concatenate([c0 * sin, z32, z64], axis=-1)
    ak = np.concatenate([z64, cos, z32], axis=-1)
    bk = np.concatenate([z64, sin, z32], axis=-1)
    return tuple(t.astype(np.float32) for t in (aq_even, bq_even, aq_odd, bq_odd, ak, bk))


def _mla_layer_operands(w_in, q_norm, w_q_up, kv_norm, w_kv_up, seq):
    c2 = MLA_Q_RANK + MLA_KV_RANK
    c3 = c2 + MLA_ROPE_DIM
    kr = w_in[:, c2:c3]
    krg = jnp.concatenate([jnp.zeros((D_MODEL, HALF), F32), kr, _rot(kr)], axis=-1)
    win = jnp.concatenate([w_in[:, :c2], krg, w_in[:, c3:]], axis=-1).astype(BF16)
    odd = (jnp.arange(MLA_HEADS) % 2 == 1)[None, :, None]
    wq = w_q_up.reshape(MLA_Q_RANK, MLA_HEADS, MLA_NOPE_DIM + MLA_ROPE_DIM)
    nope, rope = wq[..., :MLA_NOPE_DIM], wq[..., MLA_NOPE_DIM:]
    wq = jnp.where(odd, jnp.concatenate([rope, _rot(rope), nope], axis=-1),
                   jnp.concatenate([nope, rope, _rot(rope)], axis=-1))
    wq = wq.reshape(MLA_Q_RANK, PADW).astype(BF16)
    wkv = w_kv_up.reshape(MLA_KV_RANK, MLA_HEADS, MLA_NOPE_DIM + MLA_V_DIM)
    wk = wkv[..., :MLA_NOPE_DIM].reshape(MLA_KV_RANK, MLA_HEADS * MLA_NOPE_DIM).astype(BF16)
    wvt = wkv[..., MLA_NOPE_DIM:].reshape(MLA_KV_RANK, MLA_WIDTH).T.astype(BF16)
    c0 = (MLA_NOPE_DIM + MLA_ROPE_DIM) ** -0.5 * LOG2E
    return (win, q_norm.reshape(1, -1), kv_norm.reshape(1, -1), wq, wk, wvt) + _rope_tables(seq, c0)


def _fox_layer_operands(w_in, b_f):
    W = FOX_WIDTH
    wf = jnp.concatenate([w_in[:, 3 * W:3 * W + FOX_HEADS], jnp.zeros((D_MODEL, LANES - FOX_HEADS), F32)], axis=-1)
    win = jnp.concatenate([w_in[:, :2 * W], w_in[:, 3 * W + FOX_HEADS:], wf], axis=-1).astype(BF16)
    wvt = w_in[:, 2 * W:3 * W].T.astype(BF16)
    bf = jnp.concatenate([b_f, jnp.zeros((LANES - FOX_HEADS,), F32)]).reshape(1, LANES)
    h = np.arange(FOX_HEADS)
    base = (h // 2) * LANES + np.where(h % 2 == 0, HALF, 0)
    scat = np.zeros((LANES, 2 * W), np.float32)
    onesq = np.zeros((1, W), np.float32)
    onesk = np.zeros((1, W), np.float32)
    for j in range(3):
        scat[j * FOX_HEADS + h, base + j] = 1.0
        scat[j * FOX_HEADS + h, W + base + 3 + j] = -1.0
        onesq[0, base + 3 + j] = 1.0
        onesk[0, base + j] = 1.0
    return win, wvt, bf, jnp.asarray(scat, dtype=BF16), onesq, onesk


def _swa_bias():
    qi = np.arange(BLOCK)[:, None]
    kj = np.arange(2 * BLOCK)[None, :]
    dist = qi + BLOCK - kj
    valid = (dist >= 0) & (dist < WINDOW)
    slopes = 2.0 ** (-8.0 * np.arange(1, SWA_Q_HEADS + 1, dtype=np.float64) / SWA_Q_HEADS)
    ali = -(slopes[:, None, None] * dist.astype(np.float64)[None]) * LOG2E
    bias_rest = np.where(valid[None], ali, NEG_INF)
    bias_first = np.where((valid & (kj >= BLOCK))[None], ali, NEG_INF)
    return np.stack([bias_first, bias_rest]).astype(np.float32)


def kernel(x, ln_g, ln_b, mla_w_in, mla_q_norm, mla_w_q_up, mla_kv_norm, mla_w_kv_up, mla_w_out,
           swa_w_in, swa_sinks, swa_w_out, fox_w_in, fox_b_f, fox_w_out):
    seq = x.shape[1]
    out_ops = None
    for i in range(DEPTH + 1):
        j = i // N_MIXERS
        kind = ("mla", "swa", "fox")[i % N_MIXERS] if i < DEPTH else None
        if kind == "mla":
            proj_ops = _mla_layer_operands(mla_w_in[j], mla_q_norm[j], mla_w_q_up[j], mla_kv_norm[j],
                                           mla_w_kv_up[j], seq)
        elif kind == "swa":
            proj_ops = (swa_w_in[j].astype(BF16),)
        elif kind == "fox":
            proj_ops = _fox_layer_operands(fox_w_in[j], fox_b_f[j])
        else:
            proj_ops = ()
        res = _boundary(x, out_ops, kind, proj_ops)
        if out_ops is not None:
            x, res = res[0], res[1:]
        if kind is None:
            return x
        q, k, v, sg = res
        if kind == "mla":
            o = _causal_attn(q, k, v)
            w_out = mla_w_out[j]
        elif kind == "swa":
            o = _swa_attn(swa_sinks[j].astype(F32) * LOG2E, q, k, v, _swa_bias())
            w_out = swa_w_out[j]
        else:
            o = _causal_attn(q, k, v)
            w_out = fox_w_out[j]
        out_ops = (o, sg, w_out.astype(BF16), ln_g[i].reshape(1, -1), ln_b[i].reshape(1, -1))
```

```python
import functools
import math

import jax
import jax.numpy as jnp
import numpy as np
from jax import lax
from jax.experimental import pallas as pl
from jax.experimental.pallas import tpu as pltpu

D_MODEL = 1024
DEPTH = 4
N_MIXERS = 3
BLOCK = 128
NEG_INF = -1e30
RMS_EPS = 1e-6
LN_EPS = 1e-5
DEEPNORM_ALPHA = (2 * DEPTH) ** 0.25
LOG2E = math.log2(math.e)

MLA_HEADS = 16
MLA_NOPE_DIM = 64
MLA_ROPE_DIM = 32
MLA_V_DIM = 64
MLA_Q_RANK = 384
MLA_KV_RANK = 256
ROPE_THETA = 10000.0
MLA_WIDTH = MLA_HEADS * MLA_V_DIM

SWA_Q_HEADS = 16
SWA_KV_HEADS = 4
SWA_HEAD_DIM = 64
WINDOW = 128
SWA_WIDTH = SWA_Q_HEADS * SWA_HEAD_DIM
SWA_KV_WIDTH = SWA_KV_HEADS * SWA_HEAD_DIM

FOX_HEADS = 16
FOX_HEAD_DIM = 64
FOX_WIDTH = FOX_HEADS * FOX_HEAD_DIM

LANES = 128
HALF = LANES // 2
HEADS = 16
PADW = HEADS * LANES

ROW_TILE = {"mla": 512, "swa": 512, "fox": 512, None: 512}
SUB_ROWS = 256
TQ = 512
TK = 512
ATT_HEADS = 8
SWA_QB = 4
VMEM_LIMIT = 56 * 1024 * 1024
ATT_FLAGS = None

F32 = jnp.float32
BF16 = jnp.bfloat16
NT_DIMS = (((1,), (1,)), ((), ()))


def _params(sem, flags=None):
    return pltpu.CompilerParams(dimension_semantics=sem, vmem_limit_bytes=VMEM_LIMIT, flags=flags)


def _lane_lt_half(shape):
    return lax.broadcasted_iota(jnp.int32, shape, len(shape) - 1) < HALF


def _silu(g):
    return g / (1.0 + jnp.exp(-g))


def _write_vt(vt, vt_out):
    top = lax.broadcasted_iota(jnp.int32, (LANES, vt.shape[1]), 0) < HALF
    for p in range(HEADS // 2):
        blk = vt[p * LANES:(p + 1) * LANES, :]
        vt_out[(2 * p) * LANES:(2 * p + 1) * LANES, :] = jnp.where(top, blk, 1.0).astype(BF16)
        vt_out[(2 * p + 1) * LANES:(2 * p + 2) * LANES, :] = jnp.where(top, 1.0, blk).astype(BF16)


def _mla_body(x, win_ref, qg_ref, kg_ref, wq_ref, wk_ref, wvt_ref, aqe_ref, bqe_ref, aqo_ref, bqo_ref,
              ak_ref, bk_ref, q_out, k_out, vt_out, g_out):
    h = jnp.dot(x.astype(BF16), win_ref[...], preferred_element_type=F32)
    c1 = MLA_Q_RANK
    c2 = c1 + MLA_KV_RANK
    c3 = c2 + LANES
    cq, ckv, krg, gate = h[:, :c1], h[:, c1:c2], h[:, c2:c3], h[:, c3:]
    g_out[...] = _silu(gate).astype(BF16)

    qn = cq * lax.rsqrt(jnp.mean(cq * cq, axis=-1, keepdims=True) + RMS_EPS) * qg_ref[...]
    kn = (ckv * lax.rsqrt(jnp.mean(ckv * ckv, axis=-1, keepdims=True) + RMS_EPS) * kg_ref[...]).astype(BF16)
    q = jnp.dot(qn.astype(BF16), wq_ref[...], preferred_element_type=F32)
    kc = jnp.dot(kn, wk_ref[...], preferred_element_type=F32)
    _write_vt(lax.dot_general(wvt_ref[...], kn, NT_DIMS, preferred_element_type=F32), vt_out)

    shift = LANES - MLA_ROPE_DIM
    aq = (aqe_ref[...], aqo_ref[...])
    bq = (bqe_ref[...], bqo_ref[...])
    kr_hi = krg * ak_ref[...] + pltpu.roll(krg, shift, 1) * bk_ref[...]
    kr_lo = pltpu.roll(kr_hi, HALF, 1)
    lo = _lane_lt_half((x.shape[0], LANES))
    for p in range(HEADS // 2):
        e = slice((2 * p) * LANES, (2 * p + 1) * LANES)
        o = slice((2 * p + 1) * LANES, (2 * p + 2) * LANES)
        kblk = kc[:, p * LANES:(p + 1) * LANES]
        q_out[:, e] = (q[:, e] * aq[0] + pltpu.roll(q[:, e], shift, 1) * bq[0]).astype(BF16)
        q_out[:, o] = (q[:, o] * aq[1] + pltpu.roll(q[:, o], shift, 1) * bq[1]).astype(BF16)
        k_out[:, e] = jnp.where(lo, kblk, kr_hi).astype(BF16)
        k_out[:, o] = jnp.where(lo, kr_lo, kblk).astype(BF16)


def _split_pairs(nat, aug, out_ref):
    lo = _lane_lt_half((nat.shape[0], LANES))
    for p in range(HEADS // 2):
        blk = nat[:, p * LANES:(p + 1) * LANES]
        fill = aug[:, p * LANES:(p + 1) * LANES]
        out_ref[:, (2 * p) * LANES:(2 * p + 1) * LANES] = jnp.where(lo, blk, fill).astype(BF16)
        out_ref[:, (2 * p + 1) * LANES:(2 * p + 2) * LANES] = jnp.where(lo, fill, blk).astype(BF16)


def _fox_body(x, win_ref, wvt_ref, bf_ref, scat_ref, onesq_ref, onesk_ref, q_out, k_out, vt_out, g_out, carry_ref,
              first_rows):
    tm = x.shape[0]

    if first_rows:
        @pl.when(pl.program_id(1) == 0)
        def _():
            carry_ref[...] = jnp.zeros_like(carry_ref)

    xb = x.astype(BF16)
    h = jnp.dot(xb, win_ref[...], preferred_element_type=F32)
    W = FOX_WIDTH
    g_out[...] = _silu(h[:, 2 * W:3 * W]).astype(BF16)
    _write_vt(lax.dot_general(wvt_ref[...], xb, NT_DIMS, preferred_element_type=F32), vt_out)

    f = h[:, 3 * W:3 * W + LANES] + bf_ref[...]
    logf = jnp.minimum(f, 0.0) - jnp.log(1.0 + jnp.exp(-jnp.abs(f)))

    def split3(a):
        hi = a.astype(BF16)
        r = a - hi.astype(F32)
        mid = r.astype(BF16)
        lo = (r - mid.astype(F32)).astype(BF16)
        return hi, mid, lo

    ri = lax.broadcasted_iota(jnp.int32, (tm, tm), 0)
    ci = lax.broadcasted_iota(jnp.int32, (tm, tm), 1)
    tri = (ci <= ri).astype(BF16)
    l_hi, l_mid, l_lo = split3(logf)
    cum = (jnp.dot(tri, l_hi, preferred_element_type=F32) + jnp.dot(tri, l_mid, preferred_element_type=F32)
           + jnp.dot(tri, l_lo, preferred_element_type=F32)) + carry_ref[...]
    carry_ref[...] = cum[tm - 1:tm, :]

    c_hi, c_mid, c_lo = split3(cum * LOG2E)
    lane = lax.broadcasted_iota(jnp.int32, (tm, LANES), 1)
    packed = jnp.where(lane < HEADS, c_hi.astype(F32),
                       jnp.where(lane < 2 * HEADS, pltpu.roll(c_mid.astype(F32), HEADS, 1),
                                 pltpu.roll(c_lo.astype(F32), 2 * HEADS, 1)))
    packed = jnp.where(lane < 3 * HEADS, packed, 0.0).astype(BF16)
    aug = jnp.dot(packed, scat_ref[...], preferred_element_type=F32)
    augq = aug[:, :W] + onesq_ref[...]
    augk = aug[:, W:] + onesk_ref[...]

    _split_pairs(h[:, :W] * (FOX_HEAD_DIM ** -0.5 * LOG2E), augq, q_out)
    _split_pairs(h[:, W:2 * W], augk, k_out)


def _swa_body(x, win_ref, q_out, k_out, v_out, g_out):
    h = jnp.dot(x.astype(BF16), win_ref[...], preferred_element_type=F32)
    c1 = SWA_WIDTH
    c2 = c1 + SWA_KV_WIDTH
    c3 = c2 + SWA_KV_WIDTH
    q_out[...] = (h[:, :c1] * (SWA_HEAD_DIM ** -0.5 * LOG2E)).astype(BF16)
    g_out[...] = _silu(h[:, c3:]).astype(BF16)
    lo = _lane_lt_half((x.shape[0], LANES))
    for pair in range(SWA_KV_HEADS // 2):
        kp = h[:, c1 + pair * LANES:c1 + (pair + 1) * LANES]
        vp = h[:, c2 + pair * LANES:c2 + (pair + 1) * LANES]
        ks, vs = pltpu.roll(kp, HALF, 1), pltpu.roll(vp, HALF, 1)
        for t, (k_h, v_h) in enumerate(((jnp.where(lo, kp, ks), jnp.where(lo, vp, vs)),
                                        (jnp.where(lo, ks, kp), jnp.where(lo, vs, vp)))):
            g = 2 * pair + t
            k_out[:, g * LANES:(g + 1) * LANES] = k_h.astype(BF16)
            v_out[:, (2 * g) * LANES:(2 * g + 1) * LANES] = jnp.where(lo, v_h, 1.0).astype(BF16)
            v_out[:, (2 * g + 1) * LANES:(2 * g + 2) * LANES] = jnp.where(lo, 1.0, v_h).astype(BF16)


VT = "vt"
_PROJ = {
    "mla": (_mla_body, 12, (PADW, PADW, VT, MLA_WIDTH), 6),
    "fox": (_fox_body, 6, (PADW, PADW, VT, FOX_WIDTH), 0),
    "swa": (_swa_body, 1, (SWA_WIDTH, SWA_KV_HEADS * LANES, SWA_KV_HEADS * 2 * LANES, SWA_WIDTH), 0),
}


def _out_body(o_ref, g_ref, x_ref, w_ref, lg_ref, lb_ref):
    a = (o_ref[...].astype(F32) * g_ref[...].astype(F32)).astype(BF16)
    y = jnp.dot(a, w_ref[...], preferred_element_type=F32)
    z = DEEPNORM_ALPHA * x_ref[...] + y
    mu = jnp.mean(z, axis=-1, keepdims=True)
    zc = z - mu
    var = jnp.mean(zc * zc, axis=-1, keepdims=True)
    return zc * lax.rsqrt(var + LN_EPS) * lg_ref[...] + lb_ref[...]


def _boundary_kernel(*refs, with_out, kind):
    refs = list(refs)
    if with_out:
        out_in, refs = refs[:6], refs[6:]
    else:
        x_ref, refs = refs[0], refs[1:]
    n_ops, widths, n_tab = _PROJ[kind][1:] if kind else (0, (), 0)
    proj_in, refs = refs[:n_ops], refs[n_ops:]
    if with_out:
        y_ref, refs = refs[0], refs[1:]
    outs, scratch = refs[:len(widths)], refs[len(widths):]
    tm = (out_in[2] if with_out else x_ref).shape[0]
    blocks = [pl.ds(r * SUB_ROWS, SUB_ROWS) for r in range(tm // SUB_ROWS)]
    xs = []
    for rows in blocks:
        if with_out:
            xs.append(_out_body(*(ref.at[rows] for ref in out_in[:3]), *out_in[3:]))
            y_ref[rows, :] = xs[-1]
        else:
            xs.append(x_ref[rows, :])
    for r, (rows, x) in enumerate(zip(blocks, xs)):
        if kind:
            extra = (r == 0,) if kind == "fox" else ()
            _PROJ[kind][0](x, *proj_in[:n_ops - n_tab], *(t.at[rows] for t in proj_in[n_ops - n_tab:]),
                           *(o.at[:, rows] if w == VT else o.at[rows] for o, w in zip(outs, widths)),
                           *scratch, *extra)


def _boundary(x, out_ops, kind, proj_ops):
    B, S, _ = x.shape
    tm = ROW_TILE[kind]
    row = lambda w: pl.BlockSpec((None, tm, w), lambda b, i: (b, i, 0))
    full = lambda a: pl.BlockSpec(a.shape, lambda b, i: (0,) * a.ndim, pipeline_mode=pl.Buffered(1))
    tab = pl.BlockSpec((tm, LANES), lambda b, i: (i, 0))
    in_specs, args, out_specs, out_shape, scratch = [], [], [], [], []
    if out_ops is not None:
        o, sg, w, lg, lb = out_ops
        in_specs += [row(D_MODEL), row(D_MODEL), row(D_MODEL), full(w), full(lg), full(lb)]
        args += [o, sg, x, w, lg, lb]
        out_specs.append(row(D_MODEL))
        out_shape.append(jax.ShapeDtypeStruct((B, S, D_MODEL), F32))
    else:
        in_specs.append(row(D_MODEL))
        args.append(x)
    if kind:
        _, n_ops, widths, n_tab = _PROJ[kind]
        assert len(proj_ops) == n_ops
        in_specs += [full(a) for a in proj_ops[:n_ops - n_tab]] + [tab] * n_tab
        args += list(proj_ops)
        for w in widths:
            if w == VT:
                assert TK % tm == 0
                out_specs.append(pl.BlockSpec((None, None, PADW, tm),
                                              lambda b, i: (b, i * tm // TK, 0, i % (TK // tm))))
                out_shape.append(jax.ShapeDtypeStruct((B, S // TK, PADW, TK), BF16))
            else:
                out_specs.append(row(w))
                out_shape.append(jax.ShapeDtypeStruct((B, S, w), BF16))
        if kind == "fox":
            scratch.append(pltpu.VMEM((1, LANES), F32))
    sem = ("parallel", "arbitrary" if kind == "fox" else "parallel")
    return pl.pallas_call(
        functools.partial(_boundary_kernel, with_out=out_ops is not None, kind=kind),
        grid=(B, S // tm),
        in_specs=in_specs,
        out_specs=out_specs,
        out_shape=out_shape,
        scratch_shapes=scratch,
        compiler_params=_params(sem),
        name="boundary_" + ("out_" if out_ops is not None else "") + (kind or "final"),
    )(*args)


def _normalize_pair(ext_even, ext_odd, extra_even=None, extra_odd=None):
    lo = _lane_lt_half(ext_even.shape)
    den = jnp.where(lo, ext_odd, ext_even)
    if extra_even is not None:
        den = den + jnp.where(lo, extra_odd, extra_even)
    return jnp.where(lo, ext_even, ext_odd) * pltpu.roll(1.0 / den, HALF, 1)


def _causal_attn_kernel(q_ref, k_ref, vt_ref, o_ref, m_ref, acc_ref):
    S = q_ref.shape[0]
    heads = range(ATT_HEADS)
    lanes = [slice(h * LANES, (h + 1) * LANES) for h in heads]

    def tile(q0, kt, jobs, diagonal):
        chains = [(h,) + job for job in jobs for h in heads]
        k0 = pl.multiple_of(kt * TK, TK)
        st_all = [lax.dot_general(k_ref[pl.ds(k0, nk), lanes[h]], q_ref[pl.ds(q0 + coff, nq), lanes[h]],
                                  NT_DIMS, preferred_element_type=F32) for h, coff, nq, nk in chains]
        pt_all, alpha_all = [], []
        for (h, coff, nq, nk), st in zip(chains, st_all):
            cols = slice(coff, coff + nq)
            blocks = [st[:, j * LANES:(j + 1) * LANES] for j in range(nq // LANES)]
            if diagonal:
                key = lax.broadcasted_iota(jnp.int32, (nk, LANES), 0)
                qry = lax.broadcasted_iota(jnp.int32, (nk, LANES), 1) + coff
                blocks = [b if nk - 1 <= coff + j * LANES else jnp.where(key <= qry + j * LANES, b, NEG_INF)
                          for j, b in enumerate(blocks)]
            m_blk = jnp.concatenate([jnp.max(b, axis=0, keepdims=True) for b in blocks], axis=1)
            if diagonal:
                m_new = m_blk
                alpha_all.append(None)
            else:
                m_old = m_ref[h, :, cols]
                m_new = jnp.maximum(m_old, m_blk)
                alpha_all.append(jnp.exp2(m_old - m_new))
            pt_all.append(jnp.concatenate(
                [jnp.exp2(b - m_new[:, j * LANES:(j + 1) * LANES]) for j, b in enumerate(blocks)],
                axis=1).astype(BF16))
            m_ref[h, :, cols] = m_new
        for (h, coff, nq, nk), pt, alpha in zip(chains, pt_all, alpha_all):
            cols = slice(coff, coff + nq)
            pv = jnp.dot(vt_ref[kt, lanes[h], 0:nk], pt, preferred_element_type=F32)
            acc_ref[h, :, cols] = pv if alpha is None else acc_ref[h, :, cols] * alpha + pv

    def q_body(qi, _):
        q0 = pl.multiple_of(qi * TQ, TQ)
        half = TQ // 2
        tile(q0, qi, [(0, half, half), (half, half, TQ)], True)

        def kv_body(kj, _):
            tile(q0, kj, [(0, TQ, TK)], False)
            return 0

        lax.fori_loop(0, qi, kv_body, 0)
        sub = 8
        for p in range(ATT_HEADS // 2):
            acc_e, acc_o = acc_ref[2 * p], acc_ref[2 * p + 1]
            r_e = jnp.concatenate([1.0 / acc_e[HALF:HALF + sub]] * (HALF // sub), axis=0)
            r_o = jnp.concatenate([1.0 / acc_o[0:sub]] * (HALF // sub), axis=0)
            pair_t = jnp.concatenate([acc_e[:HALF] * r_e, acc_o[HALF:] * r_o], axis=0)
            o_ref[pl.ds(q0, TQ), p * LANES:(p + 1) * LANES] = pair_t.T.astype(BF16)
        return 0

    lax.fori_loop(0, S // TQ, q_body, 0)


def _causal_attn(q, k, vt):
    B, S, _ = q.shape
    spec = pl.BlockSpec((None, S, ATT_HEADS * LANES), lambda b, p: (b, 0, p))
    return pl.pallas_call(
        _causal_attn_kernel,
        grid=(B, HEADS // ATT_HEADS),
        in_specs=[spec, spec, pl.BlockSpec((None, S // TK, ATT_HEADS * LANES, TK), lambda b, p: (b, 0, p, 0))],
        out_specs=pl.BlockSpec((None, S, ATT_HEADS * HALF), lambda b, p: (b, 0, p)),
        out_shape=jax.ShapeDtypeStruct((B, S, HEADS * HALF), BF16),
        scratch_shapes=[pltpu.VMEM((ATT_HEADS, 1, TQ), F32), pltpu.VMEM((ATT_HEADS, LANES, TQ), F32)],
        compiler_params=_params(("parallel", "parallel"), ATT_FLAGS),
        name="causal_attn",
    )(q, k, vt)


def _swa_attn_kernel(sink_ref, q_ref, kp_ref, kc_ref, vp_ref, vc_ref, bias0_ref, bias_ref, o_ref):
    nqb = q_ref.shape[0] // BLOCK
    lo = _lane_lt_half((BLOCK, LANES))
    zero = jnp.zeros((), BF16)
    chains = [(g, par) for g in range(SWA_KV_HEADS) for par in range(2)]

    def band(prev_ref, cur_ref, i, group):
        cols = slice(group * LANES, (group + 1) * LANES)
        if i == 0:
            return jnp.concatenate([prev_ref[:, cols], cur_ref[0:BLOCK, cols]], axis=0)
        return cur_ref[(i - 1) * BLOCK:(i + 1) * BLOCK, cols]

    def qk(i):
        rows = slice(i * BLOCK, (i + 1) * BLOCK)
        out = []
        for g in range(SWA_KV_HEADS):
            kb = band(kp_ref, kc_ref, i, g)
            qa = q_ref[rows, (2 * g) * LANES:(2 * g + 1) * LANES]
            qb = q_ref[rows, (2 * g + 1) * LANES:(2 * g + 2) * LANES]
            for par in range(2):
                keep = lo if par == 0 else jnp.logical_not(lo)
                stack = jnp.concatenate([jnp.where(keep, qa, zero), jnp.where(keep, qb, zero)], axis=0)
                out.append(lax.dot_general(stack, kb, NT_DIMS, preferred_element_type=F32))
        return out

    def softmax(i, s_list):
        b_ref = bias0_ref if i == 0 else bias_ref
        out = []
        for (g, par), s in zip(chains, s_list):
            ha, hb = 4 * g + par, 4 * g + 2 + par
            blocks = [s[:, j * LANES:(j + 1) * LANES]
                      + jnp.concatenate([b_ref[ha, :, j * LANES:(j + 1) * LANES],
                                         b_ref[hb, :, j * LANES:(j + 1) * LANES]], axis=0) for j in range(2)]
            sink = jnp.concatenate([jnp.full((BLOCK, LANES), sink_ref[ha], F32),
                                    jnp.full((BLOCK, LANES), sink_ref[hb], F32)], axis=0)
            m = jnp.maximum(sink, jnp.max(jnp.maximum(blocks[0], blocks[1]), axis=-1, keepdims=True))
            p = jnp.concatenate([jnp.exp2(b - m) for b in blocks], axis=1).astype(BF16)
            out.append((p, jnp.exp2(sink - m)))
        return out

    def pv(i, sm):
        ext, extra = {}, {}
        for (g, par), (p, ex) in zip(chains, sm):
            vb = band(vp_ref, vc_ref, i, 2 * g + par)
            ext[g, par] = jnp.dot(p, vb, preferred_element_type=F32)
            extra[g, par] = ex
        for g in range(SWA_KV_HEADS):
            for j in range(2):
                rows = slice(j * BLOCK, (j + 1) * BLOCK)
                o_ref[i * BLOCK:(i + 1) * BLOCK, (2 * g + j) * LANES:(2 * g + j + 1) * LANES] = _normalize_pair(
                    ext[g, 0][rows], ext[g, 1][rows], extra[g, 0][rows], extra[g, 1][rows]).astype(BF16)

    s_cur = qk(0)
    for i in range(nqb):
        s_next = qk(i + 1) if i + 1 < nqb else None
        pv(i, softmax(i, s_cur))
        s_cur = s_next


def _swa_attn(sinks2, q, k, v, bias):
    B, S, _ = q.shape
    rows = SWA_QB * BLOCK
    cur = lambda w: pl.BlockSpec((None, rows, w), lambda b, n: (b, n, 0))
    prev = lambda w: pl.BlockSpec((None, BLOCK, w), lambda b, n: (b, jnp.maximum(n * SWA_QB - 1, 0), 0))
    bias_shape = (None, SWA_Q_HEADS, BLOCK, 2 * BLOCK)
    kw, vw = k.shape[-1], v.shape[-1]
    return pl.pallas_call(
        _swa_attn_kernel,
        grid=(B, S // rows),
        in_specs=[pl.BlockSpec(memory_space=pltpu.SMEM), cur(SWA_WIDTH), prev(kw), cur(kw), prev(vw), cur(vw),
                  pl.BlockSpec(bias_shape, lambda b, n: (jnp.minimum(n, 1), 0, 0, 0)),
                  pl.BlockSpec(bias_shape, lambda b, n: (1, 0, 0, 0))],
        out_specs=cur(SWA_WIDTH),
        out_shape=jax.ShapeDtypeStruct((B, S, SWA_WIDTH), BF16),
        compiler_params=_params(("parallel", "parallel")),
        name="swa_attn",
    )(sinks2, q, k, k, v, v, bias, bias)


def _rot(w):
    half = w.shape[-1] // 2
    return jnp.concatenate([-w[..., half:], w[..., :half]], axis=-1)


def _rope_tables(seq, c0):
    inv = ROPE_THETA ** (-np.arange(0, MLA_ROPE_DIM, 2, dtype=np.float64) / MLA_ROPE_DIM)
    ang = np.arange(seq, dtype=np.float64)[:, None] * inv[None, :]
    cos = np.concatenate([np.cos(ang)] * 2, axis=-1)
    sin = np.concatenate([np.sin(ang)] * 2, axis=-1)
    z32 = np.zeros((seq, LANES - HALF - MLA_ROPE_DIM))
    z64 = np.zeros((seq, HALF))
    c64 = np.full((seq, HALF), c0)
    aq_even = np.concatenate([c64, c0 * cos, z32], axis=-1)
    bq_even = np.concatenate([z64, c0 * sin, z32], axis=-1)
    aq_odd = np.concatenate([c0 * cos, z32, c64], axis=-1)
    bq_odd = np.concatenate([c0 * sin, z32, z64], axis=-1)
    ak = np.concatenate([z64, cos, z32], axis=-1)
    bk = np.concatenate([z64, sin, z32], axis=-1)
    return tuple(t.astype(np.float32) for t in (aq_even, bq_even, aq_odd, bq_odd, ak, bk))


def _mla_layer_operands(w_in, q_norm, w_q_up, kv_norm, w_kv_up, seq):
    c2 = MLA_Q_RANK + MLA_KV_RANK
    c3 = c2 + MLA_ROPE_DIM
    kr = w_in[:, c2:c3]
    krg = jnp.concatenate([jnp.zeros((D_MODEL, HALF), F32), kr, _rot(kr)], axis=-1)
    win = jnp.concatenate([w_in[:, :c2], krg, w_in[:, c3:]], axis=-1).astype(BF16)
    odd = (jnp.arange(MLA_HEADS) % 2 == 1)[None, :, None]
    wq = w_q_up.reshape(MLA_Q_RANK, MLA_HEADS, MLA_NOPE_DIM + MLA_ROPE_DIM)
    nope, rope = wq[..., :MLA_NOPE_DIM], wq[..., MLA_NOPE_DIM:]
    wq = jnp.where(odd, jnp.concatenate([rope, _rot(rope), nope], axis=-1),
                   jnp.concatenate([nope, rope, _rot(rope)], axis=-1))
    wq = wq.reshape(MLA_Q_RANK, PADW).astype(BF16)
    wkv = w_kv_up.reshape(MLA_KV_RANK, MLA_HEADS, MLA_NOPE_DIM + MLA_V_DIM)
    wk = wkv[..., :MLA_NOPE_DIM].reshape(MLA_KV_RANK, MLA_HEADS * MLA_NOPE_DIM).astype(BF16)
    wvt = wkv[..., MLA_NOPE_DIM:].reshape(MLA_KV_RANK, MLA_WIDTH).T.astype(BF16)
    c0 = (MLA_NOPE_DIM + MLA_ROPE_DIM) ** -0.5 * LOG2E
    return (win, q_norm.reshape(1, -1), kv_norm.reshape(1, -1), wq, wk, wvt) + _rope_tables(seq, c0)


def _fox_layer_operands(w_in, b_f):
    W = FOX_WIDTH
    wf = jnp.concatenate([w_in[:, 3 * W:3 * W + FOX_HEADS], jnp.zeros((D_MODEL, LANES - FOX_HEADS), F32)], axis=-1)
    win = jnp.concatenate([w_in[:, :2 * W], w_in[:, 3 * W + FOX_HEADS:], wf], axis=-1).astype(BF16)
    wvt = w_in[:, 2 * W:3 * W].T.astype(BF16)
    bf = jnp.concatenate([b_f, jnp.zeros((LANES - FOX_HEADS,), F32)]).reshape(1, LANES)
    h = np.arange(FOX_HEADS)
    base = (h // 2) * LANES + np.where(h % 2 == 0, HALF, 0)
    scat = np.zeros((LANES, 2 * W), np.float32)
    onesq = np.zeros((1, W), np.float32)
    onesk = np.zeros((1, W), np.float32)
    for j in range(3):
        scat[j * FOX_HEADS + h, base + j] = 1.0
        scat[j * FOX_HEADS + h, W + base + 3 + j] = -1.0
        onesq[0, base + 3 + j] = 1.0
        onesk[0, base + j] = 1.0
    return win, wvt, bf, jnp.asarray(scat, dtype=BF16), onesq, onesk


def _swa_bias():
    qi = np.arange(BLOCK)[:, None]
    kj = np.arange(2 * BLOCK)[None, :]
    dist = qi + BLOCK - kj
    valid = (dist >= 0) & (dist < WINDOW)
    slopes = 2.0 ** (-8.0 * np.arange(1, SWA_Q_HEADS + 1, dtype=np.float64) / SWA_Q_HEADS)
    ali = -(slopes[:, None, None] * dist.astype(np.float64)[None]) * LOG2E
    bias_rest = np.where(valid[None], ali, NEG_INF)
    bias_first = np.where((valid & (kj >= BLOCK))[None], ali, NEG_INF)
    return np.stack([bias_first, bias_rest]).astype(np.float32)


def kernel(x, ln_g, ln_b, mla_w_in, mla_q_norm, mla_w_q_up, mla_kv_norm, mla_w_kv_up, mla_w_out,
           swa_w_in, swa_sinks, swa_w_out, fox_w_in, fox_b_f, fox_w_out):
    seq = x.shape[1]
    out_ops = None
    for i in range(DEPTH + 1):
        j = i // N_MIXERS
        kind = ("mla", "swa", "fox")[i % N_MIXERS] if i < DEPTH else None
        if kind == "mla":
            proj_ops = _mla_layer_operands(mla_w_in[j], mla_q_norm[j], mla_w_q_up[j], mla_kv_norm[j],
                                           mla_w_kv_up[j], seq)
        elif kind == "swa":
            proj_ops = (swa_w_in[j].astype(BF16),)
        elif kind == "fox":
            proj_ops = _fox_layer_operands(fox_w_in[j], fox_b_f[j])
        else:
            proj_ops = ()
        res = _boundary(x, out_ops, kind, proj_ops)
        if out_ops is not None:
            x, res = res[0], res[1:]
        if kind is None:
            return x
        q, k, v, sg = res
        if kind == "mla":
            o = _causal_attn(q, k, v)
            w_out = mla_w_out[j]
        elif kind == "swa":
            o = _swa_attn(swa_sinks[j].astype(F32) * LOG2E, q, k, v, _swa_bias())
            w_out = swa_w_out[j]
        else:
            o = _causal_attn(q, k, v)
            w_out = fox_w_out[j]
        out_ops = (o, sg, w_out.astype(BF16), ln_g[i].reshape(1, -1), ln_b[i].reshape(1, -1))
```

```python
import functools
import math

import jax
import jax.numpy as jnp
import numpy as np
from jax import lax
from jax.experimental import pallas as pl
from jax.experimental.pallas import tpu as pltpu

D_MODEL = 1024
DEPTH = 4
N_MIXERS = 3
BLOCK = 128
NEG_INF = -1e30
RMS_EPS = 1e-6
LN_EPS = 1e-5
DEEPNORM_ALPHA = (2 * DEPTH) ** 0.25
LOG2E = math.log2(math.e)

MLA_HEADS = 16
MLA_NOPE_DIM = 64
MLA_ROPE_DIM = 32
MLA_V_DIM = 64
MLA_Q_RANK = 384
MLA_KV_RANK = 256
ROPE_THETA = 10000.0
MLA_WIDTH = MLA_HEADS * MLA_V_DIM

SWA_Q_HEADS = 16
SWA_KV_HEADS = 4
SWA_HEAD_DIM = 64
WINDOW = 128
SWA_WIDTH = SWA_Q_HEADS * SWA_HEAD_DIM
SWA_KV_WIDTH = SWA_KV_HEADS * SWA_HEAD_DIM

FOX_HEADS = 16
FOX_HEAD_DIM = 64
FOX_WIDTH = FOX_HEADS * FOX_HEAD_DIM

LANES = 128
HALF = LANES // 2
HEADS = 16
PADW = HEADS * LANES

ROW_TILE = {"mla": 512, "swa": 512, "fox": 512, None: 512}
SUB_ROWS = 256
TQ = 512
TK = 512
ATT_HEADS = 8
SWA_QB = 4
VMEM_LIMIT = 56 * 1024 * 1024
ATT_FLAGS = None

F32 = jnp.float32
BF16 = jnp.bfloat16
NT_DIMS = (((1,), (1,)), ((), ()))


def _params(sem, flags=None):
    return pltpu.CompilerParams(dimension_semantics=sem, vmem_limit_bytes=VMEM_LIMIT, flags=flags)


def _lane_lt_half(shape):
    return lax.broadcasted_iota(jnp.int32, shape, len(shape) - 1) < HALF


def _silu(g):
    return g / (1.0 + jnp.exp(-g))


def _write_vt(vt, vt_out):
    top = lax.broadcasted_iota(jnp.int32, (LANES, vt.shape[1]), 0) < HALF
    for p in range(HEADS // 2):
        blk = vt[p * LANES:(p + 1) * LANES, :]
        vt_out[(2 * p) * LANES:(2 * p + 1) * LANES, :] = jnp.where(top, blk, 1.0).astype(BF16)
        vt_out[(2 * p + 1) * LANES:(2 * p + 2) * LANES, :] = jnp.where(top, 1.0, blk).astype(BF16)


def _mla_body(x, win_ref, qg_ref, kg_ref, wq_ref, wk_ref, wvt_ref, aqe_ref, bqe_ref, aqo_ref, bqo_ref,
              ak_ref, bk_ref, q_out, k_out, vt_out, g_out):
    h = jnp.dot(x.astype(BF16), win_ref[...], preferred_element_type=F32)
    c1 = MLA_Q_RANK
    c2 = c1 + MLA_KV_RANK
    c3 = c2 + LANES
    cq, ckv, krg, gate = h[:, :c1], h[:, c1:c2], h[:, c2:c3], h[:, c3:]
    g_out[...] = _silu(gate).astype(BF16)

    qn = cq * lax.rsqrt(jnp.mean(cq * cq, axis=-1, keepdims=True) + RMS_EPS) * qg_ref[...]
    kn = (ckv * lax.rsqrt(jnp.mean(ckv * ckv, axis=-1, keepdims=True) + RMS_EPS) * kg_ref[...]).astype(BF16)
    q = jnp.dot(qn.astype(BF16), wq_ref[...], preferred_element_type=F32)
    kc = jnp.dot(kn, wk_ref[...], preferred_element_type=F32)
    _write_vt(lax.dot_general(wvt_ref[...], kn, NT_DIMS, preferred_element_type=F32), vt_out)

    shift = LANES - MLA_ROPE_DIM
    aq = (aqe_ref[...], aqo_ref[...])
    bq = (bqe_ref[...], bqo_ref[...])
    kr_hi = krg * ak_ref[...] + pltpu.roll(krg, shift, 1) * bk_ref[...]
    kr_lo = pltpu.roll(kr_hi, HALF, 1)
    lo = _lane_lt_half((x.shape[0], LANES))
    for p in range(HEADS // 2):
        e = slice((2 * p) * LANES, (2 * p + 1) * LANES)
        o = slice((2 * p + 1) * LANES, (2 * p + 2) * LANES)
        kblk = kc[:, p * LANES:(p + 1) * LANES]
        q_out[:, e] = (q[:, e] * aq[0] + pltpu.roll(q[:, e], shift, 1) * bq[0]).astype(BF16)
        q_out[:, o] = (q[:, o] * aq[1] + pltpu.roll(q[:, o], shift, 1) * bq[1]).astype(BF16)
        k_out[:, e] = jnp.where(lo, kblk, kr_hi).astype(BF16)
        k_out[:, o] = jnp.where(lo, kr_lo, kblk).astype(BF16)


def _split_pairs(nat, aug, out_ref):
    lo = _lane_lt_half((nat.shape[0], LANES))
    for p in range(HEADS // 2):
        blk = nat[:, p * LANES:(p + 1) * LANES]
        fill = aug[:, p * LANES:(p + 1) * LANES]
        out_ref[:, (2 * p) * LANES:(2 * p + 1) * LANES] = jnp.where(lo, blk, fill).astype(BF16)
        out_ref[:, (2 * p + 1) * LANES:(2 * p + 2) * LANES] = jnp.where(lo, fill, blk).astype(BF16)


def _fox_body(x, win_ref, wvt_ref, bf_ref, scat_ref, onesq_ref, onesk_ref, q_out, k_out, vt_out, g_out, carry_ref,
              first_rows):
    tm = x.shape[0]

    if first_rows:
        @pl.when(pl.program_id(1) == 0)
        def _():
            carry_ref[...] = jnp.zeros_like(carry_ref)

    xb = x.astype(BF16)
    h = jnp.dot(xb, win_ref[...], preferred_element_type=F32)
    W = FOX_WIDTH
    vt = lax.dot_general(wvt_ref[...], xb, NT_DIMS, preferred_element_type=F32)
    g_out[...] = _silu(h[:, 2 * W:3 * W]).astype(BF16)
    _write_vt(vt, vt_out)

    f = h[:, 3 * W:3 * W + LANES] + bf_ref[...]
    logf = jnp.minimum(f, 0.0) - jnp.log(1.0 + jnp.exp(-jnp.abs(f)))

    def split3(a):
        hi = a.astype(BF16)
        r = a - hi.astype(F32)
        mid = r.astype(BF16)
        lo = (r - mid.astype(F32)).astype(BF16)
        return hi, mid, lo

    ri = lax.broadcasted_iota(jnp.int32, (tm, tm), 0)
    ci = lax.broadcasted_iota(jnp.int32, (tm, tm), 1)
    tri = (ci <= ri).astype(BF16)
    l_hi, l_mid, l_lo = split3(logf)
    cum = (jnp.dot(tri, l_hi, preferred_element_type=F32) + jnp.dot(tri, l_mid, preferred_element_type=F32)
           + jnp.dot(tri, l_lo, preferred_element_type=F32)) + carry_ref[...]
    carry_ref[...] = cum[tm - 1:tm, :]

    c_hi, c_mid, c_lo = split3(cum * LOG2E)
    lane = lax.broadcasted_iota(jnp.int32, (tm, LANES), 1)
    packed = jnp.where(lane < HEADS, c_hi.astype(F32),
                       jnp.where(lane < 2 * HEADS, pltpu.roll(c_mid.astype(F32), HEADS, 1),
                                 pltpu.roll(c_lo.astype(F32), 2 * HEADS, 1)))
    packed = jnp.where(lane < 3 * HEADS, packed, 0.0).astype(BF16)
    aug = jnp.dot(packed, scat_ref[...], preferred_element_type=F32)
    augq = aug[:, :W] + onesq_ref[...]
    augk = aug[:, W:] + onesk_ref[...]

    _split_pairs(h[:, :W] * (FOX_HEAD_DIM ** -0.5 * LOG2E), augq, q_out)
    _split_pairs(h[:, W:2 * W], augk, k_out)


def _swa_body(x, win_ref, q_out, k_out, v_out, g_out):
    h = jnp.dot(x.astype(BF16), win_ref[...], preferred_element_type=F32)
    c1 = SWA_WIDTH
    c2 = c1 + SWA_KV_WIDTH
    c3 = c2 + SWA_KV_WIDTH
    q_out[...] = (h[:, :c1] * (SWA_HEAD_DIM ** -0.5 * LOG2E)).astype(BF16)
    g_out[...] = _silu(h[:, c3:]).astype(BF16)
    lo = _lane_lt_half((x.shape[0], LANES))
    for pair in range(SWA_KV_HEADS // 2):
        kp = h[:, c1 + pair * LANES:c1 + (pair + 1) * LANES]
        vp = h[:, c2 + pair * LANES:c2 + (pair + 1) * LANES]
        ks, vs = pltpu.roll(kp, HALF, 1), pltpu.roll(vp, HALF, 1)
        for t, (k_h, v_h) in enumerate(((jnp.where(lo, kp, ks), jnp.where(lo, vp, vs)),
                                        (jnp.where(lo, ks, kp), jnp.where(lo, vs, vp)))):
            g = 2 * pair + t
            k_out[:, g * LANES:(g + 1) * LANES] = k_h.astype(BF16)
            v_out[:, (2 * g) * LANES:(2 * g + 1) * LANES] = jnp.where(lo, v_h, 1.0).astype(BF16)
            v_out[:, (2 * g + 1) * LANES:(2 * g + 2) * LANES] = jnp.where(lo, 1.0, v_h).astype(BF16)


VT = "vt"
_PROJ = {
    "mla": (_mla_body, 12, (PADW, PADW, VT, MLA_WIDTH), 6),
    "fox": (_fox_body, 6, (PADW, PADW, VT, FOX_WIDTH), 0),
    "swa": (_swa_body, 1, (SWA_WIDTH, SWA_KV_HEADS * LANES, SWA_KV_HEADS * 2 * LANES, SWA_WIDTH), 0),
}


def _out_body(o_ref, g_ref, x_ref, w_ref, lg_ref, lb_ref):
    a = (o_ref[...].astype(F32) * g_ref[...].astype(F32)).astype(BF16)
    y = jnp.dot(a, w_ref[...], preferred_element_type=F32)
    z = DEEPNORM_ALPHA * x_ref[...] + y
    mu = jnp.mean(z, axis=-1, keepdims=True)
    zc = z - mu
    var = jnp.mean(zc * zc, axis=-1, keepdims=True)
    return zc * lax.rsqrt(var + LN_EPS) * lg_ref[...] + lb_ref[...]


def _boundary_kernel(*refs, with_out, kind):
    refs = list(refs)
    if with_out:
        out_in, refs = refs[:6], refs[6:]
    else:
        x_ref, refs = refs[0], refs[1:]
    n_ops, widths, n_tab = _PROJ[kind][1:] if kind else (0, (), 0)
    proj_in, refs = refs[:n_ops], refs[n_ops:]
    if with_out:
        y_ref, refs = refs[0], refs[1:]
    outs, scratch = refs[:len(widths)], refs[len(widths):]
    tm = (out_in[2] if with_out else x_ref).shape[0]
    blocks = [pl.ds(r * SUB_ROWS, SUB_ROWS) for r in range(tm // SUB_ROWS)]
    xs = []
    for rows in blocks:
        if with_out:
            xs.append(_out_body(*(ref.at[rows] for ref in out_in[:3]), *out_in[3:]))
            y_ref[rows, :] = xs[-1]
        else:
            xs.append(x_ref[rows, :])
    for r, (rows, x) in enumerate(zip(blocks, xs)):
        if kind:
            extra = (r == 0,) if kind == "fox" else ()
            _PROJ[kind][0](x, *proj_in[:n_ops - n_tab], *(t.at[rows] for t in proj_in[n_ops - n_tab:]),
                           *(o.at[:, rows] if w == VT else o.at[rows] for o, w in zip(outs, widths)),
                           *scratch, *extra)


class _Layer:
    def __init__(self, arr, layer):
        self.arr, self.layer = arr, layer


def _boundary(x, out_ops, kind, proj_ops):
    B, S, _ = x.shape
    tm = ROW_TILE[kind]
    row = lambda w: pl.BlockSpec((None, tm, w), lambda b, i: (b, i, 0))

    def full(a):
        if isinstance(a, _Layer):
            idx = (a.layer,) + (0,) * (a.arr.ndim - 1)
            return pl.BlockSpec((None,) + a.arr.shape[1:], lambda b, i: idx, pipeline_mode=pl.Buffered(1))
        return pl.BlockSpec(a.shape, lambda b, i: (0,) * a.ndim, pipeline_mode=pl.Buffered(1))

    unwrap = lambda a: a.arr if isinstance(a, _Layer) else a
    tab = pl.BlockSpec((tm, LANES), lambda b, i: (i, 0))
    in_specs, args, out_specs, out_shape, scratch = [], [], [], [], []
    if out_ops is not None:
        o, sg, w, lg, lb = out_ops
        in_specs += [row(D_MODEL), row(D_MODEL), row(D_MODEL), full(w), full(lg), full(lb)]
        args += [o, sg, x, unwrap(w), unwrap(lg), unwrap(lb)]
        out_specs.append(row(D_MODEL))
        out_shape.append(jax.ShapeDtypeStruct((B, S, D_MODEL), F32))
    else:
        in_specs.append(row(D_MODEL))
        args.append(x)
    if kind:
        _, n_ops, widths, n_tab = _PROJ[kind]
        assert len(proj_ops) == n_ops
        in_specs += [full(a) for a in proj_ops[:n_ops - n_tab]] + [tab] * n_tab
        args += [unwrap(a) for a in proj_ops]
        for w in widths:
            if w == VT:
                assert TK % tm == 0
                out_specs.append(pl.BlockSpec((None, None, PADW, tm),
                                              lambda b, i: (b, i * tm // TK, 0, i % (TK // tm))))
                out_shape.append(jax.ShapeDtypeStruct((B, S // TK, PADW, TK), BF16))
            else:
                out_specs.append(row(w))
                out_shape.append(jax.ShapeDtypeStruct((B, S, w), BF16))
        if kind == "fox":
            scratch.append(pltpu.VMEM((1, LANES), F32))
    sem = ("parallel", "arbitrary" if kind == "fox" else "parallel")
    return pl.pallas_call(
        functools.partial(_boundary_kernel, with_out=out_ops is not None, kind=kind),
        grid=(B, S // tm),
        in_specs=in_specs,
        out_specs=out_specs,
        out_shape=out_shape,
        scratch_shapes=scratch,
        compiler_params=_params(sem),
        name="boundary_" + ("out_" if out_ops is not None else "") + (kind or "final"),
    )(*args)


def _normalize_pair(ext_even, ext_odd, extra_even=None, extra_odd=None):
    lo = _lane_lt_half(ext_even.shape)
    den = jnp.where(lo, ext_odd, ext_even)
    if extra_even is not None:
        den = den + jnp.where(lo, extra_odd, extra_even)
    return jnp.where(lo, ext_even, ext_odd) * pltpu.roll(1.0 / den, HALF, 1)


def _causal_attn_kernel(q_ref, k_ref, vt_ref, o_ref, m_ref, acc_ref):
    S = q_ref.shape[0]
    heads = range(ATT_HEADS)
    lanes = [slice(h * LANES, (h + 1) * LANES) for h in heads]

    def tile(q0, kt, jobs, diagonal):
        chains = [(h,) + job for job in jobs for h in heads]
        k0 = pl.multiple_of(kt * TK, TK)
        st_all = [lax.dot_general(k_ref[pl.ds(k0, nk), lanes[h]], q_ref[pl.ds(q0 + coff, nq), lanes[h]],
                                  NT_DIMS, preferred_element_type=F32) for h, coff, nq, nk in chains]
        pt_all, alpha_all = [], []
        for (h, coff, nq, nk), st in zip(chains, st_all):
            cols = slice(coff, coff + nq)
            blocks = [st[:, j * LANES:(j + 1) * LANES] for j in range(nq // LANES)]
            if diagonal:
                key = lax.broadcasted_iota(jnp.int32, (nk, LANES), 0)
                qry = lax.broadcasted_iota(jnp.int32, (nk, LANES), 1) + coff
                blocks = [b if nk - 1 <= coff + j * LANES else jnp.where(key <= qry + j * LANES, b, NEG_INF)
                          for j, b in enumerate(blocks)]
            m_blk = jnp.concatenate([jnp.max(b, axis=0, keepdims=True) for b in blocks], axis=1)
            if diagonal:
                m_new = m_blk
                alpha_all.append(None)
            else:
                m_old = m_ref[h, :, cols]
                m_new = jnp.maximum(m_old, m_blk)
                alpha_all.append(jnp.exp2(m_old - m_new))
            pt_all.append(jnp.concatenate(
                [jnp.exp2(b - m_new[:, j * LANES:(j + 1) * LANES]) for j, b in enumerate(blocks)],
                axis=1).astype(BF16))
            m_ref[h, :, cols] = m_new
        for (h, coff, nq, nk), pt, alpha in zip(chains, pt_all, alpha_all):
            cols = slice(coff, coff + nq)
            pv = jnp.dot(vt_ref[kt, lanes[h], 0:nk], pt, preferred_element_type=F32)
            acc_ref[h, :, cols] = pv if alpha is None else acc_ref[h, :, cols] * alpha + pv

    def q_body(qi, _):
        q0 = pl.multiple_of(qi * TQ, TQ)
        half = TQ // 2
        tile(q0, qi, [(0, half, half), (half, half, TQ)], True)

        def kv_body(kj, _):
            tile(q0, kj, [(0, TQ, TK)], False)
            return 0

        lax.fori_loop(0, qi, kv_body, 0)
        sub = 8
        for p in range(ATT_HEADS // 2):
            acc_e, acc_o = acc_ref[2 * p], acc_ref[2 * p + 1]
            r_e = jnp.concatenate([1.0 / acc_e[HALF:HALF + sub]] * (HALF // sub), axis=0)
            r_o = jnp.concatenate([1.0 / acc_o[0:sub]] * (HALF // sub), axis=0)
            pair_t = jnp.concatenate([acc_e[:HALF] * r_e, acc_o[HALF:] * r_o], axis=0)
            o_ref[pl.ds(q0, TQ), p * LANES:(p + 1) * LANES] = pair_t.T.astype(BF16)
        return 0

    lax.fori_loop(0, S // TQ, q_body, 0)


def _causal_attn(q, k, vt):
    B, S, _ = q.shape
    spec = pl.BlockSpec((None, S, ATT_HEADS * LANES), lambda b, p: (b, 0, p))
    return pl.pallas_call(
        _causal_attn_kernel,
        grid=(B, HEADS // ATT_HEADS),
        in_specs=[spec, spec, pl.BlockSpec((None, S // TK, ATT_HEADS * LANES, TK), lambda b, p: (b, 0, p, 0))],
        out_specs=pl.BlockSpec((None, S, ATT_HEADS * HALF), lambda b, p: (b, 0, p)),
        out_shape=jax.ShapeDtypeStruct((B, S, HEADS * HALF), BF16),
        scratch_shapes=[pltpu.VMEM((ATT_HEADS, 1, TQ), F32), pltpu.VMEM((ATT_HEADS, LANES, TQ), F32)],
        compiler_params=_params(("parallel", "parallel"), ATT_FLAGS),
        name="causal_attn",
    )(q, k, vt)


def _swa_attn_kernel(sink_ref, q_ref, kp_ref, kc_ref, vp_ref, vc_ref, bias0_ref, bias_ref, o_ref):
    nqb = q_ref.shape[0] // BLOCK
    lo = _lane_lt_half((BLOCK, LANES))
    zero = jnp.zeros((), BF16)
    chains = [(g, par) for g in range(SWA_KV_HEADS) for par in range(2)]

    def band(prev_ref, cur_ref, i, group):
        cols = slice(group * LANES, (group + 1) * LANES)
        if i == 0:
            return jnp.concatenate([prev_ref[:, cols], cur_ref[0:BLOCK, cols]], axis=0)
        return cur_ref[(i - 1) * BLOCK:(i + 1) * BLOCK, cols]

    def qk(i):
        rows = slice(i * BLOCK, (i + 1) * BLOCK)
        out = []
        for g in range(SWA_KV_HEADS):
            kb = band(kp_ref, kc_ref, i, g)
            qa = q_ref[rows, (2 * g) * LANES:(2 * g + 1) * LANES]
            qb = q_ref[rows, (2 * g + 1) * LANES:(2 * g + 2) * LANES]
            for par in range(2):
                keep = lo if par == 0 else jnp.logical_not(lo)
                stack = jnp.concatenate([jnp.where(keep, qa, zero), jnp.where(keep, qb, zero)], axis=0)
                out.append(lax.dot_general(stack, kb, NT_DIMS, preferred_element_type=F32))
        return out

    def softmax(i, s_list):
        b_ref = bias0_ref if i == 0 else bias_ref
        out = []
        for (g, par), s in zip(chains, s_list):
            ha, hb = 4 * g + par, 4 * g + 2 + par
            blocks = [s[:, j * LANES:(j + 1) * LANES]
                      + jnp.concatenate([b_ref[ha, :, j * LANES:(j + 1) * LANES],
                                         b_ref[hb, :, j * LANES:(j + 1) * LANES]], axis=0) for j in range(2)]
            sink = jnp.concatenate([jnp.full((BLOCK, LANES), sink_ref[ha], F32),
                                    jnp.full((BLOCK, LANES), sink_ref[hb], F32)], axis=0)
            m = jnp.maximum(sink, jnp.max(jnp.maximum(blocks[0], blocks[1]), axis=-1, keepdims=True))
            p = jnp.concatenate([jnp.exp2(b - m) for b in blocks], axis=1).astype(BF16)
            out.append((p, jnp.exp2(sink - m)))
        return out

    def pv(i, sm):
        ext, extra = {}, {}
        for (g, par), (p, ex) in zip(chains, sm):
            vb = band(vp_ref, vc_ref, i, 2 * g + par)
            ext[g, par] = jnp.dot(p, vb, preferred_element_type=F32)
            extra[g, par] = ex
        for g in range(SWA_KV_HEADS):
            for j in range(2):
                rows = slice(j * BLOCK, (j + 1) * BLOCK)
                o_ref[i * BLOCK:(i + 1) * BLOCK, (2 * g + j) * LANES:(2 * g + j + 1) * LANES] = _normalize_pair(
                    ext[g, 0][rows], ext[g, 1][rows], extra[g, 0][rows], extra[g, 1][rows]).astype(BF16)

    s_cur = qk(0)
    for i in range(nqb):
        s_next = qk(i + 1) if i + 1 < nqb else None
        pv(i, softmax(i, s_cur))
        s_cur = s_next


def _swa_attn(sinks2, q, k, v, bias):
    B, S, _ = q.shape
    rows = SWA_QB * BLOCK
    cur = lambda w: pl.BlockSpec((None, rows, w), lambda b, n: (b, n, 0))
    prev = lambda w: pl.BlockSpec((None, BLOCK, w), lambda b, n: (b, jnp.maximum(n * SWA_QB - 1, 0), 0))
    bias_shape = (None, SWA_Q_HEADS, BLOCK, 2 * BLOCK)
    kw, vw = k.shape[-1], v.shape[-1]
    return pl.pallas_call(
        _swa_attn_kernel,
        grid=(B, S // rows),
        in_specs=[pl.BlockSpec(memory_space=pltpu.SMEM), cur(SWA_WIDTH), prev(kw), cur(kw), prev(vw), cur(vw),
                  pl.BlockSpec(bias_shape, lambda b, n: (jnp.minimum(n, 1), 0, 0, 0)),
                  pl.BlockSpec(bias_shape, lambda b, n: (1, 0, 0, 0))],
        out_specs=cur(SWA_WIDTH),
        out_shape=jax.ShapeDtypeStruct((B, S, SWA_WIDTH), BF16),
        compiler_params=_params(("parallel", "parallel")),
        name="swa_attn",
    )(sinks2, q, k, k, v, v, bias, bias)


def _rot(w):
    half = w.shape[-1] // 2
    return jnp.concatenate([-w[..., half:], w[..., :half]], axis=-1)


def _rope_tables(seq, c0):
    inv = ROPE_THETA ** (-np.arange(0, MLA_ROPE_DIM, 2, dtype=np.float64) / MLA_ROPE_DIM)
    ang = np.arange(seq, dtype=np.float64)[:, None] * inv[None, :]
    cos = np.concatenate([np.cos(ang)] * 2, axis=-1)
    sin = np.concatenate([np.sin(ang)] * 2, axis=-1)
    z32 = np.zeros((seq, LANES - HALF - MLA_ROPE_DIM))
    z64 = np.zeros((seq, HALF))
    c64 = np.full((seq, HALF), c0)
    aq_even = np.concatenate([c64, c0 * cos, z32], axis=-1)
    bq_even = np.concatenate([z64, c0 * sin, z32], axis=-1)
    aq_odd = np.concatenate([c0 * cos, z32, c64], axis=-1)
    bq_odd = np.concatenate([c0 * sin, z32, z64], axis=-1)
    ak = np.concatenate([z64, cos, z32], axis=-1)
    bk = np.concatenate([z64, sin, z32], axis=-1)
    return tuple(t.astype(np.float32) for t in (aq_even, bq_even, aq_odd, bq_odd, ak, bk))


def _mla_operands(w_in, q_norm, w_q_up, kv_norm, w_kv_up):
    L = w_in.shape[0]
    c2 = MLA_Q_RANK + MLA_KV_RANK
    c3 = c2 + MLA_ROPE_DIM
    kr = w_in[..., c2:c3]
    krg = jnp.concatenate([jnp.zeros((L, D_MODEL, HALF), F32), kr, _rot(kr)], axis=-1)
    win = jnp.concatenate([w_in[..., :c2], krg, w_in[..., c3:]], axis=-1).astype(BF16)
    odd = (jnp.arange(MLA_HEADS) % 2 == 1)[:, None]
    wq = w_q_up.reshape(L, MLA_Q_RANK, MLA_HEADS, MLA_NOPE_DIM + MLA_ROPE_DIM)
    nope, rope = wq[..., :MLA_NOPE_DIM], wq[..., MLA_NOPE_DIM:]
    wq = jnp.where(odd, jnp.concatenate([rope, _rot(rope), nope], axis=-1),
                   jnp.concatenate([nope, rope, _rot(rope)], axis=-1))
    wq = wq.reshape(L, MLA_Q_RANK, PADW).astype(BF16)
    wkv = w_kv_up.reshape(L, MLA_KV_RANK, MLA_HEADS, MLA_NOPE_DIM + MLA_V_DIM)
    wk = wkv[..., :MLA_NOPE_DIM].reshape(L, MLA_KV_RANK, MLA_HEADS * MLA_NOPE_DIM).astype(BF16)
    wvt = jnp.swapaxes(wkv[..., MLA_NOPE_DIM:].reshape(L, MLA_KV_RANK, MLA_WIDTH), 1, 2).astype(BF16)
    return win, q_norm.reshape(L, 1, -1), kv_norm.reshape(L, 1, -1), wq, wk, wvt


def _fox_layer_operands(w_in, b_f):
    W = FOX_WIDTH
    wf = jnp.concatenate([w_in[:, 3 * W:3 * W + FOX_HEADS], jnp.zeros((D_MODEL, LANES - FOX_HEADS), F32)], axis=-1)
    win = jnp.concatenate([w_in[:, :2 * W], w_in[:, 3 * W + FOX_HEADS:], wf], axis=-1).astype(BF16)
    wvt = w_in[:, 2 * W:3 * W].T.astype(BF16)
    bf = jnp.concatenate([b_f, jnp.zeros((LANES - FOX_HEADS,), F32)]).reshape(1, LANES)
    h = np.arange(FOX_HEADS)
    base = (h // 2) * LANES + np.where(h % 2 == 0, HALF, 0)
    scat = np.zeros((LANES, 2 * W), np.float32)
    onesq = np.zeros((1, W), np.float32)
    onesk = np.zeros((1, W), np.float32)
    for j in range(3):
        scat[j * FOX_HEADS + h, base + j] = 1.0
        scat[j * FOX_HEADS + h, W + base + 3 + j] = -1.0
        onesq[0, base + 3 + j] = 1.0
        onesk[0, base + j] = 1.0
    return win, wvt, bf, jnp.asarray(scat, dtype=BF16), onesq, onesk


def _swa_bias():
    qi = np.arange(BLOCK)[:, None]
    kj = np.arange(2 * BLOCK)[None, :]
    dist = qi + BLOCK - kj
    valid = (dist >= 0) & (dist < WINDOW)
    slopes = 2.0 ** (-8.0 * np.arange(1, SWA_Q_HEADS + 1, dtype=np.float64) / SWA_Q_HEADS)
    ali = -(slopes[:, None, None] * dist.astype(np.float64)[None]) * LOG2E
    bias_rest = np.where(valid[None], ali, NEG_INF)
    bias_first = np.where((valid & (kj >= BLOCK))[None], ali, NEG_INF)
    return np.stack([bias_first, bias_rest]).astype(np.float32)


def kernel(x, ln_g, ln_b, mla_w_in, mla_q_norm, mla_w_q_up, mla_kv_norm, mla_w_kv_up, mla_w_out,
           swa_w_in, swa_sinks, swa_w_out, fox_w_in, fox_b_f, fox_w_out):
    seq = x.shape[1]
    mla_ops = _mla_operands(mla_w_in, mla_q_norm, mla_w_q_up, mla_kv_norm, mla_w_kv_up)
    mla_tables = _rope_tables(seq, (MLA_NOPE_DIM + MLA_ROPE_DIM) ** -0.5 * LOG2E)
    w_outs = {"mla": mla_w_out.astype(BF16), "swa": swa_w_out.astype(BF16), "fox": fox_w_out.astype(BF16)}
    ln_g3, ln_b3 = ln_g.reshape(DEPTH, 1, D_MODEL), ln_b.reshape(DEPTH, 1, D_MODEL)
    out_ops = None
    for i in range(DEPTH + 1):
        j = i // N_MIXERS
        kind = ("mla", "swa", "fox")[i % N_MIXERS] if i < DEPTH else None
        if kind == "mla":
            proj_ops = tuple(_Layer(a, j) for a in mla_ops) + mla_tables
        elif kind == "swa":
            proj_ops = (swa_w_in[j].astype(BF16),)
        elif kind == "fox":
            proj_ops = _fox_layer_operands(fox_w_in[j], fox_b_f[j])
        else:
            proj_ops = ()
        res = _boundary(x, out_ops, kind, proj_ops)
        if out_ops is not None:
            x, res = res[0], res[1:]
        if kind is None:
            return x
        q, k, v, sg = res
        if kind == "swa":
            o = _swa_attn(swa_sinks[j].astype(F32) * LOG2E, q, k, v, _swa_bias())
        else:
            o = _causal_attn(q, k, v)
        out_ops = (o, sg, _Layer(w_outs[kind], j), _Layer(ln_g3, i), _Layer(ln_b3, i))
```

```python
import functools
import math

import jax
import jax.numpy as jnp
import numpy as np
from jax import lax
from jax.experimental import pallas as pl
from jax.experimental.pallas import tpu as pltpu

D_MODEL = 1024
DEPTH = 4
N_MIXERS = 3
BLOCK = 128
NEG_INF = -1e30
RMS_EPS = 1e-6
LN_EPS = 1e-5
DEEPNORM_ALPHA = (2 * DEPTH) ** 0.25
LOG2E = math.log2(math.e)

MLA_HEADS = 16
MLA_NOPE_DIM = 64
MLA_ROPE_DIM = 32
MLA_V_DIM = 64
MLA_Q_RANK = 384
MLA_KV_RANK = 256
ROPE_THETA = 10000.0
MLA_WIDTH = MLA_HEADS * MLA_V_DIM

SWA_Q_HEADS = 16
SWA_KV_HEADS = 4
SWA_HEAD_DIM = 64
WINDOW = 128
SWA_WIDTH = SWA_Q_HEADS * SWA_HEAD_DIM
SWA_KV_WIDTH = SWA_KV_HEADS * SWA_HEAD_DIM

FOX_HEADS = 16
FOX_HEAD_DIM = 64
FOX_WIDTH = FOX_HEADS * FOX_HEAD_DIM

LANES = 128
HALF = LANES // 2
HEADS = 16
PADW = HEADS * LANES

ROW_TILE = 512
SUB_ROWS = 256
TQ = 512
TK = 512
ATT_HEADS = 8
SWA_QB = 4
VMEM_LIMIT = 56 * 1024 * 1024

F32 = jnp.float32
BF16 = jnp.bfloat16
NT_DIMS = (((1,), (1,)), ((), ()))


def _params(sem):
    return pltpu.CompilerParams(dimension_semantics=sem, vmem_limit_bytes=VMEM_LIMIT)


def _lane_lt_half(shape):
    return lax.broadcasted_iota(jnp.int32, shape, len(shape) - 1) < HALF


def _silu(g):
    return g / (1.0 + jnp.exp(-g))


def _write_vt(vt, vt_out):
    top = lax.broadcasted_iota(jnp.int32, (LANES, vt.shape[1]), 0) < HALF
    for p in range(HEADS // 2):
        blk = vt[p * LANES:(p + 1) * LANES, :]
        vt_out[(2 * p) * LANES:(2 * p + 1) * LANES, :] = jnp.where(top, blk, 1.0).astype(BF16)
        vt_out[(2 * p + 1) * LANES:(2 * p + 2) * LANES, :] = jnp.where(top, 1.0, blk).astype(BF16)


def _mla_body(x, win_ref, qg_ref, kg_ref, wq_ref, wk_ref, wvt_ref, aqe_ref, bqe_ref, aqo_ref, bqo_ref,
              ak_ref, bk_ref, q_out, k_out, vt_out, g_out):
    h = jnp.dot(x.astype(BF16), win_ref[...], preferred_element_type=F32)
    c1 = MLA_Q_RANK
    c2 = c1 + MLA_KV_RANK
    c3 = c2 + LANES
    cq, ckv, krg, gate = h[:, :c1], h[:, c1:c2], h[:, c2:c3], h[:, c3:]
    g_out[...] = _silu(gate).astype(BF16)

    qn = cq * lax.rsqrt(jnp.mean(cq * cq, axis=-1, keepdims=True) + RMS_EPS) * qg_ref[...]
    kn = (ckv * lax.rsqrt(jnp.mean(ckv * ckv, axis=-1, keepdims=True) + RMS_EPS) * kg_ref[...]).astype(BF16)
    q = jnp.dot(qn.astype(BF16), wq_ref[...], preferred_element_type=F32)
    kc = jnp.dot(kn, wk_ref[...], preferred_element_type=F32)
    _write_vt(lax.dot_general(wvt_ref[...], kn, NT_DIMS, preferred_element_type=F32), vt_out)

    shift = LANES - MLA_ROPE_DIM
    aq = (aqe_ref[...], aqo_ref[...])
    bq = (bqe_ref[...], bqo_ref[...])
    kr_hi = krg * ak_ref[...] + pltpu.roll(krg, shift, 1) * bk_ref[...]
    kr_lo = pltpu.roll(kr_hi, HALF, 1)
    lo = _lane_lt_half((x.shape[0], LANES))
    for p in range(HEADS // 2):
        e = slice((2 * p) * LANES, (2 * p + 1) * LANES)
        o = slice((2 * p + 1) * LANES, (2 * p + 2) * LANES)
        kblk = kc[:, p * LANES:(p + 1) * LANES]
        q_out[:, e] = (q[:, e] * aq[0] + pltpu.roll(q[:, e], shift, 1) * bq[0]).astype(BF16)
        q_out[:, o] = (q[:, o] * aq[1] + pltpu.roll(q[:, o], shift, 1) * bq[1]).astype(BF16)
        k_out[:, e] = jnp.where(lo, kblk, kr_hi).astype(BF16)
        k_out[:, o] = jnp.where(lo, kr_lo, kblk).astype(BF16)


def _split_pairs(nat, aug, out_ref):
    lo = _lane_lt_half((nat.shape[0], LANES))
    for p in range(HEADS // 2):
        blk = nat[:, p * LANES:(p + 1) * LANES]
        fill = aug[:, p * LANES:(p + 1) * LANES]
        out_ref[:, (2 * p) * LANES:(2 * p + 1) * LANES] = jnp.where(lo, blk, fill).astype(BF16)
        out_ref[:, (2 * p + 1) * LANES:(2 * p + 2) * LANES] = jnp.where(lo, fill, blk).astype(BF16)


def _fox_body(x, win_ref, wvt_ref, bf_ref, scat_ref, onesq_ref, onesk_ref, q_out, k_out, vt_out, g_out, carry_ref,
              first_rows):
    tm = x.shape[0]

    if first_rows:
        @pl.when(pl.program_id(1) == 0)
        def _():
            carry_ref[...] = jnp.zeros_like(carry_ref)

    xb = x.astype(BF16)
    h = jnp.dot(xb, win_ref[...], preferred_element_type=F32)
    W = FOX_WIDTH
    g_out[...] = _silu(h[:, 2 * W:3 * W]).astype(BF16)
    _write_vt(lax.dot_general(wvt_ref[...], xb, NT_DIMS, preferred_element_type=F32), vt_out)

    f = h[:, 3 * W:3 * W + LANES] + bf_ref[...]
    logf = jnp.minimum(f, 0.0) - jnp.log(1.0 + jnp.exp(-jnp.abs(f)))

    def split3(a):
        hi = a.astype(BF16)
        r = a - hi.astype(F32)
        mid = r.astype(BF16)
        lo = (r - mid.astype(F32)).astype(BF16)
        return hi, mid, lo

    ri = lax.broadcasted_iota(jnp.int32, (tm, tm), 0)
    ci = lax.broadcasted_iota(jnp.int32, (tm, tm), 1)
    tri = (ci <= ri).astype(BF16)
    l_hi, l_mid, l_lo = split3(logf)
    cum = (jnp.dot(tri, l_hi, preferred_element_type=F32) + jnp.dot(tri, l_mid, preferred_element_type=F32)
           + jnp.dot(tri, l_lo, preferred_element_type=F32)) + carry_ref[...]
    carry_ref[...] = cum[tm - 1:tm, :]

    c_hi, c_mid, c_lo = split3(cum * LOG2E)
    lane = lax.broadcasted_iota(jnp.int32, (tm, LANES), 1)
    packed = jnp.where(lane < HEADS, c_hi.astype(F32),
                       jnp.where(lane < 2 * HEADS, pltpu.roll(c_mid.astype(F32), HEADS, 1),
                                 pltpu.roll(c_lo.astype(F32), 2 * HEADS, 1)))
    packed = jnp.where(lane < 3 * HEADS, packed, 0.0).astype(BF16)
    aug = jnp.dot(packed, scat_ref[...], preferred_element_type=F32)
    augq = aug[:, :W] + onesq_ref[...]
    augk = aug[:, W:] + onesk_ref[...]

    _split_pairs(h[:, :W] * (FOX_HEAD_DIM ** -0.5 * LOG2E), augq, q_out)
    _split_pairs(h[:, W:2 * W], augk, k_out)


def _swa_body(x, win_ref, q_out, k_out, v_out, g_out):
    h = jnp.dot(x.astype(BF16), win_ref[...], preferred_element_type=F32)
    c1 = SWA_WIDTH
    c2 = c1 + SWA_KV_WIDTH
    c3 = c2 + SWA_KV_WIDTH
    q_out[...] = (h[:, :c1] * (SWA_HEAD_DIM ** -0.5 * LOG2E)).astype(BF16)
    g_out[...] = _silu(h[:, c3:]).astype(BF16)
    lo = _lane_lt_half((x.shape[0], LANES))
    for pair in range(SWA_KV_HEADS // 2):
        kp = h[:, c1 + pair * LANES:c1 + (pair + 1) * LANES]
        vp = h[:, c2 + pair * LANES:c2 + (pair + 1) * LANES]
        ks, vs = pltpu.roll(kp, HALF, 1), pltpu.roll(vp, HALF, 1)
        for t, (k_h, v_h) in enumerate(((jnp.where(lo, kp, ks), jnp.where(lo, vp, vs)),
                                        (jnp.where(lo, ks, kp), jnp.where(lo, vs, vp)))):
            g = 2 * pair + t
            k_out[:, g * LANES:(g + 1) * LANES] = k_h.astype(BF16)
            v_out[:, (2 * g) * LANES:(2 * g + 1) * LANES] = jnp.where(lo, v_h, 1.0).astype(BF16)
            v_out[:, (2 * g + 1) * LANES:(2 * g + 2) * LANES] = jnp.where(lo, 1.0, v_h).astype(BF16)


VT = "vt"
_PROJ = {
    "mla": (_mla_body, 12, (PADW, PADW, VT, MLA_WIDTH), 6),
    "fox": (_fox_body, 6, (PADW, PADW, VT, FOX_WIDTH), 0),
    "swa": (_swa_body, 1, (SWA_WIDTH, SWA_KV_HEADS * LANES, SWA_KV_HEADS * 2 * LANES, SWA_WIDTH), 0),
}


def _out_body(o_ref, g_ref, x_ref, w_ref, lg_ref, lb_ref):
    a = (o_ref[...].astype(F32) * g_ref[...].astype(F32)).astype(BF16)
    y = jnp.dot(a, w_ref[...], preferred_element_type=F32)
    z = DEEPNORM_ALPHA * x_ref[...] + y
    mu = jnp.mean(z, axis=-1, keepdims=True)
    zc = z - mu
    var = jnp.mean(zc * zc, axis=-1, keepdims=True)
    return zc * lax.rsqrt(var + LN_EPS) * lg_ref[...] + lb_ref[...]


def _boundary_kernel(*refs, with_out, kind):
    refs = list(refs)
    if with_out:
        out_in, refs = refs[:6], refs[6:]
    else:
        x_ref, refs = refs[0], refs[1:]
    n_ops, widths, n_tab = _PROJ[kind][1:] if kind else (0, (), 0)
    proj_in, refs = refs[:n_ops], refs[n_ops:]
    if with_out:
        y_ref, refs = refs[0], refs[1:]
    outs, scratch = refs[:len(widths)], refs[len(widths):]
    tm = (out_in[2] if with_out else x_ref).shape[0]
    blocks = [pl.ds(r * SUB_ROWS, SUB_ROWS) for r in range(tm // SUB_ROWS)]
    xs = []
    for rows in blocks:
        if with_out:
            xs.append(_out_body(*(ref.at[rows] for ref in out_in[:3]), *out_in[3:]))
            y_ref[rows, :] = xs[-1]
        else:
            xs.append(x_ref[rows, :])
    for r, (rows, x) in enumerate(zip(blocks, xs)):
        if kind:
            extra = (r == 0,) if kind == "fox" else ()
            _PROJ[kind][0](x, *proj_in[:n_ops - n_tab], *(t.at[rows] for t in proj_in[n_ops - n_tab:]),
                           *(o.at[:, rows] if w == VT else o.at[rows] for o, w in zip(outs, widths)),
                           *scratch, *extra)


class _Layer:
    def __init__(self, arr, layer):
        self.arr, self.layer = arr, layer


def _boundary(x, out_ops, kind, proj_ops):
    B, S, _ = x.shape
    tm = ROW_TILE
    row = lambda w: pl.BlockSpec((None, tm, w), lambda b, i: (b, i, 0))

    def full(a):
        if isinstance(a, _Layer):
            idx = (a.layer,) + (0,) * (a.arr.ndim - 1)
            return pl.BlockSpec((None,) + a.arr.shape[1:], lambda b, i: idx, pipeline_mode=pl.Buffered(1))
        return pl.BlockSpec(a.shape, lambda b, i: (0,) * a.ndim, pipeline_mode=pl.Buffered(1))

    unwrap = lambda a: a.arr if isinstance(a, _Layer) else a
    tab = pl.BlockSpec((tm, LANES), lambda b, i: (i, 0))
    in_specs, args, out_specs, out_shape, scratch = [], [], [], [], []
    if out_ops is not None:
        o, sg, w, lg, lb = out_ops
        in_specs += [row(D_MODEL), row(D_MODEL), row(D_MODEL), full(w), full(lg), full(lb)]
        args += [o, sg, x, unwrap(w), unwrap(lg), unwrap(lb)]
        out_specs.append(row(D_MODEL))
        out_shape.append(jax.ShapeDtypeStruct((B, S, D_MODEL), F32))
    else:
        in_specs.append(row(D_MODEL))
        args.append(x)
    if kind:
        _, n_ops, widths, n_tab = _PROJ[kind]
        assert len(proj_ops) == n_ops
        in_specs += [full(a) for a in proj_ops[:n_ops - n_tab]] + [tab] * n_tab
        args += [unwrap(a) for a in proj_ops]
        for w in widths:
            if w == VT:
                assert TK % tm == 0
                out_specs.append(pl.BlockSpec((None, None, PADW, tm),
                                              lambda b, i: (b, i * tm // TK, 0, i % (TK // tm))))
                out_shape.append(jax.ShapeDtypeStruct((B, S // TK, PADW, TK), BF16))
            else:
                out_specs.append(row(w))
                out_shape.append(jax.ShapeDtypeStruct((B, S, w), BF16))
        if kind == "fox":
            scratch.append(pltpu.VMEM((1, LANES), F32))
    sem = ("parallel", "arbitrary" if kind == "fox" else "parallel")
    return pl.pallas_call(
        functools.partial(_boundary_kernel, with_out=out_ops is not None, kind=kind),
        grid=(B, S // tm),
        in_specs=in_specs,
        out_specs=out_specs,
        out_shape=out_shape,
        scratch_shapes=scratch,
        compiler_params=_params(sem),
        name="boundary_" + ("out_" if out_ops is not None else "") + (kind or "final"),
    )(*args)


def _normalize_pair(ext_even, ext_odd, extra_even=None, extra_odd=None):
    lo = _lane_lt_half(ext_even.shape)
    den = jnp.where(lo, ext_odd, ext_even)
    if extra_even is not None:
        den = den + jnp.where(lo, extra_odd, extra_even)
    return jnp.where(lo, ext_even, ext_odd) * pltpu.roll(1.0 / den, HALF, 1)


def _causal_attn_kernel(q_ref, k_ref, vt_ref, o_ref, m_ref, acc_ref):
    S = q_ref.shape[0]
    heads = range(ATT_HEADS)
    lanes = [slice(h * LANES, (h + 1) * LANES) for h in heads]

    def tile(q0, kt, jobs, diagonal):
        chains = [(h,) + job for job in jobs for h in heads]
        k0 = pl.multiple_of(kt * TK, TK)
        s_all = [lax.dot_general(q_ref[pl.ds(q0 + roff, nq), lanes[h]], k_ref[pl.ds(k0, nk), lanes[h]],
                                 NT_DIMS, preferred_element_type=F32) for h, roff, nq, nk in chains]
        p_all, alpha_all = [], []
        for (h, roff, nq, nk), s in zip(chains, s_all):
            st = slice(roff, roff + nq)
            blocks = [s[:, j * LANES:(j + 1) * LANES] for j in range(nk // LANES)]
            if diagonal:
                row = lax.broadcasted_iota(jnp.int32, (nq, LANES), 0) + roff
                col = lax.broadcasted_iota(jnp.int32, (nq, LANES), 1)
                blocks = [b if (j + 1) * LANES - 1 <= roff else jnp.where(col + j * LANES <= row, b, NEG_INF)
                          for j, b in enumerate(blocks)]
            m_blk = jnp.max(functools.reduce(jnp.maximum, blocks), axis=-1, keepdims=True)
            if diagonal:
                m_new = jnp.broadcast_to(m_blk, (nq, LANES))
                alpha_all.append(None)
            else:
                m_old = m_ref[h, st, :]
                m_new = jnp.maximum(m_old, m_blk)
                alpha_all.append(jnp.exp2(m_old - m_new))
            p_all.append(jnp.concatenate([jnp.exp2(b - m_new) for b in blocks], axis=1).astype(BF16))
            m_ref[h, st, :] = m_new
        for (h, roff, nq, nk), p, alpha in zip(chains, p_all, alpha_all):
            st = slice(roff, roff + nq)
            pv = lax.dot_general(p, vt_ref[kt, lanes[h], 0:nk], NT_DIMS, preferred_element_type=F32)
            acc_ref[h, st, :] = pv if alpha is None else acc_ref[h, st, :] * alpha + pv

    def q_body(qi, _):
        q0 = pl.multiple_of(qi * TQ, TQ)
        half = TQ // 2
        tile(q0, qi, [(0, half, half), (half, half, TQ)], True)

        def kv_body(kj, _):
            tile(q0, kj, [(0, TQ, TK)], False)
            return 0

        lax.fori_loop(0, qi, kv_body, 0)
        for p in range(ATT_HEADS // 2):
            o_ref[pl.ds(q0, TQ), p * LANES:(p + 1) * LANES] = _normalize_pair(
                acc_ref[2 * p], acc_ref[2 * p + 1]).astype(BF16)
        return 0

    lax.fori_loop(0, S // TQ, q_body, 0)


def _causal_attn(q, k, vt):
    B, S, _ = q.shape
    spec = pl.BlockSpec((None, S, ATT_HEADS * LANES), lambda b, p: (b, 0, p))
    return pl.pallas_call(
        _causal_attn_kernel,
        grid=(B, HEADS // ATT_HEADS),
        in_specs=[spec, spec, pl.BlockSpec((None, S // TK, ATT_HEADS * LANES, TK), lambda b, p: (b, 0, p, 0))],
        out_specs=pl.BlockSpec((None, S, ATT_HEADS * HALF), lambda b, p: (b, 0, p)),
        out_shape=jax.ShapeDtypeStruct((B, S, HEADS * HALF), BF16),
        scratch_shapes=[pltpu.VMEM((ATT_HEADS, TQ, LANES), F32), pltpu.VMEM((ATT_HEADS, TQ, LANES), F32)],
        compiler_params=_params(("parallel", "parallel")),
        name="causal_attn",
    )(q, k, vt)


def _swa_attn_kernel(sink_ref, q_ref, kp_ref, kc_ref, vp_ref, vc_ref, bias0_ref, bias_ref, o_ref):
    nqb = q_ref.shape[0] // BLOCK
    lo = _lane_lt_half((BLOCK, LANES))
    zero = jnp.zeros((), BF16)
    chains = [(g, par) for g in range(SWA_KV_HEADS) for par in range(2)]

    def band(prev_ref, cur_ref, i, group):
        cols = slice(group * LANES, (group + 1) * LANES)
        if i == 0:
            return jnp.concatenate([prev_ref[:, cols], cur_ref[0:BLOCK, cols]], axis=0)
        return cur_ref[(i - 1) * BLOCK:(i + 1) * BLOCK, cols]

    def qk(i):
        rows = slice(i * BLOCK, (i + 1) * BLOCK)
        out = []
        for g in range(SWA_KV_HEADS):
            kb = band(kp_ref, kc_ref, i, g)
            qa = q_ref[rows, (2 * g) * LANES:(2 * g + 1) * LANES]
            qb = q_ref[rows, (2 * g + 1) * LANES:(2 * g + 2) * LANES]
            for par in range(2):
                keep = lo if par == 0 else jnp.logical_not(lo)
                stack = jnp.concatenate([jnp.where(keep, qa, zero), jnp.where(keep, qb, zero)], axis=0)
                out.append(lax.dot_general(stack, kb, NT_DIMS, preferred_element_type=F32))
        return out

    def softmax(i, s_list):
        b_ref = bias0_ref if i == 0 else bias_ref
        out = []
        for (g, par), s in zip(chains, s_list):
            ha, hb = 4 * g + par, 4 * g + 2 + par
            blocks = [s[:, j * LANES:(j + 1) * LANES]
                      + jnp.concatenate([b_ref[ha, :, j * LANES:(j + 1) * LANES],
                                         b_ref[hb, :, j * LANES:(j + 1) * LANES]], axis=0) for j in range(2)]
            sink = jnp.concatenate([jnp.full((BLOCK, LANES), sink_ref[ha], F32),
                                    jnp.full((BLOCK, LANES), sink_ref[hb], F32)], axis=0)
            m = jnp.maximum(sink, jnp.max(jnp.maximum(blocks[0], blocks[1]), axis=-1, keepdims=True))
            p = jnp.concatenate([jnp.exp2(b - m) for b in blocks], axis=1).astype(BF16)
            out.append((p, jnp.exp2(sink - m)))
        return out

    def pv(i, sm):
        ext, extra = {}, {}
        for (g, par), (p, ex) in zip(chains, sm):
            vb = band(vp_ref, vc_ref, i, 2 * g + par)
            ext[g, par] = jnp.dot(p, vb, preferred_element_type=F32)
            extra[g, par] = ex
        for g in range(SWA_KV_HEADS):
            for j in range(2):
                rows = slice(j * BLOCK, (j + 1) * BLOCK)
                o_ref[i * BLOCK:(i + 1) * BLOCK, (2 * g + j) * LANES:(2 * g + j + 1) * LANES] = _normalize_pair(
                    ext[g, 0][rows], ext[g, 1][rows], extra[g, 0][rows], extra[g, 1][rows]).astype(BF16)

    s_cur = qk(0)
    for i in range(nqb):
        s_next = qk(i + 1) if i + 1 < nqb else None
        pv(i, softmax(i, s_cur))
        s_cur = s_next


def _swa_attn(sinks2, q, k, v, bias):
    B, S, _ = q.shape
    rows = SWA_QB * BLOCK
    cur = lambda w: pl.BlockSpec((None, rows, w), lambda b, n: (b, n, 0))
    prev = lambda w: pl.BlockSpec((None, BLOCK, w), lambda b, n: (b, jnp.maximum(n * SWA_QB - 1, 0), 0))
    bias_shape = (None, SWA_Q_HEADS, BLOCK, 2 * BLOCK)
    kw, vw = k.shape[-1], v.shape[-1]
    return pl.pallas_call(
        _swa_attn_kernel,
        grid=(B, S // rows),
        in_specs=[pl.BlockSpec(memory_space=pltpu.SMEM), cur(SWA_WIDTH), prev(kw), cur(kw), prev(vw), cur(vw),
                  pl.BlockSpec(bias_shape, lambda b, n: (jnp.minimum(n, 1), 0, 0, 0)),
                  pl.BlockSpec(bias_shape, lambda b, n: (1, 0, 0, 0))],
        out_specs=cur(SWA_WIDTH),
        out_shape=jax.ShapeDtypeStruct((B, S, SWA_WIDTH), BF16),
        compiler_params=_params(("parallel", "parallel")),
        name="swa_attn",
    )(sinks2, q, k, k, v, v, bias, bias)


def _rot(w):
    half = w.shape[-1] // 2
    return jnp.concatenate([-w[..., half:], w[..., :half]], axis=-1)


def _rope_tables(seq, c0):
    inv = ROPE_THETA ** (-np.arange(0, MLA_ROPE_DIM, 2, dtype=np.float64) / MLA_ROPE_DIM)
    ang = np.arange(seq, dtype=np.float64)[:, None] * inv[None, :]
    cos = np.concatenate([np.cos(ang)] * 2, axis=-1)
    sin = np.concatenate([np.sin(ang)] * 2, axis=-1)
    z32 = np.zeros((seq, LANES - HALF - MLA_ROPE_DIM))
    z64 = np.zeros((seq, HALF))
    c64 = np.full((seq, HALF), c0)
    aq_even = np.concatenate([c64, c0 * cos, z32], axis=-1)
    bq_even = np.concatenate([z64, c0 * sin, z32], axis=-1)
    aq_odd = np.concatenate([c0 * cos, z32, c64], axis=-1)
    bq_odd = np.concatenate([c0 * sin, z32, z64], axis=-1)
    ak = np.concatenate([z64, cos, z32], axis=-1)
    bk = np.concatenate([z64, sin, z32], axis=-1)
    return tuple(t.astype(np.float32) for t in (aq_even, bq_even, aq_odd, bq_odd, ak, bk))


def _mla_operands(w_in, q_norm, w_q_up, kv_norm, w_kv_up):
    L = w_in.shape[0]
    c2 = MLA_Q_RANK + MLA_KV_RANK
    c3 = c2 + MLA_ROPE_DIM
    kr = w_in[..., c2:c3]
    krg = jnp.concatenate([jnp.zeros((L, D_MODEL, HALF), F32), kr, _rot(kr)], axis=-1)
    win = jnp.concatenate([w_in[..., :c2], krg, w_in[..., c3:]], axis=-1).astype(BF16)
    odd = (jnp.arange(MLA_HEADS) % 2 == 1)[:, None]
    wq = w_q_up.reshape(L, MLA_Q_RANK, MLA_HEADS, MLA_NOPE_DIM + MLA_ROPE_DIM)
    nope, rope = wq[..., :MLA_NOPE_DIM], wq[..., MLA_NOPE_DIM:]
    wq = jnp.where(odd, jnp.concatenate([rope, _rot(rope), nope], axis=-1),
                   jnp.concatenate([nope, rope, _rot(rope)], axis=-1))
    wq = wq.reshape(L, MLA_Q_RANK, PADW).astype(BF16)
    wkv = w_kv_up.reshape(L, MLA_KV_RANK, MLA_HEADS, MLA_NOPE_DIM + MLA_V_DIM)
    wk = wkv[..., :MLA_NOPE_DIM].reshape(L, MLA_KV_RANK, MLA_HEADS * MLA_NOPE_DIM).astype(BF16)
    wvt = jnp.swapaxes(wkv[..., MLA_NOPE_DIM:].reshape(L, MLA_KV_RANK, MLA_WIDTH), 1, 2).astype(BF16)
    return win, q_norm.reshape(L, 1, -1), kv_norm.reshape(L, 1, -1), wq, wk, wvt


def _fox_layer_operands(w_in, b_f):
    W = FOX_WIDTH
    wf = jnp.concatenate([w_in[:, 3 * W:3 * W + FOX_HEADS], jnp.zeros((D_MODEL, LANES - FOX_HEADS), F32)], axis=-1)
    win = jnp.concatenate([w_in[:, :2 * W], w_in[:, 3 * W + FOX_HEADS:], wf], axis=-1).astype(BF16)
    wvt = w_in[:, 2 * W:3 * W].T.astype(BF16)
    bf = jnp.concatenate([b_f, jnp.zeros((LANES - FOX_HEADS,), F32)]).reshape(1, LANES)
    h = np.arange(FOX_HEADS)
    base = (h // 2) * LANES + np.where(h % 2 == 0, HALF, 0)
    scat = np.zeros((LANES, 2 * W), np.float32)
    onesq = np.zeros((1, W), np.float32)
    onesk = np.zeros((1, W), np.float32)
    for j in range(3):
        scat[j * FOX_HEADS + h, base + j] = 1.0
        scat[j * FOX_HEADS + h, W + base + 3 + j] = -1.0
        onesq[0, base + 3 + j] = 1.0
        onesk[0, base + j] = 1.0
    return win, wvt, bf, jnp.asarray(scat, dtype=BF16), onesq, onesk


def _swa_bias():
    qi = np.arange(BLOCK)[:, None]
    kj = np.arange(2 * BLOCK)[None, :]
    dist = qi + BLOCK - kj
    valid = (dist >= 0) & (dist < WINDOW)
    slopes = 2.0 ** (-8.0 * np.arange(1, SWA_Q_HEADS + 1, dtype=np.float64) / SWA_Q_HEADS)
    ali = -(slopes[:, None, None] * dist.astype(np.float64)[None]) * LOG2E
    bias_rest = np.where(valid[None], ali, NEG_INF)
    bias_first = np.where((valid & (kj >= BLOCK))[None], ali, NEG_INF)
    return np.stack([bias_first, bias_rest]).astype(np.float32)


def kernel(x, ln_g, ln_b, mla_w_in, mla_q_norm, mla_w_q_up, mla_kv_norm, mla_w_kv_up, mla_w_out,
           swa_w_in, swa_sinks, swa_w_out, fox_w_in, fox_b_f, fox_w_out):
    seq = x.shape[1]
    mla_ops = _mla_operands(mla_w_in, mla_q_norm, mla_w_q_up, mla_kv_norm, mla_w_kv_up)
    mla_tables = _rope_tables(seq, (MLA_NOPE_DIM + MLA_ROPE_DIM) ** -0.5 * LOG2E)
    w_outs = {"mla": mla_w_out.astype(BF16), "swa": swa_w_out.astype(BF16), "fox": fox_w_out.astype(BF16)}
    ln_g3, ln_b3 = ln_g.reshape(DEPTH, 1, D_MODEL), ln_b.reshape(DEPTH, 1, D_MODEL)
    out_ops = None
    for i in range(DEPTH + 1):
        j = i // N_MIXERS
        kind = ("mla", "swa", "fox")[i % N_MIXERS] if i < DEPTH else None
        if kind == "mla":
            proj_ops = tuple(_Layer(a, j) for a in mla_ops) + mla_tables
        elif kind == "swa":
            proj_ops = (swa_w_in[j].astype(BF16),)
        elif kind == "fox":
            proj_ops = _fox_layer_operands(fox_w_in[j], fox_b_f[j])
        else:
            proj_ops = ()
        res = _boundary(x, out_ops, kind, proj_ops)
        if out_ops is not None:
            x, res = res[0], res[1:]
        if kind is None:
            return x
        q, k, v, sg = res
        if kind == "swa":
            o = _swa_attn(swa_sinks[j].astype(F32) * LOG2E, q, k, v, _swa_bias())
        else:
            o = _causal_attn(q, k, v)
        out_ops = (o, sg, _Layer(w_outs[kind], j), _Layer(ln_g3, i), _Layer(ln_b3, i))
```

```python
import functools
import math

import jax
import jax.numpy as jnp
import numpy as np
from jax import lax
from jax.experimental import pallas as pl
from jax.experimental.pallas import tpu as pltpu

D_MODEL = 1024
DEPTH = 4
N_MIXERS = 3
BLOCK = 128
NEG_INF = -1e30
RMS_EPS = 1e-6
LN_EPS = 1e-5
DEEPNORM_ALPHA = (2 * DEPTH) ** 0.25
LOG2E = math.log2(math.e)

MLA_HEADS = 16
MLA_NOPE_DIM = 64
MLA_ROPE_DIM = 32
MLA_V_DIM = 64
MLA_Q_RANK = 384
MLA_KV_RANK = 256
ROPE_THETA = 10000.0
MLA_WIDTH = MLA_HEADS * MLA_V_DIM

SWA_Q_HEADS = 16
SWA_KV_HEADS = 4
SWA_HEAD_DIM = 64
WINDOW = 128
SWA_WIDTH = SWA_Q_HEADS * SWA_HEAD_DIM
SWA_KV_WIDTH = SWA_KV_HEADS * SWA_HEAD_DIM

FOX_HEADS = 16
FOX_HEAD_DIM = 64
FOX_WIDTH = FOX_HEADS * FOX_HEAD_DIM

LANES = 128
HALF = LANES // 2
HEADS = 16
PADW = HEADS * LANES

ROW_TILE = 512
SUB_ROWS = 256
TQ = 512
TK = 512
ATT_HEADS = 8
SWA_QB = 4
VMEM_LIMIT = 56 * 1024 * 1024

F32 = jnp.float32
BF16 = jnp.bfloat16
NT_DIMS = (((1,), (1,)), ((), ()))


def _params(sem):
    return pltpu.CompilerParams(dimension_semantics=sem, vmem_limit_bytes=VMEM_LIMIT)


def _lane_lt_half(shape):
    return lax.broadcasted_iota(jnp.int32, shape, len(shape) - 1) < HALF


def _silu(g):
    return g / (1.0 + jnp.exp(-g))


def _write_vt(vt, vt_out):
    top = lax.broadcasted_iota(jnp.int32, (LANES, vt.shape[1]), 0) < HALF
    for p in range(HEADS // 2):
        blk = vt[p * LANES:(p + 1) * LANES, :]
        vt_out[(2 * p) * LANES:(2 * p + 1) * LANES, :] = jnp.where(top, blk, 1.0).astype(BF16)
        vt_out[(2 * p + 1) * LANES:(2 * p + 2) * LANES, :] = jnp.where(top, 1.0, blk).astype(BF16)


def _mla_body(x, win_ref, qg_ref, kg_ref, wq_ref, wk_ref, wvt_ref, aqe_ref, bqe_ref, aqo_ref, bqo_ref,
              ak_ref, bk_ref, q_out, k_out, vt_out, g_out):
    h = jnp.dot(x.astype(BF16), win_ref[...], preferred_element_type=F32)
    c1 = MLA_Q_RANK
    c2 = c1 + MLA_KV_RANK
    c3 = c2 + LANES
    cq, ckv, krg, gate = h[:, :c1], h[:, c1:c2], h[:, c2:c3], h[:, c3:]
    g_out[...] = _silu(gate).astype(BF16)

    qn = cq * lax.rsqrt(jnp.mean(cq * cq, axis=-1, keepdims=True) + RMS_EPS) * qg_ref[...]
    kn = (ckv * lax.rsqrt(jnp.mean(ckv * ckv, axis=-1, keepdims=True) + RMS_EPS) * kg_ref[...]).astype(BF16)
    q = jnp.dot(qn.astype(BF16), wq_ref[...], preferred_element_type=F32)
    kc = jnp.dot(kn, wk_ref[...], preferred_element_type=F32)
    _write_vt(lax.dot_general(wvt_ref[...], kn, NT_DIMS, preferred_element_type=F32), vt_out)

    shift = LANES - MLA_ROPE_DIM
    aq = (aqe_ref[...], aqo_ref[...])
    bq = (bqe_ref[...], bqo_ref[...])
    kr_hi = krg * ak_ref[...] + pltpu.roll(krg, shift, 1) * bk_ref[...]
    kr_lo = pltpu.roll(kr_hi, HALF, 1)
    lo = _lane_lt_half((x.shape[0], LANES))
    for p in range(HEADS // 2):
        e = slice((2 * p) * LANES, (2 * p + 1) * LANES)
        o = slice((2 * p + 1) * LANES, (2 * p + 2) * LANES)
        kblk = kc[:, p * LANES:(p + 1) * LANES]
        q_out[:, e] = (q[:, e] * aq[0] + pltpu.roll(q[:, e], shift, 1) * bq[0]).astype(BF16)
        q_out[:, o] = (q[:, o] * aq[1] + pltpu.roll(q[:, o], shift, 1) * bq[1]).astype(BF16)
        k_out[:, e] = jnp.where(lo, kblk, kr_hi).astype(BF16)
        k_out[:, o] = jnp.where(lo, kr_lo, kblk).astype(BF16)


def _split_pairs(nat, aug, out_ref):
    lo = _lane_lt_half((nat.shape[0], LANES))
    for p in range(HEADS // 2):
        blk = nat[:, p * LANES:(p + 1) * LANES]
        fill = aug[:, p * LANES:(p + 1) * LANES]
        out_ref[:, (2 * p) * LANES:(2 * p + 1) * LANES] = jnp.where(lo, blk, fill).astype(BF16)
        out_ref[:, (2 * p + 1) * LANES:(2 * p + 2) * LANES] = jnp.where(lo, fill, blk).astype(BF16)


def _fox_body(x, win_ref, wvt_ref, bf_ref, scat_ref, onesq_ref, onesk_ref, q_out, k_out, vt_out, g_out, carry_ref,
              first_rows):
    tm = x.shape[0]

    if first_rows:
        @pl.when(pl.program_id(1) == 0)
        def _():
            carry_ref[...] = jnp.zeros_like(carry_ref)

    xb = x.astype(BF16)
    h = jnp.dot(xb, win_ref[...], preferred_element_type=F32)
    W = FOX_WIDTH
    g_out[...] = _silu(h[:, 2 * W:3 * W]).astype(BF16)
    _write_vt(lax.dot_general(wvt_ref[...], xb, NT_DIMS, preferred_element_type=F32), vt_out)

    f = h[:, 3 * W:3 * W + LANES] + bf_ref[...]
    logf = jnp.minimum(f, 0.0) - jnp.log(1.0 + jnp.exp(-jnp.abs(f)))

    def split3(a):
        hi = a.astype(BF16)
        r = a - hi.astype(F32)
        mid = r.astype(BF16)
        lo = (r - mid.astype(F32)).astype(BF16)
        return hi, mid, lo

    ri = lax.broadcasted_iota(jnp.int32, (tm, tm), 0)
    ci = lax.broadcasted_iota(jnp.int32, (tm, tm), 1)
    tri = (ci <= ri).astype(BF16)
    l_hi, l_mid, l_lo = split3(logf)
    cum = (jnp.dot(tri, l_hi, preferred_element_type=F32) + jnp.dot(tri, l_mid, preferred_element_type=F32)
           + jnp.dot(tri, l_lo, preferred_element_type=F32)) + carry_ref[...]
    carry_ref[...] = cum[tm - 1:tm, :]

    c_hi, c_mid, c_lo = split3(cum * LOG2E)
    lane = lax.broadcasted_iota(jnp.int32, (tm, LANES), 1)
    packed = jnp.where(lane < HEADS, c_hi.astype(F32),
                       jnp.where(lane < 2 * HEADS, pltpu.roll(c_mid.astype(F32), HEADS, 1),
                                 pltpu.roll(c_lo.astype(F32), 2 * HEADS, 1)))
    packed = jnp.where(lane < 3 * HEADS, packed, 0.0).astype(BF16)
    aug = jnp.dot(packed, scat_ref[...], preferred_element_type=F32)
    augq = aug[:, :W] + onesq_ref[...]
    augk = aug[:, W:] + onesk_ref[...]

    _split_pairs(h[:, :W] * (FOX_HEAD_DIM ** -0.5 * LOG2E), augq, q_out)
    _split_pairs(h[:, W:2 * W], augk, k_out)


def _swa_body(x, win_ref, q_out, k_out, v_out, g_out):
    h = jnp.dot(x.astype(BF16), win_ref[...], preferred_element_type=F32)
    c1 = SWA_WIDTH
    c2 = c1 + SWA_KV_WIDTH
    c3 = c2 + SWA_KV_WIDTH
    q_out[...] = (h[:, :c1] * (SWA_HEAD_DIM ** -0.5 * LOG2E)).astype(BF16)
    g_out[...] = _silu(h[:, c3:]).astype(BF16)
    lo = _lane_lt_half((x.shape[0], LANES))
    for pair in range(SWA_KV_HEADS // 2):
        kp = h[:, c1 + pair * LANES:c1 + (pair + 1) * LANES]
        vp = h[:, c2 + pair * LANES:c2 + (pair + 1) * LANES]
        ks, vs = pltpu.roll(kp, HALF, 1), pltpu.roll(vp, HALF, 1)
        for t, (k_h, v_h) in enumerate(((jnp.where(lo, kp, ks), jnp.where(lo, vp, vs)),
                                        (jnp.where(lo, ks, kp), jnp.where(lo, vs, vp)))):
            g = 2 * pair + t
            k_out[:, g * LANES:(g + 1) * LANES] = k_h.astype(BF16)
            v_out[:, (2 * g) * LANES:(2 * g + 1) * LANES] = jnp.where(lo, v_h, 1.0).astype(BF16)
            v_out[:, (2 * g + 1) * LANES:(2 * g + 2) * LANES] = jnp.where(lo, 1.0, v_h).astype(BF16)


VT = "vt"
_PROJ = {
    "mla": (_mla_body, 12, (PADW, PADW, VT, MLA_WIDTH), 6),
    "fox": (_fox_body, 6, (PADW, PADW, VT, FOX_WIDTH), 0),
    "swa": (_swa_body, 1, (SWA_WIDTH, SWA_KV_HEADS * LANES, SWA_KV_HEADS * 2 * LANES, SWA_WIDTH), 0),
}


def _out_body(o_ref, g_ref, x_ref, w_ref, lg_ref, lb_ref):
    a = (o_ref[...].astype(F32) * g_ref[...].astype(F32)).astype(BF16)
    y = jnp.dot(a, w_ref[...], preferred_element_type=F32)
    z = DEEPNORM_ALPHA * x_ref[...] + y
    mu = jnp.mean(z, axis=-1, keepdims=True)
    zc = z - mu
    var = jnp.mean(zc * zc, axis=-1, keepdims=True)
    return zc * lax.rsqrt(var + LN_EPS) * lg_ref[...] + lb_ref[...]


def _boundary_kernel(*refs, with_out, kind):
    refs = list(refs)
    if with_out:
        out_in, refs = refs[:6], refs[6:]
    else:
        x_ref, refs = refs[0], refs[1:]
    n_ops, widths, n_tab = _PROJ[kind][1:] if kind else (0, (), 0)
    proj_in, refs = refs[:n_ops], refs[n_ops:]
    if with_out:
        y_ref, refs = refs[0], refs[1:]
    outs, scratch = refs[:len(widths)], refs[len(widths):]
    tm = (out_in[2] if with_out else x_ref).shape[0]
    blocks = [pl.ds(r * SUB_ROWS, SUB_ROWS) for r in range(tm // SUB_ROWS)]
    xs = []
    for rows in blocks:
        if with_out:
            xs.append(_out_body(*(ref.at[rows] for ref in out_in[:3]), *out_in[3:]))
            y_ref[rows, :] = xs[-1]
        else:
            xs.append(x_ref[rows, :])
    for r, (rows, x) in enumerate(zip(blocks, xs)):
        if kind:
            extra = (r == 0,) if kind == "fox" else ()
            _PROJ[kind][0](x, *proj_in[:n_ops - n_tab], *(t.at[rows] for t in proj_in[n_ops - n_tab:]),
                           *(o.at[:, rows] if w == VT else o.at[rows] for o, w in zip(outs, widths)),
                           *scratch, *extra)


class _Layer:
    def __init__(self, arr, layer):
        self.arr, self.layer = arr, layer


def _boundary(x, out_ops, kind, proj_ops):
    B, S, _ = x.shape
    tm = ROW_TILE
    row = lambda w: pl.BlockSpec((None, tm, w), lambda b, i: (b, i, 0))

    def full(a):
        if isinstance(a, _Layer):
            idx = (a.layer,) + (0,) * (a.arr.ndim - 1)
            return pl.BlockSpec((None,) + a.arr.shape[1:], lambda b, i: idx, pipeline_mode=pl.Buffered(1))
        return pl.BlockSpec(a.shape, lambda b, i: (0,) * a.ndim, pipeline_mode=pl.Buffered(1))

    unwrap = lambda a: a.arr if isinstance(a, _Layer) else a
    tab = pl.BlockSpec((tm, LANES), lambda b, i: (i, 0))
    in_specs, args, out_specs, out_shape, scratch = [], [], [], [], []
    if out_ops is not None:
        o, sg, w, lg, lb = out_ops
        in_specs += [row(D_MODEL), row(D_MODEL), row(D_MODEL), full(w), full(lg), full(lb)]
        args += [o, sg, x, unwrap(w), unwrap(lg), unwrap(lb)]
        out_specs.append(row(D_MODEL))
        out_shape.append(jax.ShapeDtypeStruct((B, S, D_MODEL), F32))
    else:
        in_specs.append(row(D_MODEL))
        args.append(x)
    if kind:
        _, n_ops, widths, n_tab = _PROJ[kind]
        assert len(proj_ops) == n_ops
        in_specs += [full(a) for a in proj_ops[:n_ops - n_tab]] + [tab] * n_tab
        args += [unwrap(a) for a in proj_ops]
        for w in widths:
            if w == VT:
                assert TK % tm == 0
                out_specs.append(pl.BlockSpec((None, None, PADW, tm),
                                              lambda b, i: (b, i * tm // TK, 0, i % (TK // tm))))
                out_shape.append(jax.ShapeDtypeStruct((B, S // TK, PADW, TK), BF16))
            else:
                out_specs.append(row(w))
                out_shape.append(jax.ShapeDtypeStruct((B, S, w), BF16))
        if kind == "fox":
            scratch.append(pltpu.VMEM((1, LANES), F32))
    sem = ("parallel", "arbitrary" if kind == "fox" else "parallel")
    return pl.pallas_call(
        functools.partial(_boundary_kernel, with_out=out_ops is not None, kind=kind),
        grid=(B, S // tm),
        in_specs=in_specs,
        out_specs=out_specs,
        out_shape=out_shape,
        scratch_shapes=scratch,
        compiler_params=_params(sem),
        name="boundary_" + ("out_" if out_ops is not None else "") + (kind or "final"),
    )(*args)


def _normalize_pair(ext_even, ext_odd, extra_even=None, extra_odd=None):
    lo = _lane_lt_half(ext_even.shape)
    den = jnp.where(lo, ext_odd, ext_even)
    if extra_even is not None:
        den = den + jnp.where(lo, extra_odd, extra_even)
    return jnp.where(lo, ext_even, ext_odd) * pltpu.roll(1.0 / den, HALF, 1)


def _causal_attn_kernel(q_ref, k_ref, vt_ref, o_ref, m_ref, acc_ref):
    S = q_ref.shape[0]
    heads = range(ATT_HEADS)
    lanes = [slice(h * LANES, (h + 1) * LANES) for h in heads]

    def tile(q0, kt, jobs, diagonal):
        chains = [(h,) + job for job in jobs for h in heads]
        k0 = pl.multiple_of(kt * TK, TK)
        s_all = [lax.dot_general(q_ref[pl.ds(q0 + roff, nq), lanes[h]], k_ref[pl.ds(k0, nk), lanes[h]],
                                 NT_DIMS, preferred_element_type=F32) for h, roff, nq, nk in chains]
        p_all, alpha_all = [], []
        for (h, roff, nq, nk), s in zip(chains, s_all):
            st = slice(roff, roff + nq)
            blocks = [s[:, j * LANES:(j + 1) * LANES] for j in range(nk // LANES)]
            if diagonal:
                row = lax.broadcasted_iota(jnp.int32, (nq, LANES), 0) + roff
                col = lax.broadcasted_iota(jnp.int32, (nq, LANES), 1)
                blocks = [b if (j + 1) * LANES - 1 <= roff else jnp.where(col + j * LANES <= row, b, NEG_INF)
                          for j, b in enumerate(blocks)]
            m_blk = jnp.max(functools.reduce(jnp.maximum, blocks), axis=-1, keepdims=True)
            if diagonal:
                m_new = jnp.broadcast_to(m_blk, (nq, LANES))
                alpha_all.append(None)
            else:
                m_old = m_ref[h, st, :]
                m_new = jnp.maximum(m_old, m_blk)
                alpha_all.append(jnp.exp2(m_old - m_new))
            p_all.append(jnp.concatenate([jnp.exp2(b - m_new) for b in blocks], axis=1).astype(BF16))
            m_ref[h, st, :] = m_new
        for (h, roff, nq, nk), p, alpha in zip(chains, p_all, alpha_all):
            st = slice(roff, roff + nq)
            pv = lax.dot_general(p, vt_ref[kt, lanes[h], 0:nk], NT_DIMS, preferred_element_type=F32)
            acc_ref[h, st, :] = pv if alpha is None else acc_ref[h, st, :] * alpha + pv

    def q_body(qi, _):
        q0 = pl.multiple_of(qi * TQ, TQ)
        half = TQ // 2
        tile(q0, qi, [(0, half, half), (half, half, TQ)], True)

        def kv_body(kj, _):
            tile(q0, kj, [(0, TQ, TK)], False)
            return 0

        lax.fori_loop(0, qi, kv_body, 0)
        for p in range(ATT_HEADS // 2):
            o_ref[pl.ds(q0, TQ), p * LANES:(p + 1) * LANES] = _normalize_pair(
                acc_ref[2 * p], acc_ref[2 * p + 1]).astype(BF16)
        return 0

    lax.fori_loop(0, S // TQ, q_body, 0)


def _causal_attn(q, k, vt):
    B, S, _ = q.shape
    spec = pl.BlockSpec((None, S, ATT_HEADS * LANES), lambda b, p: (b, 0, p))
    return pl.pallas_call(
        _causal_attn_kernel,
        grid=(B, HEADS // ATT_HEADS),
        in_specs=[spec, spec, pl.BlockSpec((None, S // TK, ATT_HEADS * LANES, TK), lambda b, p: (b, 0, p, 0))],
        out_specs=pl.BlockSpec((None, S, ATT_HEADS * HALF), lambda b, p: (b, 0, p)),
        out_shape=jax.ShapeDtypeStruct((B, S, HEADS * HALF), BF16),
        scratch_shapes=[pltpu.VMEM((ATT_HEADS, TQ, LANES), F32), pltpu.VMEM((ATT_HEADS, TQ, LANES), F32)],
        compiler_params=_params(("parallel", "parallel")),
        name="causal_attn",
    )(q, k, vt)


def _swa_attn_kernel(sink_ref, q_ref, kp_ref, kc_ref, vp_ref, vc_ref, bias0_ref, bias_ref, o_ref):
    nqb = q_ref.shape[0] // BLOCK
    lo = _lane_lt_half((BLOCK, LANES))
    zero = jnp.zeros((), BF16)

    def band(prev_ref, cur_ref, i, group):
        cols = slice(group * LANES, (group + 1) * LANES)
        if i == 0:
            return jnp.concatenate([prev_ref[:, cols], cur_ref[0:BLOCK, cols]], axis=0)
        return cur_ref[(i - 1) * BLOCK:(i + 1) * BLOCK, cols]

    def qk(i, g):
        rows = slice(i * BLOCK, (i + 1) * BLOCK)
        kb = band(kp_ref, kc_ref, i, g)
        qa = q_ref[rows, (2 * g) * LANES:(2 * g + 1) * LANES]
        qb = q_ref[rows, (2 * g + 1) * LANES:(2 * g + 2) * LANES]
        out = []
        for par in range(2):
            keep = lo if par == 0 else jnp.logical_not(lo)
            stack = jnp.concatenate([jnp.where(keep, qa, zero), jnp.where(keep, qb, zero)], axis=0)
            out.append(lax.dot_general(stack, kb, NT_DIMS, preferred_element_type=F32))
        return out

    def finish(i, g, s_pair):
        b_ref = bias0_ref if i == 0 else bias_ref
        ext, extra = [], []
        for par, s in enumerate(s_pair):
            ha, hb = 4 * g + par, 4 * g + 2 + par
            blocks = [s[:, j * LANES:(j + 1) * LANES]
                      + jnp.concatenate([b_ref[ha, :, j * LANES:(j + 1) * LANES],
                                         b_ref[hb, :, j * LANES:(j + 1) * LANES]], axis=0) for j in range(2)]
            sink = jnp.concatenate([jnp.full((BLOCK, LANES), sink_ref[ha], F32),
                                    jnp.full((BLOCK, LANES), sink_ref[hb], F32)], axis=0)
            m = jnp.maximum(sink, jnp.max(jnp.maximum(blocks[0], blocks[1]), axis=-1, keepdims=True))
            p = jnp.concatenate([jnp.exp2(b - m) for b in blocks], axis=1).astype(BF16)
            vb = band(vp_ref, vc_ref, i, 2 * g + par)
            ext.append(jnp.dot(p, vb, preferred_element_type=F32))
            extra.append(jnp.exp2(sink - m))
        for j in range(2):
            rows = slice(j * BLOCK, (j + 1) * BLOCK)
            o_ref[i * BLOCK:(i + 1) * BLOCK, (2 * g + j) * LANES:(2 * g + j + 1) * LANES] = _normalize_pair(
                ext[0][rows], ext[1][rows], extra[0][rows], extra[1][rows]).astype(BF16)

    units = [(i, g) for i in range(nqb) for g in range(SWA_KV_HEADS)]
    s_cur = qk(*units[0])
    for u, (i, g) in enumerate(units):
        s_next = qk(*units[u + 1]) if u + 1 < len(units) else None
        finish(i, g, s_cur)
        s_cur = s_next


def _swa_attn(sinks2, q, k, v, bias):
    B, S, _ = q.shape
    rows = SWA_QB * BLOCK
    cur = lambda w: pl.BlockSpec((None, rows, w), lambda b, n: (b, n, 0))
    prev = lambda w: pl.BlockSpec((None, BLOCK, w), lambda b, n: (b, jnp.maximum(n * SWA_QB - 1, 0), 0))
    bias_shape = (None, SWA_Q_HEADS, BLOCK, 2 * BLOCK)
    kw, vw = k.shape[-1], v.shape[-1]
    return pl.pallas_call(
        _swa_attn_kernel,
        grid=(B, S // rows),
        in_specs=[pl.BlockSpec(memory_space=pltpu.SMEM), cur(SWA_WIDTH), prev(kw), cur(kw), prev(vw), cur(vw),
                  pl.BlockSpec(bias_shape, lambda b, n: (jnp.minimum(n, 1), 0, 0, 0)),
                  pl.BlockSpec(bias_shape, lambda b, n: (1, 0, 0, 0))],
        out_specs=cur(SWA_WIDTH),
        out_shape=jax.ShapeDtypeStruct((B, S, SWA_WIDTH), BF16),
        compiler_params=_params(("parallel", "parallel")),
        name="swa_attn",
    )(sinks2, q, k, k, v, v, bias, bias)


def _rot(w):
    half = w.shape[-1] // 2
    return jnp.concatenate([-w[..., half:], w[..., :half]], axis=-1)


def _rope_tables(seq, c0):
    inv = ROPE_THETA ** (-np.arange(0, MLA_ROPE_DIM, 2, dtype=np.float64) / MLA_ROPE_DIM)
    ang = np.arange(seq, dtype=np.float64)[:, None] * inv[None, :]
    cos = np.concatenate([np.cos(ang)] * 2, axis=-1)
    sin = np.concatenate([np.sin(ang)] * 2, axis=-1)
    z32 = np.zeros((seq, LANES - HALF - MLA_ROPE_DIM))
    z64 = np.zeros((seq, HALF))
    c64 = np.full((seq, HALF), c0)
    aq_even = np.concatenate([c64, c0 * cos, z32], axis=-1)
    bq_even = np.concatenate([z64, c0 * sin, z32], axis=-1)
    aq_odd = np.concatenate([c0 * cos, z32, c64], axis=-1)
    bq_odd = np.concatenate([c0 * sin, z32, z64], axis=-1)
    ak = np.concatenate([z64, cos, z32], axis=-1)
    bk = np.concatenate([z64, sin, z32], axis=-1)
    return tuple(t.astype(np.float32) for t in (aq_even, bq_even, aq_odd, bq_odd, ak, bk))


def _mla_operands(w_in, q_norm, w_q_up, kv_norm, w_kv_up):
    L = w_in.shape[0]
    c2 = MLA_Q_RANK + MLA_KV_RANK
    c3 = c2 + MLA_ROPE_DIM
    kr = w_in[..., c2:c3]
    krg = jnp.concatenate([jnp.zeros((L, D_MODEL, HALF), F32), kr, _rot(kr)], axis=-1)
    win = jnp.concatenate([w_in[..., :c2], krg, w_in[..., c3:]], axis=-1).astype(BF16)
    odd = (jnp.arange(MLA_HEADS) % 2 == 1)[:, None]
    wq = w_q_up.reshape(L, MLA_Q_RANK, MLA_HEADS, MLA_NOPE_DIM + MLA_ROPE_DIM)
    nope, rope = wq[..., :MLA_NOPE_DIM], wq[..., MLA_NOPE_DIM:]
    wq = jnp.where(odd, jnp.concatenate([rope, _rot(rope), nope], axis=-1),
                   jnp.concatenate([nope, rope, _rot(rope)], axis=-1))
    wq = wq.reshape(L, MLA_Q_RANK, PADW).astype(BF16)
    wkv = w_kv_up.reshape(L, MLA_KV_RANK, MLA_HEADS, MLA_NOPE_DIM + MLA_V_DIM)
    wk = wkv[..., :MLA_NOPE_DIM].reshape(L, MLA_KV_RANK, MLA_HEADS * MLA_NOPE_DIM).astype(BF16)
    wvt = jnp.swapaxes(wkv[..., MLA_NOPE_DIM:].reshape(L, MLA_KV_RANK, MLA_WIDTH), 1, 2).astype(BF16)
    return win, q_norm.reshape(L, 1, -1), kv_norm.reshape(L, 1, -1), wq, wk, wvt


def _fox_layer_operands(w_in, b_f):
    W = FOX_WIDTH
    wf = jnp.concatenate([w_in[:, 3 * W:3 * W + FOX_HEADS], jnp.zeros((D_MODEL, LANES - FOX_HEADS), F32)], axis=-1)
    win = jnp.concatenate([w_in[:, :2 * W], w_in[:, 3 * W + FOX_HEADS:], wf], axis=-1).astype(BF16)
    wvt = w_in[:, 2 * W:3 * W].T.astype(BF16)
    bf = jnp.concatenate([b_f, jnp.zeros((LANES - FOX_HEADS,), F32)]).reshape(1, LANES)
    h = np.arange(FOX_HEADS)
    base = (h // 2) * LANES + np.where(h % 2 == 0, HALF, 0)
    scat = np.zeros((LANES, 2 * W), np.float32)
    onesq = np.zeros((1, W), np.float32)
    onesk = np.zeros((1, W), np.float32)
    for j in range(3):
        scat[j * FOX_HEADS + h, base + j] = 1.0
        scat[j * FOX_HEADS + h, W + base + 3 + j] = -1.0
        onesq[0, base + 3 + j] = 1.0
        onesk[0, base + j] = 1.0
    return win, wvt, bf, jnp.asarray(scat, dtype=BF16), onesq, onesk


def _swa_bias():
    qi = np.arange(BLOCK)[:, None]
    kj = np.arange(2 * BLOCK)[None, :]
    dist = qi + BLOCK - kj
    valid = (dist >= 0) & (dist < WINDOW)
    slopes = 2.0 ** (-8.0 * np.arange(1, SWA_Q_HEADS + 1, dtype=np.float64) / SWA_Q_HEADS)
    ali = -(slopes[:, None, None] * dist.astype(np.float64)[None]) * LOG2E
    bias_rest = np.where(valid[None], ali, NEG_INF)
    bias_first = np.where((valid & (kj >= BLOCK))[None], ali, NEG_INF)
    return np.stack([bias_first, bias_rest]).astype(np.float32)


def kernel(x, ln_g, ln_b, mla_w_in, mla_q_norm, mla_w_q_up, mla_kv_norm, mla_w_kv_up, mla_w_out,
           swa_w_in, swa_sinks, swa_w_out, fox_w_in, fox_b_f, fox_w_out):
    seq = x.shape[1]
    mla_ops = _mla_operands(mla_w_in, mla_q_norm, mla_w_q_up, mla_kv_norm, mla_w_kv_up)
    mla_tables = _rope_tables(seq, (MLA_NOPE_DIM + MLA_ROPE_DIM) ** -0.5 * LOG2E)
    w_outs = {"mla": mla_w_out.astype(BF16), "swa": swa_w_out.astype(BF16), "fox": fox_w_out.astype(BF16)}
    ln_g3, ln_b3 = ln_g.reshape(DEPTH, 1, D_MODEL), ln_b.reshape(DEPTH, 1, D_MODEL)
    out_ops = None
    for i in range(DEPTH + 1):
        j = i // N_MIXERS
        kind = ("mla", "swa", "fox")[i % N_MIXERS] if i < DEPTH else None
        if kind == "mla":
            proj_ops = tuple(_Layer(a, j) for a in mla_ops) + mla_tables
        elif kind == "swa":
            proj_ops = (swa_w_in[j].astype(BF16),)
        elif kind == "fox":
            proj_ops = _fox_layer_operands(fox_w_in[j], fox_b_f[j])
        else:
            proj_ops = ()
        res = _boundary(x, out_ops, kind, proj_ops)
        if out_ops is not None:
            x, res = res[0], res[1:]
        if kind is None:
            return x
        q, k, v, sg = res
        if kind == "swa":
            o = _swa_attn(swa_sinks[j].astype(F32) * LOG2E, q, k, v, _swa_bias())
        else:
            o = _causal_attn(q, k, v)
        out_ops = (o, sg, _Layer(w_outs[kind], j), _Layer(ln_g3, i), _Layer(ln_b3, i))
```

```python
import functools
import math

import jax
import jax.numpy as jnp
import numpy as np
from jax import lax
from jax.experimental import pallas as pl
from jax.experimental.pallas import tpu as pltpu

D_MODEL = 1024
DEPTH = 4
N_MIXERS = 3
BLOCK = 128
NEG_INF = -1e30
RMS_EPS = 1e-6
LN_EPS = 1e-5
DEEPNORM_ALPHA = (2 * DEPTH) ** 0.25
LOG2E = math.log2(math.e)

MLA_HEADS = 16
MLA_NOPE_DIM = 64
MLA_ROPE_DIM = 32
MLA_V_DIM = 64
MLA_Q_RANK = 384
MLA_KV_RANK = 256
ROPE_THETA = 10000.0
MLA_WIDTH = MLA_HEADS * MLA_V_DIM

SWA_Q_HEADS = 16
SWA_KV_HEADS = 4
SWA_HEAD_DIM = 64
WINDOW = 128
SWA_WIDTH = SWA_Q_HEADS * SWA_HEAD_DIM
SWA_KV_WIDTH = SWA_KV_HEADS * SWA_HEAD_DIM

FOX_HEADS = 16
FOX_HEAD_DIM = 64
FOX_WIDTH = FOX_HEADS * FOX_HEAD_DIM

LANES = 128
HALF = LANES // 2
HEADS = 16
PADW = HEADS * LANES

ROW_TILE = 512
SUB_ROWS = 256
TQ = 512
TK = 512
ATT_HEADS = 4
SWA_QB = 4
VMEM_LIMIT = 56 * 1024 * 1024

F32 = jnp.float32
BF16 = jnp.bfloat16
NT_DIMS = (((1,), (1,)), ((), ()))


def _params(sem):
    return pltpu.CompilerParams(dimension_semantics=sem, vmem_limit_bytes=VMEM_LIMIT)


def _lane_lt_half(shape):
    return lax.broadcasted_iota(jnp.int32, shape, len(shape) - 1) < HALF


def _silu(g):
    return g / (1.0 + jnp.exp(-g))


def _write_vt(vt, vt_out):
    top = lax.broadcasted_iota(jnp.int32, (LANES, vt.shape[1]), 0) < HALF
    for p in range(HEADS // 2):
        blk = vt[p * LANES:(p + 1) * LANES, :]
        vt_out[(2 * p) * LANES:(2 * p + 1) * LANES, :] = jnp.where(top, blk, 1.0).astype(BF16)
        vt_out[(2 * p + 1) * LANES:(2 * p + 2) * LANES, :] = jnp.where(top, 1.0, blk).astype(BF16)


def _mla_body(x, win_ref, qg_ref, kg_ref, wq_ref, wk_ref, wvt_ref, aqe_ref, bqe_ref, aqo_ref, bqo_ref,
              ak_ref, bk_ref, q_out, k_out, vt_out, g_out):
    h = jnp.dot(x.astype(BF16), win_ref[...], preferred_element_type=F32)
    c1 = MLA_Q_RANK
    c2 = c1 + MLA_KV_RANK
    c3 = c2 + LANES
    cq, ckv, krg, gate = h[:, :c1], h[:, c1:c2], h[:, c2:c3], h[:, c3:]
    g_out[...] = _silu(gate).astype(BF16)

    qn = cq * lax.rsqrt(jnp.mean(cq * cq, axis=-1, keepdims=True) + RMS_EPS) * qg_ref[...]
    kn = (ckv * lax.rsqrt(jnp.mean(ckv * ckv, axis=-1, keepdims=True) + RMS_EPS) * kg_ref[...]).astype(BF16)
    q = jnp.dot(qn.astype(BF16), wq_ref[...], preferred_element_type=F32)
    kc = jnp.dot(kn, wk_ref[...], preferred_element_type=F32)
    _write_vt(lax.dot_general(wvt_ref[...], kn, NT_DIMS, preferred_element_type=F32), vt_out)

    shift = LANES - MLA_ROPE_DIM
    aq = (aqe_ref[...], aqo_ref[...])
    bq = (bqe_ref[...], bqo_ref[...])
    kr_hi = krg * ak_ref[...] + pltpu.roll(krg, shift, 1) * bk_ref[...]
    kr_lo = pltpu.roll(kr_hi, HALF, 1)
    lo = _lane_lt_half((x.shape[0], LANES))
    for p in range(HEADS // 2):
        e = slice((2 * p) * LANES, (2 * p + 1) * LANES)
        o = slice((2 * p + 1) * LANES, (2 * p + 2) * LANES)
        kblk = kc[:, p * LANES:(p + 1) * LANES]
        q_out[:, e] = (q[:, e] * aq[0] + pltpu.roll(q[:, e], shift, 1) * bq[0]).astype(BF16)
        q_out[:, o] = (q[:, o] * aq[1] + pltpu.roll(q[:, o], shift, 1) * bq[1]).astype(BF16)
        k_out[:, e] = jnp.where(lo, kblk, kr_hi).astype(BF16)
        k_out[:, o] = jnp.where(lo, kr_lo, kblk).astype(BF16)


def _split_pairs(nat, aug, out_ref):
    lo = _lane_lt_half((nat.shape[0], LANES))
    for p in range(HEADS // 2):
        blk = nat[:, p * LANES:(p + 1) * LANES]
        fill = aug[:, p * LANES:(p + 1) * LANES]
        out_ref[:, (2 * p) * LANES:(2 * p + 1) * LANES] = jnp.where(lo, blk, fill).astype(BF16)
        out_ref[:, (2 * p + 1) * LANES:(2 * p + 2) * LANES] = jnp.where(lo, fill, blk).astype(BF16)


def _fox_body(x, win_ref, wvt_ref, bf_ref, scat_ref, onesq_ref, onesk_ref, q_out, k_out, vt_out, g_out, carry_ref,
              first_rows):
    tm = x.shape[0]

    if first_rows:
        @pl.when(pl.program_id(1) == 0)
        def _():
            carry_ref[...] = jnp.zeros_like(carry_ref)

    xb = x.astype(BF16)
    h = jnp.dot(xb, win_ref[...], preferred_element_type=F32)
    W = FOX_WIDTH
    g_out[...] = _silu(h[:, 2 * W:3 * W]).astype(BF16)
    _write_vt(lax.dot_general(wvt_ref[...], xb, NT_DIMS, preferred_element_type=F32), vt_out)

    f = h[:, 3 * W:3 * W + LANES] + bf_ref[...]
    logf = jnp.minimum(f, 0.0) - jnp.log(1.0 + jnp.exp(-jnp.abs(f)))

    def split3(a):
        hi = a.astype(BF16)
        r = a - hi.astype(F32)
        mid = r.astype(BF16)
        lo = (r - mid.astype(F32)).astype(BF16)
        return hi, mid, lo

    ri = lax.broadcasted_iota(jnp.int32, (tm, tm), 0)
    ci = lax.broadcasted_iota(jnp.int32, (tm, tm), 1)
    tri = (ci <= ri).astype(BF16)
    l_hi, l_mid, l_lo = split3(logf)
    cum = (jnp.dot(tri, l_hi, preferred_element_type=F32) + jnp.dot(tri, l_mid, preferred_element_type=F32)
           + jnp.dot(tri, l_lo, preferred_element_type=F32)) + carry_ref[...]
    carry_ref[...] = cum[tm - 1:tm, :]

    c_hi, c_mid, c_lo = split3(cum * LOG2E)
    lane = lax.broadcasted_iota(jnp.int32, (tm, LANES), 1)
    packed = jnp.where(lane < HEADS, c_hi.astype(F32),
                       jnp.where(lane < 2 * HEADS, pltpu.roll(c_mid.astype(F32), HEADS, 1),
                                 pltpu.roll(c_lo.astype(F32), 2 * HEADS, 1)))
    packed = jnp.where(lane < 3 * HEADS, packed, 0.0).astype(BF16)
    aug = jnp.dot(packed, scat_ref[...], preferred_element_type=F32)
    augq = aug[:, :W] + onesq_ref[...]
    augk = aug[:, W:] + onesk_ref[...]

    _split_pairs(h[:, :W] * (FOX_HEAD_DIM ** -0.5 * LOG2E), augq, q_out)
    _split_pairs(h[:, W:2 * W], augk, k_out)


def _swa_body(x, win_ref, q_out, k_out, v_out, g_out):
    h = jnp.dot(x.astype(BF16), win_ref[...], preferred_element_type=F32)
    c1 = SWA_WIDTH
    c2 = c1 + SWA_KV_WIDTH
    c3 = c2 + SWA_KV_WIDTH
    q_out[...] = (h[:, :c1] * (SWA_HEAD_DIM ** -0.5 * LOG2E)).astype(BF16)
    g_out[...] = _silu(h[:, c3:]).astype(BF16)
    lo = _lane_lt_half((x.shape[0], LANES))
    for pair in range(SWA_KV_HEADS // 2):
        kp = h[:, c1 + pair * LANES:c1 + (pair + 1) * LANES]
        vp = h[:, c2 + pair * LANES:c2 + (pair + 1) * LANES]
        ks, vs = pltpu.roll(kp, HALF, 1), pltpu.roll(vp, HALF, 1)
        for t, (k_h, v_h) in enumerate(((jnp.where(lo, kp, ks), jnp.where(lo, vp, vs)),
                                        (jnp.where(lo, ks, kp), jnp.where(lo, vs, vp)))):
            g = 2 * pair + t
            k_out[:, g * LANES:(g + 1) * LANES] = k_h.astype(BF16)
            v_out[:, (2 * g) * LANES:(2 * g + 1) * LANES] = jnp.where(lo, v_h, 1.0).astype(BF16)
            v_out[:, (2 * g + 1) * LANES:(2 * g + 2) * LANES] = jnp.where(lo, 1.0, v_h).astype(BF16)


VT = "vt"
_PROJ = {
    "mla": (_mla_body, 12, (PADW, PADW, VT, MLA_WIDTH), 6),
    "fox": (_fox_body, 6, (PADW, PADW, VT, FOX_WIDTH), 0),
    "swa": (_swa_body, 1, (SWA_WIDTH, SWA_KV_HEADS * LANES, SWA_KV_HEADS * 2 * LANES, SWA_WIDTH), 0),
}


def _out_body(o_ref, g_ref, x_ref, w_ref, lg_ref, lb_ref):
    a = (o_ref[...].astype(F32) * g_ref[...].astype(F32)).astype(BF16)
    y = jnp.dot(a, w_ref[...], preferred_element_type=F32)
    z = DEEPNORM_ALPHA * x_ref[...] + y
    mu = jnp.mean(z, axis=-1, keepdims=True)
    zc = z - mu
    var = jnp.mean(zc * zc, axis=-1, keepdims=True)
    return zc * lax.rsqrt(var + LN_EPS) * lg_ref[...] + lb_ref[...]


def _boundary_kernel(*refs, with_out, kind):
    refs = list(refs)
    if with_out:
        out_in, refs = refs[:6], refs[6:]
    else:
        x_ref, refs = refs[0], refs[1:]
    n_ops, widths, n_tab = _PROJ[kind][1:] if kind else (0, (), 0)
    proj_in, refs = refs[:n_ops], refs[n_ops:]
    if with_out:
        y_ref, refs = refs[0], refs[1:]
    outs, scratch = refs[:len(widths)], refs[len(widths):]
    tm = (out_in[2] if with_out else x_ref).shape[0]
    blocks = [pl.ds(r * SUB_ROWS, SUB_ROWS) for r in range(tm // SUB_ROWS)]
    xs = []
    for rows in blocks:
        if with_out:
            xs.append(_out_body(*(ref.at[rows] for ref in out_in[:3]), *out_in[3:]))
            y_ref[rows, :] = xs[-1]
        else:
            xs.append(x_ref[rows, :])
    for r, (rows, x) in enumerate(zip(blocks, xs)):
        if kind:
            extra = (r == 0,) if kind == "fox" else ()
            _PROJ[kind][0](x, *proj_in[:n_ops - n_tab], *(t.at[rows] for t in proj_in[n_ops - n_tab:]),
                           *(o.at[:, rows] if w == VT else o.at[rows] for o, w in zip(outs, widths)),
                           *scratch, *extra)


class _Layer:
    def __init__(self, arr, layer):
        self.arr, self.layer = arr, layer


def _boundary(x, out_ops, kind, proj_ops):
    B, S, _ = x.shape
    tm = ROW_TILE
    row = lambda w: pl.BlockSpec((None, tm, w), lambda b, i: (b, i, 0))

    def full(a):
        if isinstance(a, _Layer):
            idx = (a.layer,) + (0,) * (a.arr.ndim - 1)
            return pl.BlockSpec((None,) + a.arr.shape[1:], lambda b, i: idx, pipeline_mode=pl.Buffered(1))
        return pl.BlockSpec(a.shape, lambda b, i: (0,) * a.ndim, pipeline_mode=pl.Buffered(1))

    unwrap = lambda a: a.arr if isinstance(a, _Layer) else a
    tab = pl.BlockSpec((tm, LANES), lambda b, i: (i, 0))
    in_specs, args, out_specs, out_shape, scratch = [], [], [], [], []
    if out_ops is not None:
        o, sg, w, lg, lb = out_ops
        in_specs += [row(D_MODEL), row(D_MODEL), row(D_MODEL), full(w), full(lg), full(lb)]
        args += [o, sg, x, unwrap(w), unwrap(lg), unwrap(lb)]
        out_specs.append(row(D_MODEL))
        out_shape.append(jax.ShapeDtypeStruct((B, S, D_MODEL), F32))
    else:
        in_specs.append(row(D_MODEL))
        args.append(x)
    if kind:
        _, n_ops, widths, n_tab = _PROJ[kind]
        assert len(proj_ops) == n_ops
        in_specs += [full(a) for a in proj_ops[:n_ops - n_tab]] + [tab] * n_tab
        args += [unwrap(a) for a in proj_ops]
        for w in widths:
            if w == VT:
                assert TK % tm == 0
                out_specs.append(pl.BlockSpec((None, None, PADW, tm),
                                              lambda b, i: (b, i * tm // TK, 0, i % (TK // tm))))
                out_shape.append(jax.ShapeDtypeStruct((B, S // TK, PADW, TK), BF16))
            else:
                out_specs.append(row(w))
                out_shape.append(jax.ShapeDtypeStruct((B, S, w), BF16))
        if kind == "fox":
            scratch.append(pltpu.VMEM((1, LANES), F32))
    sem = ("parallel", "arbitrary" if kind == "fox" else "parallel")
    return pl.pallas_call(
        functools.partial(_boundary_kernel, with_out=out_ops is not None, kind=kind),
        grid=(B, S // tm),
        in_specs=in_specs,
        out_specs=out_specs,
        out_shape=out_shape,
        scratch_shapes=scratch,
        compiler_params=_params(sem),
        name="boundary_" + ("out_" if out_ops is not None else "") + (kind or "final"),
    )(*args)


def _normalize_pair(ext_even, ext_odd, extra_even=None, extra_odd=None):
    lo = _lane_lt_half(ext_even.shape)
    den = jnp.where(lo, ext_odd, ext_even)
    if extra_even is not None:
        den = den + jnp.where(lo, extra_odd, extra_even)
    return jnp.where(lo, ext_even, ext_odd) * pltpu.roll(1.0 / den, HALF, 1)


def _causal_attn_kernel(q_ref, k_ref, vt_ref, o_ref, m_ref, acc_ref):
    S = q_ref.shape[0]
    heads = range(ATT_HEADS)
    lanes = [slice(h * LANES, (h + 1) * LANES) for h in heads]

    def tile(q0, kt, jobs, diagonal):
        chains = [(h,) + job for job in jobs for h in heads]
        k0 = kt * TK
        s_all = [lax.dot_general(q_ref[pl.ds(q0 + roff, nq), lanes[h]], k_ref[pl.ds(k0, nk), lanes[h]],
                                 NT_DIMS, preferred_element_type=F32) for h, roff, nq, nk in chains]
        p_all, alpha_all = [], []
        for (h, roff, nq, nk), s in zip(chains, s_all):
            st = slice(roff, roff + nq)
            blocks = [s[:, j * LANES:(j + 1) * LANES] for j in range(nk // LANES)]
            if diagonal:
                row = lax.broadcasted_iota(jnp.int32, (nq, LANES), 0) + roff
                col = lax.broadcasted_iota(jnp.int32, (nq, LANES), 1)
                blocks = [b if (j + 1) * LANES - 1 <= roff else jnp.where(col + j * LANES <= row, b, NEG_INF)
                          for j, b in enumerate(blocks)]
            m_blk = jnp.max(functools.reduce(jnp.maximum, blocks), axis=-1, keepdims=True)
            if diagonal:
                m_new = jnp.broadcast_to(m_blk, (nq, LANES))
                alpha_all.append(None)
            else:
                m_old = m_ref[h, st, :]
                m_new = jnp.maximum(m_old, m_blk)
                alpha_all.append(jnp.exp2(m_old - m_new))
            p_all.append(jnp.concatenate([jnp.exp2(b - m_new) for b in blocks], axis=1).astype(BF16))
            m_ref[h, st, :] = m_new
        for (h, roff, nq, nk), p, alpha in zip(chains, p_all, alpha_all):
            st = slice(roff, roff + nq)
            pv = lax.dot_general(p, vt_ref[kt, lanes[h], 0:nk], NT_DIMS, preferred_element_type=F32)
            acc_ref[h, st, :] = pv if alpha is None else acc_ref[h, st, :] * alpha + pv

    half = TQ // 2
    for qi in range(S // TQ):
        q0 = qi * TQ
        tile(q0, qi, [(0, half, half), (half, half, TQ)], True)
        for kj in range(qi):
            tile(q0, kj, [(0, TQ, TK)], False)
        for p in range(ATT_HEADS // 2):
            o_ref[q0:q0 + TQ, p * LANES:(p + 1) * LANES] = _normalize_pair(
                acc_ref[2 * p], acc_ref[2 * p + 1]).astype(BF16)


def _causal_attn(q, k, vt):
    B, S, _ = q.shape
    spec = pl.BlockSpec((None, S, ATT_HEADS * LANES), lambda b, p: (b, 0, p))
    return pl.pallas_call(
        _causal_attn_kernel,
        grid=(B, HEADS // ATT_HEADS),
        in_specs=[spec, spec, pl.BlockSpec((None, S // TK, ATT_HEADS * LANES, TK), lambda b, p: (b, 0, p, 0))],
        out_specs=pl.BlockSpec((None, S, ATT_HEADS * HALF), lambda b, p: (b, 0, p)),
        out_shape=jax.ShapeDtypeStruct((B, S, HEADS * HALF), BF16),
        scratch_shapes=[pltpu.VMEM((ATT_HEADS, TQ, LANES), F32), pltpu.VMEM((ATT_HEADS, TQ, LANES), F32)],
        compiler_params=_params(("parallel", "parallel")),
        name="causal_attn",
    )(q, k, vt)


def _swa_attn_kernel(sink_ref, q_ref, kp_ref, kc_ref, vp_ref, vc_ref, bias0_ref, bias_ref, o_ref):
    nqb = q_ref.shape[0] // BLOCK
    lo = _lane_lt_half((BLOCK, LANES))
    zero = jnp.zeros((), BF16)

    def band(prev_ref, cur_ref, i, group):
        cols = slice(group * LANES, (group + 1) * LANES)
        if i == 0:
            return jnp.concatenate([prev_ref[:, cols], cur_ref[0:BLOCK, cols]], axis=0)
        return cur_ref[(i - 1) * BLOCK:(i + 1) * BLOCK, cols]

    def qk(i, g):
        rows = slice(i * BLOCK, (i + 1) * BLOCK)
        kb = band(kp_ref, kc_ref, i, g)
        qa = q_ref[rows, (2 * g) * LANES:(2 * g + 1) * LANES]
        qb = q_ref[rows, (2 * g + 1) * LANES:(2 * g + 2) * LANES]
        out = []
        for par in range(2):
            keep = lo if par == 0 else jnp.logical_not(lo)
            stack = jnp.concatenate([jnp.where(keep, qa, zero), jnp.where(keep, qb, zero)], axis=0)
            out.append(lax.dot_general(stack, kb, NT_DIMS, preferred_element_type=F32))
        return out

    def finish(i, g, s_pair):
        b_ref = bias0_ref if i == 0 else bias_ref
        ext, extra = [], []
        for par, s in enumerate(s_pair):
            ha, hb = 4 * g + par, 4 * g + 2 + par
            blocks = [s[:, j * LANES:(j + 1) * LANES]
                      + jnp.concatenate([b_ref[ha, :, j * LANES:(j + 1) * LANES],
                                         b_ref[hb, :, j * LANES:(j + 1) * LANES]], axis=0) for j in range(2)]
            sink = jnp.concatenate([jnp.full((BLOCK, LANES), sink_ref[ha], F32),
                                    jnp.full((BLOCK, LANES), sink_ref[hb], F32)], axis=0)
            m = jnp.maximum(sink, jnp.max(jnp.maximum(blocks[0], blocks[1]), axis=-1, keepdims=True))
            p = jnp.concatenate([jnp.exp2(b - m) for b in blocks], axis=1).astype(BF16)
            vb = band(vp_ref, vc_ref, i, 2 * g + par)
            ext.append(jnp.dot(p, vb, preferred_element_type=F32))
            extra.append(jnp.exp2(sink - m))
        for j in range(2):
            rows = slice(j * BLOCK, (j + 1) * BLOCK)
            o_ref[i * BLOCK:(i + 1) * BLOCK, (2 * g + j) * LANES:(2 * g + j + 1) * LANES] = _normalize_pair(
                ext[0][rows], ext[1][rows], extra[0][rows], extra[1][rows]).astype(BF16)

    units = [(i, g) for i in range(nqb) for g in range(SWA_KV_HEADS)]
    s_cur = qk(*units[0])
    for u, (i, g) in enumerate(units):
        s_next = qk(*units[u + 1]) if u + 1 < len(units) else None
        finish(i, g, s_cur)
        s_cur = s_next


def _swa_attn(sinks2, q, k, v, bias):
    B, S, _ = q.shape
    rows = SWA_QB * BLOCK
    cur = lambda w: pl.BlockSpec((None, rows, w), lambda b, n: (b, n, 0))
    prev = lambda w: pl.BlockSpec((None, BLOCK, w), lambda b, n: (b, jnp.maximum(n * SWA_QB - 1, 0), 0))
    bias_shape = (None, SWA_Q_HEADS, BLOCK, 2 * BLOCK)
    kw, vw = k.shape[-1], v.shape[-1]
    return pl.pallas_call(
        _swa_attn_kernel,
        grid=(B, S // rows),
        in_specs=[pl.BlockSpec(memory_space=pltpu.SMEM), cur(SWA_WIDTH), prev(kw), cur(kw), prev(vw), cur(vw),
                  pl.BlockSpec(bias_shape, lambda b, n: (jnp.minimum(n, 1), 0, 0, 0)),
                  pl.BlockSpec(bias_shape, lambda b, n: (1, 0, 0, 0))],
        out_specs=cur(SWA_WIDTH),
        out_shape=jax.ShapeDtypeStruct((B, S, SWA_WIDTH), BF16),
        compiler_params=_params(("parallel", "parallel")),
        name="swa_attn",
    )(sinks2, q, k, k, v, v, bias, bias)


def _rot(w):
    half = w.shape[-1] // 2
    return jnp.concatenate([-w[..., half:], w[..., :half]], axis=-1)


def _rope_tables(seq, c0):
    inv = ROPE_THETA ** (-np.arange(0, MLA_ROPE_DIM, 2, dtype=np.float64) / MLA_ROPE_DIM)
    ang = np.arange(seq, dtype=np.float64)[:, None] * inv[None, :]
    cos = np.concatenate([np.cos(ang)] * 2, axis=-1)
    sin = np.concatenate([np.sin(ang)] * 2, axis=-1)
    z32 = np.zeros((seq, LANES - HALF - MLA_ROPE_DIM))
    z64 = np.zeros((seq, HALF))
    c64 = np.full((seq, HALF), c0)
    aq_even = np.concatenate([c64, c0 * cos, z32], axis=-1)
    bq_even = np.concatenate([z64, c0 * sin, z32], axis=-1)
    aq_odd = np.concatenate([c0 * cos, z32, c64], axis=-1)
    bq_odd = np.concatenate([c0 * sin, z32, z64], axis=-1)
    ak = np.concatenate([z64, cos, z32], axis=-1)
    bk = np.concatenate([z64, sin, z32], axis=-1)
    return tuple(t.astype(np.float32) for t in (aq_even, bq_even, aq_odd, bq_odd, ak, bk))


def _mla_operands(w_in, q_norm, w_q_up, kv_norm, w_kv_up):
    L = w_in.shape[0]
    c2 = MLA_Q_RANK + MLA_KV_RANK
    c3 = c2 + MLA_ROPE_DIM
    kr = w_in[..., c2:c3]
    krg = jnp.concatenate([jnp.zeros((L, D_MODEL, HALF), F32), kr, _rot(kr)], axis=-1)
    win = jnp.concatenate([w_in[..., :c2], krg, w_in[..., c3:]], axis=-1).astype(BF16)
    odd = (jnp.arange(MLA_HEADS) % 2 == 1)[:, None]
    wq = w_q_up.reshape(L, MLA_Q_RANK, MLA_HEADS, MLA_NOPE_DIM + MLA_ROPE_DIM)
    nope, rope = wq[..., :MLA_NOPE_DIM], wq[..., MLA_NOPE_DIM:]
    wq = jnp.where(odd, jnp.concatenate([rope, _rot(rope), nope], axis=-1),
                   jnp.concatenate([nope, rope, _rot(rope)], axis=-1))
    wq = wq.reshape(L, MLA_Q_RANK, PADW).astype(BF16)
    wkv = w_kv_up.reshape(L, MLA_KV_RANK, MLA_HEADS, MLA_NOPE_DIM + MLA_V_DIM)
    wk = wkv[..., :MLA_NOPE_DIM].reshape(L, MLA_KV_RANK, MLA_HEADS * MLA_NOPE_DIM).astype(BF16)
    wvt = jnp.swapaxes(wkv[..., MLA_NOPE_DIM:].reshape(L, MLA_KV_RANK, MLA_WIDTH), 1, 2).astype(BF16)
    return win, q_norm.reshape(L, 1, -1), kv_norm.reshape(L, 1, -1), wq, wk, wvt


def _fox_layer_operands(w_in, b_f):
    W = FOX_WIDTH
    wf = jnp.concatenate([w_in[:, 3 * W:3 * W + FOX_HEADS], jnp.zeros((D_MODEL, LANES - FOX_HEADS), F32)], axis=-1)
    win = jnp.concatenate([w_in[:, :2 * W], w_in[:, 3 * W + FOX_HEADS:], wf], axis=-1).astype(BF16)
    wvt = w_in[:, 2 * W:3 * W].T.astype(BF16)
    bf = jnp.concatenate([b_f, jnp.zeros((LANES - FOX_HEADS,), F32)]).reshape(1, LANES)
    h = np.arange(FOX_HEADS)
    base = (h // 2) * LANES + np.where(h % 2 == 0, HALF, 0)
    scat = np.zeros((LANES, 2 * W), np.float32)
    onesq = np.zeros((1, W), np.float32)
    onesk = np.zeros((1, W), np.float32)
    for j in range(3):
        scat[j * FOX_HEADS + h, base + j] = 1.0
        scat[j * FOX_HEADS + h, W + base + 3 + j] = -1.0
        onesq[0, base + 3 + j] = 1.0
        onesk[0, base + j] = 1.0
    return win, wvt, bf, jnp.asarray(scat, dtype=BF16), onesq, onesk


def _swa_bias():
    qi = np.arange(BLOCK)[:, None]
    kj = np.arange(2 * BLOCK)[None, :]
    dist = qi + BLOCK - kj
    valid = (dist >= 0) & (dist < WINDOW)
    slopes = 2.0 ** (-8.0 * np.arange(1, SWA_Q_HEADS + 1, dtype=np.float64) / SWA_Q_HEADS)
    ali = -(slopes[:, None, None] * dist.astype(np.float64)[None]) * LOG2E
    bias_rest = np.where(valid[None], ali, NEG_INF)
    bias_first = np.where((valid & (kj >= BLOCK))[None], ali, NEG_INF)
    return np.stack([bias_first, bias_rest]).astype(np.float32)


def kernel(x, ln_g, ln_b, mla_w_in, mla_q_norm, mla_w_q_up, mla_kv_norm, mla_w_kv_up, mla_w_out,
           swa_w_in, swa_sinks, swa_w_out, fox_w_in, fox_b_f, fox_w_out):
    seq = x.shape[1]
    mla_ops = _mla_operands(mla_w_in, mla_q_norm, mla_w_q_up, mla_kv_norm, mla_w_kv_up)
    mla_tables = _rope_tables(seq, (MLA_NOPE_DIM + MLA_ROPE_DIM) ** -0.5 * LOG2E)
    w_outs = {"mla": mla_w_out.astype(BF16), "swa": swa_w_out.astype(BF16), "fox": fox_w_out.astype(BF16)}
    ln_g3, ln_b3 = ln_g.reshape(DEPTH, 1, D_MODEL), ln_b.reshape(DEPTH, 1, D_MODEL)
    out_ops = None
    for i in range(DEPTH + 1):
        j = i // N_MIXERS
        kind = ("mla", "swa", "fox")[i % N_MIXERS] if i < DEPTH else None
        if kind == "mla":
            proj_ops = tuple(_Layer(a, j) for a in mla_ops) + mla_tables
        elif kind == "swa":
            proj_ops = (swa_w_in[j].astype(BF16),)
        elif kind == "fox":
            proj_ops = _fox_layer_operands(fox_w_in[j], fox_b_f[j])
        else:
            proj_ops = ()
        res = _boundary(x, out_ops, kind, proj_ops)
        if out_ops is not None:
            x, res = res[0], res[1:]
        if kind is None:
            return x
        q, k, v, sg = res
        if kind == "swa":
            o = _swa_attn(swa_sinks[j].astype(F32) * LOG2E, q, k, v, _swa_bias())
        else:
            o = _causal_attn(q, k, v)
        out_ops = (o, sg, _Layer(w_outs[kind], j), _Layer(ln_g3, i), _Layer(ln_b3, i))
```

```python
import functools
import math

import jax
import jax.numpy as jnp
import numpy as np
from jax import lax
from jax.experimental import pallas as pl
from jax.experimental.pallas import tpu as pltpu

D_MODEL = 1024
DEPTH = 4
N_MIXERS = 3
BLOCK = 128
NEG_INF = -1e30
RMS_EPS = 1e-6
LN_EPS = 1e-5
DEEPNORM_ALPHA = (2 * DEPTH) ** 0.25
LOG2E = math.log2(math.e)

MLA_HEADS = 16
MLA_NOPE_DIM = 64
MLA_ROPE_DIM = 32
MLA_V_DIM = 64
MLA_Q_RANK = 384
MLA_KV_RANK = 256
ROPE_THETA = 10000.0
MLA_WIDTH = MLA_HEADS * MLA_V_DIM

SWA_Q_HEADS = 16
SWA_KV_HEADS = 4
SWA_HEAD_DIM = 64
WINDOW = 128
SWA_WIDTH = SWA_Q_HEADS * SWA_HEAD_DIM
SWA_KV_WIDTH = SWA_KV_HEADS * SWA_HEAD_DIM

FOX_HEADS = 16
FOX_HEAD_DIM = 64
FOX_WIDTH = FOX_HEADS * FOX_HEAD_DIM

LANES = 128
HALF = LANES // 2
HEADS = 16
PADW = HEADS * LANES

ROW_TILE = 512
SUB_ROWS = 256
TQ = 512
TK = 512
ATT_HEADS = 4
SWA_QB = 4
VMEM_LIMIT = 56 * 1024 * 1024

F32 = jnp.float32
BF16 = jnp.bfloat16
NT_DIMS = (((1,), (1,)), ((), ()))


def _params(sem):
    return pltpu.CompilerParams(dimension_semantics=sem, vmem_limit_bytes=VMEM_LIMIT)


def _lane_lt_half(shape):
    return lax.broadcasted_iota(jnp.int32, shape, len(shape) - 1) < HALF


def _silu(g):
    return g / (1.0 + jnp.exp(-g))


def _write_vt(vt, vt_out):
    top = lax.broadcasted_iota(jnp.int32, (LANES, vt.shape[1]), 0) < HALF
    for p in range(HEADS // 2):
        blk = vt[p * LANES:(p + 1) * LANES, :]
        vt_out[(2 * p) * LANES:(2 * p + 1) * LANES, :] = jnp.where(top, blk, 1.0).astype(BF16)
        vt_out[(2 * p + 1) * LANES:(2 * p + 2) * LANES, :] = jnp.where(top, 1.0, blk).astype(BF16)


def _mla_body(x, win_ref, qg_ref, kg_ref, wq_ref, wk_ref, wvt_ref, aqe_ref, bqe_ref, aqo_ref, bqo_ref,
              ak_ref, bk_ref, q_out, k_out, vt_out, g_out):
    h = jnp.dot(x.astype(BF16), win_ref[...], preferred_element_type=F32)
    c1 = MLA_Q_RANK
    c2 = c1 + MLA_KV_RANK
    c3 = c2 + LANES
    cq, ckv, krg, gate = h[:, :c1], h[:, c1:c2], h[:, c2:c3], h[:, c3:]
    g_out[...] = _silu(gate).astype(BF16)

    qn = cq * lax.rsqrt(jnp.mean(cq * cq, axis=-1, keepdims=True) + RMS_EPS) * qg_ref[...]
    kn = (ckv * lax.rsqrt(jnp.mean(ckv * ckv, axis=-1, keepdims=True) + RMS_EPS) * kg_ref[...]).astype(BF16)
    q = jnp.dot(qn.astype(BF16), wq_ref[...], preferred_element_type=F32)
    kc = jnp.dot(kn, wk_ref[...], preferred_element_type=F32)
    _write_vt(lax.dot_general(wvt_ref[...], kn, NT_DIMS, preferred_element_type=F32), vt_out)

    shift = LANES - MLA_ROPE_DIM
    aq = (aqe_ref[...], aqo_ref[...])
    bq = (bqe_ref[...], bqo_ref[...])
    kr_hi = krg * ak_ref[...] + pltpu.roll(krg, shift, 1) * bk_ref[...]
    kr_lo = pltpu.roll(kr_hi, HALF, 1)
    lo = _lane_lt_half((x.shape[0], LANES))
    for p in range(HEADS // 2):
        e = slice((2 * p) * LANES, (2 * p + 1) * LANES)
        o = slice((2 * p + 1) * LANES, (2 * p + 2) * LANES)
        kblk = kc[:, p * LANES:(p + 1) * LANES]
        q_out[:, e] = (q[:, e] * aq[0] + pltpu.roll(q[:, e], shift, 1) * bq[0]).astype(BF16)
        q_out[:, o] = (q[:, o] * aq[1] + pltpu.roll(q[:, o], shift, 1) * bq[1]).astype(BF16)
        k_out[:, e] = jnp.where(lo, kblk, kr_hi).astype(BF16)
        k_out[:, o] = jnp.where(lo, kr_lo, kblk).astype(BF16)


def _split_pairs(nat, aug, out_ref):
    lo = _lane_lt_half((nat.shape[0], LANES))
    for p in range(HEADS // 2):
        blk = nat[:, p * LANES:(p + 1) * LANES]
        fill = aug[:, p * LANES:(p + 1) * LANES]
        out_ref[:, (2 * p) * LANES:(2 * p + 1) * LANES] = jnp.where(lo, blk, fill).astype(BF16)
        out_ref[:, (2 * p + 1) * LANES:(2 * p + 2) * LANES] = jnp.where(lo, fill, blk).astype(BF16)


def _fox_body(x, win_ref, wvt_ref, bf_ref, scat_ref, onesq_ref, onesk_ref, q_out, k_out, vt_out, g_out, carry_ref):
    tm = x.shape[0]
    xb = x.astype(BF16)
    h = jnp.dot(xb, win_ref[...], preferred_element_type=F32)
    W = FOX_WIDTH
    g_out[...] = _silu(h[:, 2 * W:3 * W]).astype(BF16)
    _write_vt(lax.dot_general(wvt_ref[...], xb, NT_DIMS, preferred_element_type=F32), vt_out)

    f = h[:, 3 * W:3 * W + LANES] + bf_ref[...]
    logf = jnp.minimum(f, 0.0) - jnp.log(1.0 + jnp.exp(-jnp.abs(f)))

    def split3(a):
        hi = a.astype(BF16)
        r = a - hi.astype(F32)
        mid = r.astype(BF16)
        lo = (r - mid.astype(F32)).astype(BF16)
        return hi, mid, lo

    ri = lax.broadcasted_iota(jnp.int32, (tm, tm), 0)
    ci = lax.broadcasted_iota(jnp.int32, (tm, tm), 1)
    tri = (ci <= ri).astype(BF16)
    l_hi, l_mid, l_lo = split3(logf)
    cum = (jnp.dot(tri, l_hi, preferred_element_type=F32) + jnp.dot(tri, l_mid, preferred_element_type=F32)
           + jnp.dot(tri, l_lo, preferred_element_type=F32)) + carry_ref[...]
    carry_ref[...] = cum[tm - 1:tm, :]

    c_hi, c_mid, c_lo = split3(cum * LOG2E)
    lane = lax.broadcasted_iota(jnp.int32, (tm, LANES), 1)
    packed = jnp.where(lane < HEADS, c_hi.astype(F32),
                       jnp.where(lane < 2 * HEADS, pltpu.roll(c_mid.astype(F32), HEADS, 1),
                                 pltpu.roll(c_lo.astype(F32), 2 * HEADS, 1)))
    packed = jnp.where(lane < 3 * HEADS, packed, 0.0).astype(BF16)
    aug = jnp.dot(packed, scat_ref[...], preferred_element_type=F32)
    augq = aug[:, :W] + onesq_ref[...]
    augk = aug[:, W:] + onesk_ref[...]

    _split_pairs(h[:, :W] * (FOX_HEAD_DIM ** -0.5 * LOG2E), augq, q_out)
    _split_pairs(h[:, W:2 * W], augk, k_out)


def _swa_body(x, win_ref, q_out, k_out, v_out, g_out):
    h = jnp.dot(x.astype(BF16), win_ref[...], preferred_element_type=F32)
    c1 = SWA_WIDTH
    c2 = c1 + SWA_KV_WIDTH
    c3 = c2 + SWA_KV_WIDTH
    q_out[...] = (h[:, :c1] * (SWA_HEAD_DIM ** -0.5 * LOG2E)).astype(BF16)
    g_out[...] = _silu(h[:, c3:]).astype(BF16)
    lo = _lane_lt_half((x.shape[0], LANES))
    for pair in range(SWA_KV_HEADS // 2):
        kp = h[:, c1 + pair * LANES:c1 + (pair + 1) * LANES]
        vp = h[:, c2 + pair * LANES:c2 + (pair + 1) * LANES]
        ks, vs = pltpu.roll(kp, HALF, 1), pltpu.roll(vp, HALF, 1)
        for t, (k_h, v_h) in enumerate(((jnp.where(lo, kp, ks), jnp.where(lo, vp, vs)),
                                        (jnp.where(lo, ks, kp), jnp.where(lo, vs, vp)))):
            g = 2 * pair + t
            k_out[:, g * LANES:(g + 1) * LANES] = k_h.astype(BF16)
            v_out[:, (2 * g) * LANES:(2 * g + 1) * LANES] = jnp.where(lo, v_h, 1.0).astype(BF16)
            v_out[:, (2 * g + 1) * LANES:(2 * g + 2) * LANES] = jnp.where(lo, 1.0, v_h).astype(BF16)


VT = "vt"
_PROJ = {
    "mla": (_mla_body, 12, (PADW, PADW, VT, MLA_WIDTH), 6),
    "fox": (_fox_body, 6, (PADW, PADW, VT, FOX_WIDTH), 0),
    "swa": (_swa_body, 1, (SWA_WIDTH, SWA_KV_HEADS * LANES, SWA_KV_HEADS * 2 * LANES, SWA_WIDTH), 0),
}


def _out_body(o_ref, g_ref, x_ref, w_ref, lg_ref, lb_ref):
    a = (o_ref[...].astype(F32) * g_ref[...].astype(F32)).astype(BF16)
    y = jnp.dot(a, w_ref[...], preferred_element_type=F32)
    z = DEEPNORM_ALPHA * x_ref[...] + y
    mu = jnp.mean(z, axis=-1, keepdims=True)
    zc = z - mu
    var = jnp.mean(zc * zc, axis=-1, keepdims=True)
    return zc * lax.rsqrt(var + LN_EPS) * lg_ref[...] + lb_ref[...]


def _boundary_kernel(*refs, with_out, kind):
    refs = list(refs)
    if with_out:
        out_in, refs = refs[:6], refs[6:]
    else:
        x_ref, refs = refs[0], refs[1:]
    n_ops, widths, n_tab = _PROJ[kind][1:] if kind else (0, (), 0)
    proj_in, refs = refs[:n_ops], refs[n_ops:]
    if with_out:
        y_ref, refs = refs[0], refs[1:]
    outs, scratch = refs[:len(widths)], refs[len(widths):]
    if kind == "fox":
        @pl.when(pl.program_id(1) == 0)
        def _():
            scratch[0][...] = jnp.zeros_like(scratch[0])

    tm = (out_in[2] if with_out else x_ref).shape[0]
    blocks = [pl.ds(r * SUB_ROWS, SUB_ROWS) for r in range(tm // SUB_ROWS)]
    xs = []
    for rows in blocks:
        if with_out:
            xs.append(_out_body(*(ref.at[rows] for ref in out_in[:3]), *out_in[3:]))
            y_ref[rows, :] = xs[-1]
        else:
            xs.append(x_ref[rows, :])
    for rows, x in zip(blocks, xs):
        if kind:
            _PROJ[kind][0](x, *proj_in[:n_ops - n_tab], *(t.at[rows] for t in proj_in[n_ops - n_tab:]),
                           *(o.at[:, rows] if w == VT else o.at[rows] for o, w in zip(outs, widths)), *scratch)


class _Layer:
    def __init__(self, arr, layer):
        self.arr, self.layer = arr, layer


def _boundary(x, out_ops, kind, proj_ops):
    B, S, _ = x.shape
    tm = ROW_TILE if kind else 2 * ROW_TILE
    row = lambda w: pl.BlockSpec((None, tm, w), lambda b, i: (b, i, 0))

    def full(a):
        if isinstance(a, _Layer):
            idx = (a.layer,) + (0,) * (a.arr.ndim - 1)
            return pl.BlockSpec((None,) + a.arr.shape[1:], lambda b, i: idx, pipeline_mode=pl.Buffered(1))
        return pl.BlockSpec(a.shape, lambda b, i: (0,) * a.ndim, pipeline_mode=pl.Buffered(1))

    unwrap = lambda a: a.arr if isinstance(a, _Layer) else a
    tab = pl.BlockSpec((tm, LANES), lambda b, i: (i, 0))
    in_specs, args, out_specs, out_shape, scratch = [], [], [], [], []
    if out_ops is not None:
        o, sg, w, lg, lb = out_ops
        in_specs += [row(D_MODEL), row(D_MODEL), row(D_MODEL), full(w), full(lg), full(lb)]
        args += [o, sg, x, unwrap(w), unwrap(lg), unwrap(lb)]
        out_specs.append(row(D_MODEL))
        out_shape.append(jax.ShapeDtypeStruct((B, S, D_MODEL), F32))
    else:
        in_specs.append(row(D_MODEL))
        args.append(x)
    if kind:
        _, n_ops, widths, n_tab = _PROJ[kind]
        assert len(proj_ops) == n_ops
        in_specs += [full(a) for a in proj_ops[:n_ops - n_tab]] + [tab] * n_tab
        args += [unwrap(a) for a in proj_ops]
        for w in widths:
            if w == VT:
                assert TK % tm == 0
                out_specs.append(pl.BlockSpec((None, None, PADW, tm),
                                              lambda b, i: (b, i * tm // TK, 0, i % (TK // tm))))
                out_shape.append(jax.ShapeDtypeStruct((B, S // TK, PADW, TK), BF16))
            else:
                out_specs.append(row(w))
                out_shape.append(jax.ShapeDtypeStruct((B, S, w), BF16))
        if kind == "fox":
            scratch.append(pltpu.VMEM((1, LANES), F32))
    sem = ("parallel", "arbitrary" if kind == "fox" else "parallel")
    return pl.pallas_call(
        functools.partial(_boundary_kernel, with_out=out_ops is not None, kind=kind),
        grid=(B, S // tm),
        in_specs=in_specs,
        out_specs=out_specs,
        out_shape=out_shape,
        scratch_shapes=scratch,
        compiler_params=_params(sem),
        name="boundary_" + ("out_" if out_ops is not None else "") + (kind or "final"),
    )(*args)


def _normalize_pair(ext_even, ext_odd, extra_even=None, extra_odd=None):
    lo = _lane_lt_half(ext_even.shape)
    den = jnp.where(lo, ext_odd, ext_even)
    if extra_even is not None:
        den = den + jnp.where(lo, extra_odd, extra_even)
    return jnp.where(lo, ext_even, ext_odd) * pltpu.roll(1.0 / den, HALF, 1)


def _causal_attn_kernel(q_ref, k_ref, vt_ref, o_ref, m_ref, acc_ref):
    S = q_ref.shape[0]
    heads = range(ATT_HEADS)
    lanes = [slice(h * LANES, (h + 1) * LANES) for h in heads]

    def tile(q0, kt, jobs, diagonal):
        chains = [(h,) + job for job in jobs for h in heads]
        k0 = kt * TK
        s_all = [lax.dot_general(q_ref[pl.ds(q0 + roff, nq), lanes[h]], k_ref[pl.ds(k0, nk), lanes[h]],
                                 NT_DIMS, preferred_element_type=F32) for h, roff, nq, nk in chains]
        p_all, alpha_all = [], []
        for (h, roff, nq, nk), s in zip(chains, s_all):
            st = slice(roff, roff + nq)
            blocks = [s[:, j * LANES:(j + 1) * LANES] for j in range(nk // LANES)]
            if diagonal:
                row = lax.broadcasted_iota(jnp.int32, (nq, LANES), 0) + roff
                col = lax.broadcasted_iota(jnp.int32, (nq, LANES), 1)
                blocks = [b if (j + 1) * LANES - 1 <= roff else jnp.where(col + j * LANES <= row, b, NEG_INF)
                          for j, b in enumerate(blocks)]
            m_blk = jnp.max(functools.reduce(jnp.maximum, blocks), axis=-1, keepdims=True)
            if diagonal:
                m_new = jnp.broadcast_to(m_blk, (nq, LANES))
                alpha_all.append(None)
            else:
                m_old = m_ref[h, st, :]
                m_new = jnp.maximum(m_old, m_blk)
                alpha_all.append(jnp.exp2(m_old - m_new))
            p_all.append(jnp.concatenate([jnp.exp2(b - m_new) for b in blocks], axis=1).astype(BF16))
            m_ref[h, st, :] = m_new
        for (h, roff, nq, nk), p, alpha in zip(chains, p_all, alpha_all):
            st = slice(roff, roff + nq)
            pv = lax.dot_general(p, vt_ref[kt, lanes[h], 0:nk], NT_DIMS, preferred_element_type=F32)
            acc_ref[h, st, :] = pv if alpha is None else acc_ref[h, st, :] * alpha + pv

    half = TQ // 2
    for qi in range(S // TQ):
        q0 = qi * TQ
        tile(q0, qi, [(0, half, half), (half, half, TQ)], True)
        for kj in range(qi):
            tile(q0, kj, [(0, TQ, TK)], False)
        for p in range(ATT_HEADS // 2):
            o_ref[q0:q0 + TQ, p * LANES:(p + 1) * LANES] = _normalize_pair(
                acc_ref[2 * p], acc_ref[2 * p + 1]).astype(BF16)


def _causal_attn(q, k, vt):
    B, S, _ = q.shape
    spec = pl.BlockSpec((None, S, ATT_HEADS * LANES), lambda b, p: (b, 0, p))
    return pl.pallas_call(
        _causal_attn_kernel,
        grid=(B, HEADS // ATT_HEADS),
        in_specs=[spec, spec, pl.BlockSpec((None, S // TK, ATT_HEADS * LANES, TK), lambda b, p: (b, 0, p, 0))],
        out_specs=pl.BlockSpec((None, S, ATT_HEADS * HALF), lambda b, p: (b, 0, p)),
        out_shape=jax.ShapeDtypeStruct((B, S, HEADS * HALF), BF16),
        scratch_shapes=[pltpu.VMEM((ATT_HEADS, TQ, LANES), F32), pltpu.VMEM((ATT_HEADS, TQ, LANES), F32)],
        compiler_params=_params(("parallel", "parallel")),
        name="causal_attn",
    )(q, k, vt)


def _swa_attn_kernel(sink_ref, q_ref, kp_ref, kc_ref, vp_ref, vc_ref, bias0_ref, bias_ref, o_ref):
    nqb = q_ref.shape[0] // BLOCK
    lo = _lane_lt_half((BLOCK, LANES))
    zero = jnp.zeros((), BF16)

    def band(prev_ref, cur_ref, i, group):
        cols = slice(group * LANES, (group + 1) * LANES)
        if i == 0:
            return jnp.concatenate([prev_ref[:, cols], cur_ref[0:BLOCK, cols]], axis=0)
        return cur_ref[(i - 1) * BLOCK:(i + 1) * BLOCK, cols]

    def qk(i, g):
        rows = slice(i * BLOCK, (i + 1) * BLOCK)
        kb = band(kp_ref, kc_ref, i, g)
        qa = q_ref[rows, (2 * g) * LANES:(2 * g + 1) * LANES]
        qb = q_ref[rows, (2 * g + 1) * LANES:(2 * g + 2) * LANES]
        out = []
        for par in range(2):
            keep = lo if par == 0 else jnp.logical_not(lo)
            stack = jnp.concatenate([jnp.where(keep, qa, zero), jnp.where(keep, qb, zero)], axis=0)
            out.append(lax.dot_general(stack, kb, NT_DIMS, preferred_element_type=F32))
        return out

    def finish(i, g, s_pair):
        b_ref = bias0_ref if i == 0 else bias_ref
        ext, extra = [], []
        for par, s in enumerate(s_pair):
            ha, hb = 4 * g + par, 4 * g + 2 + par
            blocks = [s[:, j * LANES:(j + 1) * LANES]
                      + jnp.concatenate([b_ref[ha, :, j * LANES:(j + 1) * LANES],
                                         b_ref[hb, :, j * LANES:(j + 1) * LANES]], axis=0) for j in range(2)]
            sink = jnp.concatenate([jnp.full((BLOCK, LANES), sink_ref[ha], F32),
                                    jnp.full((BLOCK, LANES), sink_ref[hb], F32)], axis=0)
            m = jnp.maximum(sink, jnp.max(jnp.maximum(blocks[0], blocks[1]), axis=-1, keepdims=True))
            p = jnp.concatenate([jnp.exp2(b - m) for b in blocks], axis=1).astype(BF16)
            vb = band(vp_ref, vc_ref, i, 2 * g + par)
            ext.append(jnp.dot(p, vb, preferred_element_type=F32))
            extra.append(jnp.exp2(sink - m))
        for j in range(2):
            rows = slice(j * BLOCK, (j + 1) * BLOCK)
            o_ref[i * BLOCK:(i + 1) * BLOCK, (2 * g + j) * LANES:(2 * g + j + 1) * LANES] = _normalize_pair(
                ext[0][rows], ext[1][rows], extra[0][rows], extra[1][rows]).astype(BF16)

    units = [(i, g) for i in range(nqb) for g in range(SWA_KV_HEADS)]
    s_cur = qk(*units[0])
    for u, (i, g) in enumerate(units):
        s_next = qk(*units[u + 1]) if u + 1 < len(units) else None
        finish(i, g, s_cur)
        s_cur = s_next


def _swa_attn(sinks2, q, k, v, bias):
    B, S, _ = q.shape
    rows = SWA_QB * BLOCK
    cur = lambda w: pl.BlockSpec((None, rows, w), lambda b, n: (b, n, 0))
    prev = lambda w: pl.BlockSpec((None, BLOCK, w), lambda b, n: (b, jnp.maximum(n * SWA_QB - 1, 0), 0))
    bias_shape = (None, SWA_Q_HEADS, BLOCK, 2 * BLOCK)
    kw, vw = k.shape[-1], v.shape[-1]
    return pl.pallas_call(
        _swa_attn_kernel,
        grid=(B, S // rows),
        in_specs=[pl.BlockSpec(memory_space=pltpu.SMEM), cur(SWA_WIDTH), prev(kw), cur(kw), prev(vw), cur(vw),
                  pl.BlockSpec(bias_shape, lambda b, n: (jnp.minimum(n, 1), 0, 0, 0)),
                  pl.BlockSpec(bias_shape, lambda b, n: (1, 0, 0, 0))],
        out_specs=cur(SWA_WIDTH),
        out_shape=jax.ShapeDtypeStruct((B, S, SWA_WIDTH), BF16),
        compiler_params=_params(("parallel", "parallel")),
        name="swa_attn",
    )(sinks2, q, k, k, v, v, bias, bias)


def _rot(w):
    half = w.shape[-1] // 2
    return jnp.concatenate([-w[..., half:], w[..., :half]], axis=-1)


def _rope_tables(seq, c0):
    inv = ROPE_THETA ** (-np.arange(0, MLA_ROPE_DIM, 2, dtype=np.float64) / MLA_ROPE_DIM)
    ang = np.arange(seq, dtype=np.float64)[:, None] * inv[None, :]
    cos = np.concatenate([np.cos(ang)] * 2, axis=-1)
    sin = np.concatenate([np.sin(ang)] * 2, axis=-1)
    z32 = np.zeros((seq, LANES - HALF - MLA_ROPE_DIM))
    z64 = np.zeros((seq, HALF))
    c64 = np.full((seq, HALF), c0)
    aq_even = np.concatenate([c64, c0 * cos, z32], axis=-1)
    bq_even = np.concatenate([z64, c0 * sin, z32], axis=-1)
    aq_odd = np.concatenate([c0 * cos, z32, c64], axis=-1)
    bq_odd = np.concatenate([c0 * sin, z32, z64], axis=-1)
    ak = np.concatenate([z64, cos, z32], axis=-1)
    bk = np.concatenate([z64, sin, z32], axis=-1)
    return tuple(t.astype(np.float32) for t in (aq_even, bq_even, aq_odd, bq_odd, ak, bk))


def _mla_operands(w_in, q_norm, w_q_up, kv_norm, w_kv_up):
    L = w_in.shape[0]
    c2 = MLA_Q_RANK + MLA_KV_RANK
    c3 = c2 + MLA_ROPE_DIM
    kr = w_in[..., c2:c3]
    krg = jnp.concatenate([jnp.zeros((L, D_MODEL, HALF), F32), kr, _rot(kr)], axis=-1)
    win = jnp.concatenate([w_in[..., :c2], krg, w_in[..., c3:]], axis=-1).astype(BF16)
    odd = (jnp.arange(MLA_HEADS) % 2 == 1)[:, None]
    wq = w_q_up.reshape(L, MLA_Q_RANK, MLA_HEADS, MLA_NOPE_DIM + MLA_ROPE_DIM)
    nope, rope = wq[..., :MLA_NOPE_DIM], wq[..., MLA_NOPE_DIM:]
    wq = jnp.where(odd, jnp.concatenate([rope, _rot(rope), nope], axis=-1),
                   jnp.concatenate([nope, rope, _rot(rope)], axis=-1))
    wq = wq.reshape(L, MLA_Q_RANK, PADW).astype(BF16)
    wkv = w_kv_up.reshape(L, MLA_KV_RANK, MLA_HEADS, MLA_NOPE_DIM + MLA_V_DIM)
    wk = wkv[..., :MLA_NOPE_DIM].reshape(L, MLA_KV_RANK, MLA_HEADS * MLA_NOPE_DIM).astype(BF16)
    wvt = jnp.swapaxes(wkv[..., MLA_NOPE_DIM:].reshape(L, MLA_KV_RANK, MLA_WIDTH), 1, 2).astype(BF16)
    return win, q_norm.reshape(L, 1, -1), kv_norm.reshape(L, 1, -1), wq, wk, wvt


def _fox_layer_operands(w_in, b_f):
    W = FOX_WIDTH
    wf = jnp.concatenate([w_in[:, 3 * W:3 * W + FOX_HEADS], jnp.zeros((D_MODEL, LANES - FOX_HEADS), F32)], axis=-1)
    win = jnp.concatenate([w_in[:, :2 * W], w_in[:, 3 * W + FOX_HEADS:], wf], axis=-1).astype(BF16)
    wvt = w_in[:, 2 * W:3 * W].T.astype(BF16)
    bf = jnp.concatenate([b_f, jnp.zeros((LANES - FOX_HEADS,), F32)]).reshape(1, LANES)
    h = np.arange(FOX_HEADS)
    base = (h // 2) * LANES + np.where(h % 2 == 0, HALF, 0)
    scat = np.zeros((LANES, 2 * W), np.float32)
    onesq = np.zeros((1, W), np.float32)
    onesk = np.zeros((1, W), np.float32)
    for j in range(3):
        scat[j * FOX_HEADS + h, base + j] = 1.0
        scat[j * FOX_HEADS + h, W + base + 3 + j] = -1.0
        onesq[0, base + 3 + j] = 1.0
        onesk[0, base + j] = 1.0
    return win, wvt, bf, jnp.asarray(scat, dtype=BF16), onesq, onesk


def _swa_bias():
    qi = np.arange(BLOCK)[:, None]
    kj = np.arange(2 * BLOCK)[None, :]
    dist = qi + BLOCK - kj
    valid = (dist >= 0) & (dist < WINDOW)
    slopes = 2.0 ** (-8.0 * np.arange(1, SWA_Q_HEADS + 1, dtype=np.float64) / SWA_Q_HEADS)
    ali = -(slopes[:, None, None] * dist.astype(np.float64)[None]) * LOG2E
    bias_rest = np.where(valid[None], ali, NEG_INF)
    bias_first = np.where((valid & (kj >= BLOCK))[None], ali, NEG_INF)
    return np.stack([bias_first, bias_rest]).astype(np.float32)


def kernel(x, ln_g, ln_b, mla_w_in, mla_q_norm, mla_w_q_up, mla_kv_norm, mla_w_kv_up, mla_w_out,
           swa_w_in, swa_sinks, swa_w_out, fox_w_in, fox_b_f, fox_w_out):
    seq = x.shape[1]
    mla_ops = _mla_operands(mla_w_in, mla_q_norm, mla_w_q_up, mla_kv_norm, mla_w_kv_up)
    mla_tables = _rope_tables(seq, (MLA_NOPE_DIM + MLA_ROPE_DIM) ** -0.5 * LOG2E)
    w_outs = {"mla": mla_w_out.astype(BF16), "swa": swa_w_out.astype(BF16), "fox": fox_w_out.astype(BF16)}
    ln_g3, ln_b3 = ln_g.reshape(DEPTH, 1, D_MODEL), ln_b.reshape(DEPTH, 1, D_MODEL)
    out_ops = None
    for i in range(DEPTH + 1):
        j = i // N_MIXERS
        kind = ("mla", "swa", "fox")[i % N_MIXERS] if i < DEPTH else None
        if kind == "mla":
            proj_ops = tuple(_Layer(a, j) for a in mla_ops) + mla_tables
        elif kind == "swa":
            proj_ops = (swa_w_in[j].astype(BF16),)
        elif kind == "fox":
            proj_ops = _fox_layer_operands(fox_w_in[j], fox_b_f[j])
        else:
            proj_ops = ()
        res = _boundary(x, out_ops, kind, proj_ops)
        if out_ops is not None:
            x, res = res[0], res[1:]
        if kind is None:
            return x
        q, k, v, sg = res
        if kind == "swa":
            o = _swa_attn(swa_sinks[j].astype(F32) * LOG2E, q, k, v, _swa_bias())
        else:
            o = _causal_attn(q, k, v)
        out_ops = (o, sg, _Layer(w_outs[kind], j), _Layer(ln_g3, i), _Layer(ln_b3, i))
```

```python
import functools
import math

import jax
import jax.numpy as jnp
import numpy as np
from jax import lax
from jax.experimental import pallas as pl
from jax.experimental.pallas import tpu as pltpu

D_MODEL = 1024
DEPTH = 4
N_MIXERS = 3
BLOCK = 128
NEG_INF = -1e30
RMS_EPS = 1e-6
LN_EPS = 1e-5
DEEPNORM_ALPHA = (2 * DEPTH) ** 0.25
LOG2E = math.log2(math.e)

MLA_HEADS = 16
MLA_NOPE_DIM = 64
MLA_ROPE_DIM = 32
MLA_V_DIM = 64
MLA_Q_RANK = 384
MLA_KV_RANK = 256
ROPE_THETA = 10000.0
MLA_WIDTH = MLA_HEADS * MLA_V_DIM

SWA_Q_HEADS = 16
SWA_KV_HEADS = 4
SWA_HEAD_DIM = 64
WINDOW = 128
SWA_WIDTH = SWA_Q_HEADS * SWA_HEAD_DIM
SWA_KV_WIDTH = SWA_KV_HEADS * SWA_HEAD_DIM

FOX_HEADS = 16
FOX_HEAD_DIM = 64
FOX_WIDTH = FOX_HEADS * FOX_HEAD_DIM

LANES = 128
HALF = LANES // 2
HEADS = 16
PADW = HEADS * LANES

ROW_TILE = 512
SUB_ROWS = 256
TQ = 512
TK = 512
ATT_HEADS = 4
SWA_QB = 4
VMEM_LIMIT = 56 * 1024 * 1024

F32 = jnp.float32
BF16 = jnp.bfloat16
NT_DIMS = (((1,), (1,)), ((), ()))


def _params(sem):
    return pltpu.CompilerParams(dimension_semantics=sem, vmem_limit_bytes=VMEM_LIMIT)


def _lane_lt_half(shape):
    return lax.broadcasted_iota(jnp.int32, shape, len(shape) - 1) < HALF


def _silu(g):
    return g / (1.0 + jnp.exp(-g))


def _write_vt(vt, vt_out):
    top = lax.broadcasted_iota(jnp.int32, (LANES, vt.shape[1]), 0) < HALF
    for p in range(HEADS // 2):
        blk = vt[p * LANES:(p + 1) * LANES, :]
        vt_out[(2 * p) * LANES:(2 * p + 1) * LANES, :] = jnp.where(top, blk, 1.0).astype(BF16)
        vt_out[(2 * p + 1) * LANES:(2 * p + 2) * LANES, :] = jnp.where(top, 1.0, blk).astype(BF16)


def _mla_body(x, win_ref, qg_ref, kg_ref, wq_ref, wk_ref, wvt_ref, aqe_ref, bqe_ref, aqo_ref, bqo_ref,
              ak_ref, bk_ref, q_out, k_out, vt_out, g_out):
    h = jnp.dot(x.astype(BF16), win_ref[...], preferred_element_type=F32)
    c1 = MLA_Q_RANK
    c2 = c1 + MLA_KV_RANK
    c3 = c2 + LANES
    cq, ckv, krg, gate = h[:, :c1], h[:, c1:c2], h[:, c2:c3], h[:, c3:]
    g_out[...] = _silu(gate).astype(BF16)

    qn = cq * lax.rsqrt(jnp.mean(cq * cq, axis=-1, keepdims=True) + RMS_EPS) * qg_ref[...]
    kn = (ckv * lax.rsqrt(jnp.mean(ckv * ckv, axis=-1, keepdims=True) + RMS_EPS) * kg_ref[...]).astype(BF16)
    q = jnp.dot(qn.astype(BF16), wq_ref[...], preferred_element_type=F32)
    kc = jnp.dot(kn, wk_ref[...], preferred_element_type=F32)
    _write_vt(lax.dot_general(wvt_ref[...], kn, NT_DIMS, preferred_element_type=F32), vt_out)

    shift = LANES - MLA_ROPE_DIM
    aq = (aqe_ref[...], aqo_ref[...])
    bq = (bqe_ref[...], bqo_ref[...])
    kr_hi = krg * ak_ref[...] + pltpu.roll(krg, shift, 1) * bk_ref[...]
    kr_lo = pltpu.roll(kr_hi, HALF, 1)
    lo = _lane_lt_half((x.shape[0], LANES))
    for p in range(HEADS // 2):
        e = slice((2 * p) * LANES, (2 * p + 1) * LANES)
        o = slice((2 * p + 1) * LANES, (2 * p + 2) * LANES)
        kblk = kc[:, p * LANES:(p + 1) * LANES]
        q_out[:, e] = (q[:, e] * aq[0] + pltpu.roll(q[:, e], shift, 1) * bq[0]).astype(BF16)
        q_out[:, o] = (q[:, o] * aq[1] + pltpu.roll(q[:, o], shift, 1) * bq[1]).astype(BF16)
        k_out[:, e] = jnp.where(lo, kblk, kr_hi).astype(BF16)
        k_out[:, o] = jnp.where(lo, kr_lo, kblk).astype(BF16)


def _split_pairs(nat, aug, out_ref):
    lo = _lane_lt_half((nat.shape[0], LANES))
    for p in range(HEADS // 2):
        blk = nat[:, p * LANES:(p + 1) * LANES]
        fill = aug[:, p * LANES:(p + 1) * LANES]
        out_ref[:, (2 * p) * LANES:(2 * p + 1) * LANES] = jnp.where(lo, blk, fill).astype(BF16)
        out_ref[:, (2 * p + 1) * LANES:(2 * p + 2) * LANES] = jnp.where(lo, fill, blk).astype(BF16)


def _fox_body(x, win_ref, wvt_ref, bf_ref, scat_ref, onesq_ref, onesk_ref, q_out, k_out, vt_out, g_out, carry_ref):
    tm = x.shape[0]
    xb = x.astype(BF16)
    h = jnp.dot(xb, win_ref[...], preferred_element_type=F32)
    W = FOX_WIDTH
    g_out[...] = _silu(h[:, 2 * W:3 * W]).astype(BF16)
    _write_vt(lax.dot_general(wvt_ref[...], xb, NT_DIMS, preferred_element_type=F32), vt_out)

    f = h[:, 3 * W:3 * W + LANES] + bf_ref[...]
    logf = jnp.minimum(f, 0.0) - jnp.log(1.0 + jnp.exp(-jnp.abs(f)))

    def split3(a):
        hi = a.astype(BF16)
        r = a - hi.astype(F32)
        mid = r.astype(BF16)
        lo = (r - mid.astype(F32)).astype(BF16)
        return hi, mid, lo

    ri = lax.broadcasted_iota(jnp.int32, (tm, tm), 0)
    ci = lax.broadcasted_iota(jnp.int32, (tm, tm), 1)
    tri = (ci <= ri).astype(BF16)
    l_hi, l_mid, l_lo = split3(logf)
    cum = (jnp.dot(tri, l_hi, preferred_element_type=F32) + jnp.dot(tri, l_mid, preferred_element_type=F32)
           + jnp.dot(tri, l_lo, preferred_element_type=F32)) + carry_ref[...]
    carry_ref[...] = cum[tm - 1:tm, :]

    c_hi, c_mid, c_lo = split3(cum * LOG2E)
    lane = lax.broadcasted_iota(jnp.int32, (tm, LANES), 1)
    packed = jnp.where(lane < HEADS, c_hi.astype(F32),
                       jnp.where(lane < 2 * HEADS, pltpu.roll(c_mid.astype(F32), HEADS, 1),
                                 pltpu.roll(c_lo.astype(F32), 2 * HEADS, 1)))
    packed = jnp.where(lane < 3 * HEADS, packed, 0.0).astype(BF16)
    aug = jnp.dot(packed, scat_ref[...], preferred_element_type=F32)
    augq = aug[:, :W] + onesq_ref[...]
    augk = aug[:, W:] + onesk_ref[...]

    _split_pairs(h[:, :W] * (FOX_HEAD_DIM ** -0.5 * LOG2E), augq, q_out)
    _split_pairs(h[:, W:2 * W], augk, k_out)


def _swa_body(x, win_ref, q_out, k_out, v_out, g_out):
    h = jnp.dot(x.astype(BF16), win_ref[...], preferred_element_type=F32)
    c1 = SWA_WIDTH
    c2 = c1 + SWA_KV_WIDTH
    c3 = c2 + SWA_KV_WIDTH
    q_out[...] = (h[:, :c1] * (SWA_HEAD_DIM ** -0.5 * LOG2E)).astype(BF16)
    g_out[...] = _silu(h[:, c3:]).astype(BF16)
    lo = _lane_lt_half((x.shape[0], LANES))
    for pair in range(SWA_KV_HEADS // 2):
        kp = h[:, c1 + pair * LANES:c1 + (pair + 1) * LANES]
        vp = h[:, c2 + pair * LANES:c2 + (pair + 1) * LANES]
        ks, vs = pltpu.roll(kp, HALF, 1), pltpu.roll(vp, HALF, 1)
        for t, (k_h, v_h) in enumerate(((jnp.where(lo, kp, ks), jnp.where(lo, vp, vs)),
                                        (jnp.where(lo, ks, kp), jnp.where(lo, vs, vp)))):
            g = 2 * pair + t
            k_out[:, g * LANES:(g + 1) * LANES] = k_h.astype(BF16)
            v_out[:, (2 * g) * LANES:(2 * g + 1) * LANES] = jnp.where(lo, v_h, 1.0).astype(BF16)
            v_out[:, (2 * g + 1) * LANES:(2 * g + 2) * LANES] = jnp.where(lo, 1.0, v_h).astype(BF16)


VT = "vt"
_PROJ = {
    "mla": (_mla_body, 12, (PADW, PADW, VT, MLA_WIDTH), 6),
    "fox": (_fox_body, 6, (PADW, PADW, VT, FOX_WIDTH), 0),
    "swa": (_swa_body, 1, (SWA_WIDTH, SWA_KV_HEADS * LANES, SWA_KV_HEADS * 2 * LANES, SWA_WIDTH), 0),
}


def _out_body(o_ref, g_ref, x_ref, w_ref, lg_ref, lb_ref):
    a = (o_ref[...].astype(F32) * g_ref[...].astype(F32)).astype(BF16)
    y = jnp.dot(a, w_ref[...], preferred_element_type=F32)
    z = DEEPNORM_ALPHA * x_ref[...] + y
    mu = jnp.mean(z, axis=-1, keepdims=True)
    zc = z - mu
    var = jnp.mean(zc * zc, axis=-1, keepdims=True)
    return zc * lax.rsqrt(var + LN_EPS) * lg_ref[...] + lb_ref[...]


def _boundary_kernel(*refs, with_out, kind):
    refs = list(refs)
    if with_out:
        out_in, refs = refs[:6], refs[6:]
    else:
        x_ref, refs = refs[0], refs[1:]
    n_ops, widths, n_tab = _PROJ[kind][1:] if kind else (0, (), 0)
    proj_in, refs = refs[:n_ops], refs[n_ops:]
    if with_out:
        y_ref, refs = refs[0], refs[1:]
    outs, scratch = refs[:len(widths)], refs[len(widths):]
    if kind == "fox":
        @pl.when(pl.program_id(1) == 0)
        def _():
            scratch[0][...] = jnp.zeros_like(scratch[0])

    tm = (out_in[2] if with_out else x_ref).shape[0]
    blocks = [pl.ds(r * SUB_ROWS, SUB_ROWS) for r in range(tm // SUB_ROWS)]
    xs = []
    for rows in blocks:
        if with_out:
            xs.append(_out_body(*(ref.at[rows] for ref in out_in[:3]), *out_in[3:]))
            y_ref[rows, :] = xs[-1]
        else:
            xs.append(x_ref[rows, :])
    for rows, x in zip(blocks, xs):
        if kind:
            _PROJ[kind][0](x, *proj_in[:n_ops - n_tab], *(t.at[rows] for t in proj_in[n_ops - n_tab:]),
                           *(o.at[:, rows] if w == VT else o.at[rows] for o, w in zip(outs, widths)), *scratch)


class _Layer:
    def __init__(self, arr, layer):
        self.arr, self.layer = arr, layer


def _boundary(x, out_ops, kind, proj_ops):
    B, S, _ = x.shape
    tm = ROW_TILE if kind else 2 * ROW_TILE
    row = lambda w: pl.BlockSpec((None, tm, w), lambda b, i: (b, i, 0))

    def full(a):
        if isinstance(a, _Layer):
            idx = (a.layer,) + (0,) * (a.arr.ndim - 1)
            return pl.BlockSpec((None,) + a.arr.shape[1:], lambda b, i: idx, pipeline_mode=pl.Buffered(1))
        return pl.BlockSpec(a.shape, lambda b, i: (0,) * a.ndim, pipeline_mode=pl.Buffered(1))

    unwrap = lambda a: a.arr if isinstance(a, _Layer) else a
    tab = pl.BlockSpec((tm, LANES), lambda b, i: (i, 0))
    in_specs, args, out_specs, out_shape, scratch = [], [], [], [], []
    if out_ops is not None:
        o, sg, w, lg, lb = out_ops
        in_specs += [row(D_MODEL), row(D_MODEL), row(D_MODEL), full(w), full(lg), full(lb)]
        args += [o, sg, x, unwrap(w), unwrap(lg), unwrap(lb)]
        out_specs.append(row(D_MODEL))
        out_shape.append(jax.ShapeDtypeStruct((B, S, D_MODEL), F32))
    else:
        in_specs.append(row(D_MODEL))
        args.append(x)
    if kind:
        _, n_ops, widths, n_tab = _PROJ[kind]
        assert len(proj_ops) == n_ops
        in_specs += [full(a) for a in proj_ops[:n_ops - n_tab]] + [tab] * n_tab
        args += [unwrap(a) for a in proj_ops]
        for w in widths:
            if w == VT:
                assert TK % tm == 0
                out_specs.append(pl.BlockSpec((None, None, PADW, tm),
                                              lambda b, i: (b, i * tm // TK, 0, i % (TK // tm))))
                out_shape.append(jax.ShapeDtypeStruct((B, S // TK, PADW, TK), BF16))
            else:
                out_specs.append(row(w))
                out_shape.append(jax.ShapeDtypeStruct((B, S, w), BF16))
        if kind == "fox":
            scratch.append(pltpu.VMEM((1, LANES), F32))
    sem = ("parallel", "arbitrary" if kind == "fox" else "parallel")
    return pl.pallas_call(
        functools.partial(_boundary_kernel, with_out=out_ops is not None, kind=kind),
        grid=(B, S // tm),
        in_specs=in_specs,
        out_specs=out_specs,
        out_shape=out_shape,
        scratch_shapes=scratch,
        compiler_params=_params(sem),
        name="boundary_" + ("out_" if out_ops is not None else "") + (kind or "final"),
    )(*args)


def _normalize_pair(ext_even, ext_odd, extra_even=None, extra_odd=None):
    lo = _lane_lt_half(ext_even.shape)
    den = jnp.where(lo, ext_odd, ext_even)
    if extra_even is not None:
        den = den + jnp.where(lo, extra_odd, extra_even)
    return jnp.where(lo, ext_even, ext_odd) * pltpu.roll(1.0 / den, HALF, 1)


def _causal_attn_kernel(q_ref, k_ref, vt_ref, o_ref, m_ref, acc_ref):
    S = q_ref.shape[0]
    heads = range(ATT_HEADS)
    lanes = [slice(h * LANES, (h + 1) * LANES) for h in heads]

    def tile(q0, kt, jobs, diagonal):
        chains = [(h,) + job for job in jobs for h in heads]
        k0 = kt * TK
        s_all = [lax.dot_general(q_ref[pl.ds(q0 + roff, nq), lanes[h]], k_ref[pl.ds(k0, nk), lanes[h]],
                                 NT_DIMS, preferred_element_type=F32) for h, roff, nq, nk in chains]
        p_all, alpha_all = [], []
        for (h, roff, nq, nk), s in zip(chains, s_all):
            st = slice(roff, roff + nq)
            blocks = [s[:, j * LANES:(j + 1) * LANES] for j in range(nk // LANES)]
            if diagonal:
                row = lax.broadcasted_iota(jnp.int32, (nq, LANES), 0) + roff
                col = lax.broadcasted_iota(jnp.int32, (nq, LANES), 1)
                blocks = [b if (j + 1) * LANES - 1 <= roff else jnp.where(col + j * LANES <= row, b, NEG_INF)
                          for j, b in enumerate(blocks)]
            m_blk = jnp.max(functools.reduce(jnp.maximum, blocks), axis=-1, keepdims=True)
            if diagonal:
                m_new = jnp.broadcast_to(m_blk, (nq, LANES))
                alpha_all.append(None)
            else:
                m_old = m_ref[h, st, :]
                m_new = jnp.maximum(m_old, m_blk)
                alpha_all.append(jnp.exp2(m_old - m_new))
            p_all.append(jnp.concatenate([jnp.exp2(b - m_new) for b in blocks], axis=1).astype(BF16))
            m_ref[h, st, :] = m_new
        for (h, roff, nq, nk), p, alpha in zip(chains, p_all, alpha_all):
            st = slice(roff, roff + nq)
            if nk <= TK:
                vt = vt_ref[kt, lanes[h], 0:nk]
            else:
                vt = jnp.concatenate([vt_ref[kt + t, lanes[h], :] for t in range(nk // TK)], axis=1)
            pv = lax.dot_general(p, vt, NT_DIMS, preferred_element_type=F32)
            acc_ref[h, st, :] = pv if alpha is None else acc_ref[h, st, :] * alpha + pv

    half = TQ // 2
    for qi in range(S // TQ):
        q0 = qi * TQ
        tile(q0, qi, [(0, half, half), (half, half, TQ)], True)
        for kj in range(0, qi - 1, 2):
            tile(q0, kj, [(0, TQ, 2 * TK)], False)
        if qi % 2:
            tile(q0, qi - 1, [(0, TQ, TK)], False)
        for p in range(ATT_HEADS // 2):
            o_ref[q0:q0 + TQ, p * LANES:(p + 1) * LANES] = _normalize_pair(
                acc_ref[2 * p], acc_ref[2 * p + 1]).astype(BF16)


def _causal_attn(q, k, vt):
    B, S, _ = q.shape
    spec = pl.BlockSpec((None, S, ATT_HEADS * LANES), lambda b, p: (b, 0, p))
    return pl.pallas_call(
        _causal_attn_kernel,
        grid=(B, HEADS // ATT_HEADS),
        in_specs=[spec, spec, pl.BlockSpec((None, S // TK, ATT_HEADS * LANES, TK), lambda b, p: (b, 0, p, 0))],
        out_specs=pl.BlockSpec((None, S, ATT_HEADS * HALF), lambda b, p: (b, 0, p)),
        out_shape=jax.ShapeDtypeStruct((B, S, HEADS * HALF), BF16),
        scratch_shapes=[pltpu.VMEM((ATT_HEADS, TQ, LANES), F32), pltpu.VMEM((ATT_HEADS, TQ, LANES), F32)],
        compiler_params=_params(("parallel", "parallel")),
        name="causal_attn",
    )(q, k, vt)


def _swa_attn_kernel(sink_ref, q_ref, kp_ref, kc_ref, vp_ref, vc_ref, bias0_ref, bias_ref, o_ref):
    nqb = q_ref.shape[0] // BLOCK
    lo = _lane_lt_half((BLOCK, LANES))
    zero = jnp.zeros((), BF16)

    def band(prev_ref, cur_ref, i, group):
        cols = slice(group * LANES, (group + 1) * LANES)
        if i == 0:
            return jnp.concatenate([prev_ref[:, cols], cur_ref[0:BLOCK, cols]], axis=0)
        return cur_ref[(i - 1) * BLOCK:(i + 1) * BLOCK, cols]

    def qk(i, g):
        rows = slice(i * BLOCK, (i + 1) * BLOCK)
        kb = band(kp_ref, kc_ref, i, g)
        qa = q_ref[rows, (2 * g) * LANES:(2 * g + 1) * LANES]
        qb = q_ref[rows, (2 * g + 1) * LANES:(2 * g + 2) * LANES]
        out = []
        for par in range(2):
            keep = lo if par == 0 else jnp.logical_not(lo)
            stack = jnp.concatenate([jnp.where(keep, qa, zero), jnp.where(keep, qb, zero)], axis=0)
            out.append(lax.dot_general(stack, kb, NT_DIMS, preferred_element_type=F32))
        return out

    def finish(i, g, s_pair):
        b_ref = bias0_ref if i == 0 else bias_ref
        ext, extra = [], []
        for par, s in enumerate(s_pair):
            ha, hb = 4 * g + par, 4 * g + 2 + par
            blocks = [s[:, j * LANES:(j + 1) * LANES]
                      + jnp.concatenate([b_ref[ha, :, j * LANES:(j + 1) * LANES],
                                         b_ref[hb, :, j * LANES:(j + 1) * LANES]], axis=0) for j in range(2)]
            sink = jnp.concatenate([jnp.full((BLOCK, LANES), sink_ref[ha], F32),
                                    jnp.full((BLOCK, LANES), sink_ref[hb], F32)], axis=0)
            m = jnp.maximum(sink, jnp.max(jnp.maximum(blocks[0], blocks[1]), axis=-1, keepdims=True))
            p = jnp.concatenate([jnp.exp2(b - m) for b in blocks], axis=1).astype(BF16)
            vb = band(vp_ref, vc_ref, i, 2 * g + par)
            ext.append(jnp.dot(p, vb, preferred_element_type=F32))
            extra.append(jnp.exp2(sink - m))
        for j in range(2):
            rows = slice(j * BLOCK, (j + 1) * BLOCK)
            o_ref[i * BLOCK:(i + 1) * BLOCK, (2 * g + j) * LANES:(2 * g + j + 1) * LANES] = _normalize_pair(
                ext[0][rows], ext[1][rows], extra[0][rows], extra[1][rows]).astype(BF16)

    units = [(i, g) for i in range(nqb) for g in range(SWA_KV_HEADS)]
    s_cur = qk(*units[0])
    for u, (i, g) in enumerate(units):
        s_next = qk(*units[u + 1]) if u + 1 < len(units) else None
        finish(i, g, s_cur)
        s_cur = s_next


def _swa_attn(sinks2, q, k, v, bias):
    B, S, _ = q.shape
    rows = SWA_QB * BLOCK
    cur = lambda w: pl.BlockSpec((None, rows, w), lambda b, n: (b, n, 0))
    prev = lambda w: pl.BlockSpec((None, BLOCK, w), lambda b, n: (b, jnp.maximum(n * SWA_QB - 1, 0), 0))
    bias_shape = (None, SWA_Q_HEADS, BLOCK, 2 * BLOCK)
    kw, vw = k.shape[-1], v.shape[-1]
    return pl.pallas_call(
        _swa_attn_kernel,
        grid=(B, S // rows),
        in_specs=[pl.BlockSpec(memory_space=pltpu.SMEM), cur(SWA_WIDTH), prev(kw), cur(kw), prev(vw), cur(vw),
                  pl.BlockSpec(bias_shape, lambda b, n: (jnp.minimum(n, 1), 0, 0, 0)),
                  pl.BlockSpec(bias_shape, lambda b, n: (1, 0, 0, 0))],
        out_specs=cur(SWA_WIDTH),
        out_shape=jax.ShapeDtypeStruct((B, S, SWA_WIDTH), BF16),
        compiler_params=_params(("parallel", "parallel")),
        name="swa_attn",
    )(sinks2, q, k, k, v, v, bias, bias)


def _rot(w):
    half = w.shape[-1] // 2
    return jnp.concatenate([-w[..., half:], w[..., :half]], axis=-1)


def _rope_tables(seq, c0):
    inv = ROPE_THETA ** (-np.arange(0, MLA_ROPE_DIM, 2, dtype=np.float64) / MLA_ROPE_DIM)
    ang = np.arange(seq, dtype=np.float64)[:, None] * inv[None, :]
    cos = np.concatenate([np.cos(ang)] * 2, axis=-1)
    sin = np.concatenate([np.sin(ang)] * 2, axis=-1)
    z32 = np.zeros((seq, LANES - HALF - MLA_ROPE_DIM))
    z64 = np.zeros((seq, HALF))
    c64 = np.full((seq, HALF), c0)
    aq_even = np.concatenate([c64, c0 * cos, z32], axis=-1)
    bq_even = np.concatenate([z64, c0 * sin, z32], axis=-1)
    aq_odd = np.concatenate([c0 * cos, z32, c64], axis=-1)
    bq_odd = np.concatenate([c0 * sin, z32, z64], axis=-1)
    ak = np.concatenate([z64, cos, z32], axis=-1)
    bk = np.concatenate([z64, sin, z32], axis=-1)
    return tuple(t.astype(np.float32) for t in (aq_even, bq_even, aq_odd, bq_odd, ak, bk))


def _mla_operands(w_in, q_norm, w_q_up, kv_norm, w_kv_up):
    L = w_in.shape[0]
    c2 = MLA_Q_RANK + MLA_KV_RANK
    c3 = c2 + MLA_ROPE_DIM
    kr = w_in[..., c2:c3]
    krg = jnp.concatenate([jnp.zeros((L, D_MODEL, HALF), F32), kr, _rot(kr)], axis=-1)
    win = jnp.concatenate([w_in[..., :c2], krg, w_in[..., c3:]], axis=-1).astype(BF16)
    odd = (jnp.arange(MLA_HEADS) % 2 == 1)[:, None]
    wq = w_q_up.reshape(L, MLA_Q_RANK, MLA_HEADS, MLA_NOPE_DIM + MLA_ROPE_DIM)
    nope, rope = wq[..., :MLA_NOPE_DIM], wq[..., MLA_NOPE_DIM:]
    wq = jnp.where(odd, jnp.concatenate([rope, _rot(rope), nope], axis=-1),
                   jnp.concatenate([nope, rope, _rot(rope)], axis=-1))
    wq = wq.reshape(L, MLA_Q_RANK, PADW).astype(BF16)
    wkv = w_kv_up.reshape(L, MLA_KV_RANK, MLA_HEADS, MLA_NOPE_DIM + MLA_V_DIM)
    wk = wkv[..., :MLA_NOPE_DIM].reshape(L, MLA_KV_RANK, MLA_HEADS * MLA_NOPE_DIM).astype(BF16)
    wvt = jnp.swapaxes(wkv[..., MLA_NOPE_DIM:].reshape(L, MLA_KV_RANK, MLA_WIDTH), 1, 2).astype(BF16)
    return win, q_norm.reshape(L, 1, -1), kv_norm.reshape(L, 1, -1), wq, wk, wvt


def _fox_layer_operands(w_in, b_f):
    W = FOX_WIDTH
    wf = jnp.concatenate([w_in[:, 3 * W:3 * W + FOX_HEADS], jnp.zeros((D_MODEL, LANES - FOX_HEADS), F32)], axis=-1)
    win = jnp.concatenate([w_in[:, :2 * W], w_in[:, 3 * W + FOX_HEADS:], wf], axis=-1).astype(BF16)
    wvt = w_in[:, 2 * W:3 * W].T.astype(BF16)
    bf = jnp.concatenate([b_f, jnp.zeros((LANES - FOX_HEADS,), F32)]).reshape(1, LANES)
    h = np.arange(FOX_HEADS)
    base = (h // 2) * LANES + np.where(h % 2 == 0, HALF, 0)
    scat = np.zeros((LANES, 2 * W), np.float32)
    onesq = np.zeros((1, W), np.float32)
    onesk = np.zeros((1, W), np.float32)
    for j in range(3):
        scat[j * FOX_HEADS + h, base + j] = 1.0
        scat[j * FOX_HEADS + h, W + base + 3 + j] = -1.0
        onesq[0, base + 3 + j] = 1.0
        onesk[0, base + j] = 1.0
    return win, wvt, bf, jnp.asarray(scat, dtype=BF16), onesq, onesk


def _swa_bias():
    qi = np.arange(BLOCK)[:, None]
    kj = np.arange(2 * BLOCK)[None, :]
    dist = qi + BLOCK - kj
    valid = (dist >= 0) & (dist < WINDOW)
    slopes = 2.0 ** (-8.0 * np.arange(1, SWA_Q_HEADS + 1, dtype=np.float64) / SWA_Q_HEADS)
    ali = -(slopes[:, None, None] * dist.astype(np.float64)[None]) * LOG2E
    bias_rest = np.where(valid[None], ali, NEG_INF)
    bias_first = np.where((valid & (kj >= BLOCK))[None], ali, NEG_INF)
    return np.stack([bias_first, bias_rest]).astype(np.float32)


def kernel(x, ln_g, ln_b, mla_w_in, mla_q_norm, mla_w_q_up, mla_kv_norm, mla_w_kv_up, mla_w_out,
           swa_w_in, swa_sinks, swa_w_out, fox_w_in, fox_b_f, fox_w_out):
    seq = x.shape[1]
    mla_ops = _mla_operands(mla_w_in, mla_q_norm, mla_w_q_up, mla_kv_norm, mla_w_kv_up)
    mla_tables = _rope_tables(seq, (MLA_NOPE_DIM + MLA_ROPE_DIM) ** -0.5 * LOG2E)
    w_outs = {"mla": mla_w_out.astype(BF16), "swa": swa_w_out.astype(BF16), "fox": fox_w_out.astype(BF16)}
    ln_g3, ln_b3 = ln_g.reshape(DEPTH, 1, D_MODEL), ln_b.reshape(DEPTH, 1, D_MODEL)
    out_ops = None
    for i in range(DEPTH + 1):
        j = i // N_MIXERS
        kind = ("mla", "swa", "fox")[i % N_MIXERS] if i < DEPTH else None
        if kind == "mla":
            proj_ops = tuple(_Layer(a, j) for a in mla_ops) + mla_tables
        elif kind == "swa":
            proj_ops = (swa_w_in[j].astype(BF16),)
        elif kind == "fox":
            proj_ops = _fox_layer_operands(fox_w_in[j], fox_b_f[j])
        else:
            proj_ops = ()
        res = _boundary(x, out_ops, kind, proj_ops)
        if out_ops is not None:
            x, res = res[0], res[1:]
        if kind is None:
            return x
        q, k, v, sg = res
        if kind == "swa":
            o = _swa_attn(swa_sinks[j].astype(F32) * LOG2E, q, k, v, _swa_bias())
        else:
            o = _causal_attn(q, k, v)
        out_ops = (o, sg, _Layer(w_outs[kind], j), _Layer(ln_g3, i), _Layer(ln_b3, i))
```

```python
import functools
import math

import jax
import jax.numpy as jnp
import numpy as np
from jax import lax
from jax.experimental import pallas as pl
from jax.experimental.pallas import tpu as pltpu

D_MODEL = 1024
DEPTH = 4
N_MIXERS = 3
BLOCK = 128
NEG_INF = -1e30
RMS_EPS = 1e-6
LN_EPS = 1e-5
DEEPNORM_ALPHA = (2 * DEPTH) ** 0.25
LOG2E = math.log2(math.e)

MLA_HEADS = 16
MLA_NOPE_DIM = 64
MLA_ROPE_DIM = 32
MLA_V_DIM = 64
MLA_Q_RANK = 384
MLA_KV_RANK = 256
ROPE_THETA = 10000.0
MLA_WIDTH = MLA_HEADS * MLA_V_DIM

SWA_Q_HEADS = 16
SWA_KV_HEADS = 4
SWA_HEAD_DIM = 64
WINDOW = 128
SWA_WIDTH = SWA_Q_HEADS * SWA_HEAD_DIM
SWA_KV_WIDTH = SWA_KV_HEADS * SWA_HEAD_DIM

FOX_HEADS = 16
FOX_HEAD_DIM = 64
FOX_WIDTH = FOX_HEADS * FOX_HEAD_DIM

LANES = 128
HALF = LANES // 2
HEADS = 16
PADW = HEADS * LANES

ROW_TILE = 512
SUB_ROWS = 256
TQ = 512
TK = 512
ATT_HEADS = 4
SWA_QB = 4
VMEM_LIMIT = 56 * 1024 * 1024

F32 = jnp.float32
BF16 = jnp.bfloat16
NT_DIMS = (((1,), (1,)), ((), ()))


def _params(sem):
    return pltpu.CompilerParams(dimension_semantics=sem, vmem_limit_bytes=VMEM_LIMIT)


def _lane_lt_half(shape):
    return lax.broadcasted_iota(jnp.int32, shape, len(shape) - 1) < HALF


def _silu(g):
    return g / (1.0 + jnp.exp(-g))


def _write_vt(vt, vt_out):
    top = lax.broadcasted_iota(jnp.int32, (LANES, vt.shape[1]), 0) < HALF
    for p in range(HEADS // 2):
        blk = vt[p * LANES:(p + 1) * LANES, :]
        vt_out[(2 * p) * LANES:(2 * p + 1) * LANES, :] = jnp.where(top, blk, 1.0).astype(BF16)
        vt_out[(2 * p + 1) * LANES:(2 * p + 2) * LANES, :] = jnp.where(top, 1.0, blk).astype(BF16)


def _mla_body(x, win_ref, qg_ref, kg_ref, wq_ref, wk_ref, wvt_ref, aqe_ref, bqe_ref, aqo_ref, bqo_ref,
              ak_ref, bk_ref, q_out, k_out, vt_out, g_out):
    h = jnp.dot(x.astype(BF16), win_ref[...], preferred_element_type=F32)
    c1 = MLA_Q_RANK
    c2 = c1 + MLA_KV_RANK
    c3 = c2 + LANES
    cq, ckv, krg, gate = h[:, :c1], h[:, c1:c2], h[:, c2:c3], h[:, c3:]
    g_out[...] = _silu(gate).astype(BF16)

    qn = cq * lax.rsqrt(jnp.mean(cq * cq, axis=-1, keepdims=True) + RMS_EPS) * qg_ref[...]
    kn = (ckv * lax.rsqrt(jnp.mean(ckv * ckv, axis=-1, keepdims=True) + RMS_EPS) * kg_ref[...]).astype(BF16)
    q = jnp.dot(qn.astype(BF16), wq_ref[...], preferred_element_type=F32)
    kc = jnp.dot(kn, wk_ref[...], preferred_element_type=F32)
    _write_vt(lax.dot_general(wvt_ref[...], kn, NT_DIMS, preferred_element_type=F32), vt_out)

    shift = LANES - MLA_ROPE_DIM
    aq = (aqe_ref[...], aqo_ref[...])
    bq = (bqe_ref[...], bqo_ref[...])
    kr_hi = krg * ak_ref[...] + pltpu.roll(krg, shift, 1) * bk_ref[...]
    kr_lo = pltpu.roll(kr_hi, HALF, 1)
    lo = _lane_lt_half((x.shape[0], LANES))
    for p in range(HEADS // 2):
        e = slice((2 * p) * LANES, (2 * p + 1) * LANES)
        o = slice((2 * p + 1) * LANES, (2 * p + 2) * LANES)
        kblk = kc[:, p * LANES:(p + 1) * LANES]
        q_out[:, e] = (q[:, e] * aq[0] + pltpu.roll(q[:, e], shift, 1) * bq[0]).astype(BF16)
        q_out[:, o] = (q[:, o] * aq[1] + pltpu.roll(q[:, o], shift, 1) * bq[1]).astype(BF16)
        k_out[:, e] = jnp.where(lo, kblk, kr_hi).astype(BF16)
        k_out[:, o] = jnp.where(lo, kr_lo, kblk).astype(BF16)


def _split_pairs(nat, aug, out_ref):
    lo = _lane_lt_half((nat.shape[0], LANES))
    for p in range(HEADS // 2):
        blk = nat[:, p * LANES:(p + 1) * LANES]
        fill = aug[:, p * LANES:(p + 1) * LANES]
        out_ref[:, (2 * p) * LANES:(2 * p + 1) * LANES] = jnp.where(lo, blk, fill).astype(BF16)
        out_ref[:, (2 * p + 1) * LANES:(2 * p + 2) * LANES] = jnp.where(lo, fill, blk).astype(BF16)


def _fox_body(x, win_ref, wvt_ref, bf_ref, scat_ref, onesq_ref, onesk_ref, q_out, k_out, vt_out, g_out, carry_ref):
    tm = x.shape[0]
    xb = x.astype(BF16)
    h = jnp.dot(xb, win_ref[...], preferred_element_type=F32)
    W = FOX_WIDTH
    g_out[...] = _silu(h[:, 2 * W:3 * W]).astype(BF16)
    _write_vt(lax.dot_general(wvt_ref[...], xb, NT_DIMS, preferred_element_type=F32), vt_out)

    f = h[:, 3 * W:3 * W + LANES] + bf_ref[...]
    logf = jnp.minimum(f, 0.0) - jnp.log(1.0 + jnp.exp(-jnp.abs(f)))

    def split3(a):
        hi = a.astype(BF16)
        r = a - hi.astype(F32)
        mid = r.astype(BF16)
        lo = (r - mid.astype(F32)).astype(BF16)
        return hi, mid, lo

    ri = lax.broadcasted_iota(jnp.int32, (tm, tm), 0)
    ci = lax.broadcasted_iota(jnp.int32, (tm, tm), 1)
    tri = (ci <= ri).astype(BF16)
    l_hi, l_mid, l_lo = split3(logf)
    cum = (jnp.dot(tri, l_hi, preferred_element_type=F32) + jnp.dot(tri, l_mid, preferred_element_type=F32)
           + jnp.dot(tri, l_lo, preferred_element_type=F32)) + carry_ref[...]
    carry_ref[...] = cum[tm - 1:tm, :]

    c_hi, c_mid, c_lo = split3(cum * LOG2E)
    lane = lax.broadcasted_iota(jnp.int32, (tm, LANES), 1)
    packed = jnp.where(lane < HEADS, c_hi.astype(F32),
                       jnp.where(lane < 2 * HEADS, pltpu.roll(c_mid.astype(F32), HEADS, 1),
                                 pltpu.roll(c_lo.astype(F32), 2 * HEADS, 1)))
    packed = jnp.where(lane < 3 * HEADS, packed, 0.0).astype(BF16)
    aug = jnp.dot(packed, scat_ref[...], preferred_element_type=F32)
    augq = aug[:, :W] + onesq_ref[...]
    augk = aug[:, W:] + onesk_ref[...]

    _split_pairs(h[:, :W] * (FOX_HEAD_DIM ** -0.5 * LOG2E), augq, q_out)
    _split_pairs(h[:, W:2 * W], augk, k_out)


def _swa_body(x, win_ref, q_out, k_out, v_out, g_out):
    h = jnp.dot(x.astype(BF16), win_ref[...], preferred_element_type=F32)
    c1 = SWA_WIDTH
    c2 = c1 + SWA_KV_WIDTH
    c3 = c2 + SWA_KV_WIDTH
    q_out[...] = (h[:, :c1] * (SWA_HEAD_DIM ** -0.5 * LOG2E)).astype(BF16)
    g_out[...] = _silu(h[:, c3:]).astype(BF16)
    lo = _lane_lt_half((x.shape[0], LANES))
    for pair in range(SWA_KV_HEADS // 2):
        kp = h[:, c1 + pair * LANES:c1 + (pair + 1) * LANES]
        vp = h[:, c2 + pair * LANES:c2 + (pair + 1) * LANES]
        ks, vs = pltpu.roll(kp, HALF, 1), pltpu.roll(vp, HALF, 1)
        for t, (k_h, v_h) in enumerate(((jnp.where(lo, kp, ks), jnp.where(lo, vp, vs)),
                                        (jnp.where(lo, ks, kp), jnp.where(lo, vs, vp)))):
            g = 2 * pair + t
            k_out[:, g * LANES:(g + 1) * LANES] = k_h.astype(BF16)
            v_out[:, (2 * g) * LANES:(2 * g + 1) * LANES] = jnp.where(lo, v_h, 1.0).astype(BF16)
            v_out[:, (2 * g + 1) * LANES:(2 * g + 2) * LANES] = jnp.where(lo, 1.0, v_h).astype(BF16)


VT = "vt"
_PROJ = {
    "mla": (_mla_body, 12, (PADW, PADW, VT, MLA_WIDTH), 6),
    "fox": (_fox_body, 6, (PADW, PADW, VT, FOX_WIDTH), 0),
    "swa": (_swa_body, 1, (SWA_WIDTH, SWA_KV_HEADS * LANES, SWA_KV_HEADS * 2 * LANES, SWA_WIDTH), 0),
}


def _out_body(o_ref, g_ref, x_ref, w_ref, lg_ref, lb_ref):
    a = (o_ref[...].astype(F32) * g_ref[...].astype(F32)).astype(BF16)
    y = jnp.dot(a, w_ref[...], preferred_element_type=F32)
    z = DEEPNORM_ALPHA * x_ref[...] + y
    mu = jnp.mean(z, axis=-1, keepdims=True)
    zc = z - mu
    var = jnp.mean(zc * zc, axis=-1, keepdims=True)
    return zc * lax.rsqrt(var + LN_EPS) * lg_ref[...] + lb_ref[...]


def _boundary_kernel(*refs, with_out, kind):
    refs = list(refs)
    if with_out:
        out_in, refs = refs[:6], refs[6:]
    else:
        x_ref, refs = refs[0], refs[1:]
    n_ops, widths, n_tab = _PROJ[kind][1:] if kind else (0, (), 0)
    proj_in, refs = refs[:n_ops], refs[n_ops:]
    if with_out:
        y_ref, refs = refs[0], refs[1:]
    outs, scratch = refs[:len(widths)], refs[len(widths):]
    if kind == "fox":
        @pl.when(pl.program_id(1) == 0)
        def _():
            scratch[0][...] = jnp.zeros_like(scratch[0])

    tm = (out_in[2] if with_out else x_ref).shape[0]
    blocks = [pl.ds(r * SUB_ROWS, SUB_ROWS) for r in range(tm // SUB_ROWS)]
    xs = []
    for rows in blocks:
        if with_out:
            xs.append(_out_body(*(ref.at[rows] for ref in out_in[:3]), *out_in[3:]))
            y_ref[rows, :] = xs[-1]
        else:
            xs.append(x_ref[rows, :])
    for rows, x in zip(blocks, xs):
        if kind:
            _PROJ[kind][0](x, *proj_in[:n_ops - n_tab], *(t.at[rows] for t in proj_in[n_ops - n_tab:]),
                           *(o.at[:, rows] if w == VT else o.at[rows] for o, w in zip(outs, widths)), *scratch)


class _Layer:
    def __init__(self, arr, layer):
        self.arr, self.layer = arr, layer


def _boundary(x, out_ops, kind, proj_ops):
    B, S, _ = x.shape
    tm = ROW_TILE if kind else 2 * ROW_TILE
    row = lambda w: pl.BlockSpec((None, tm, w), lambda b, i: (b, i, 0))

    def full(a):
        if isinstance(a, _Layer):
            idx = (a.layer,) + (0,) * (a.arr.ndim - 1)
            return pl.BlockSpec((None,) + a.arr.shape[1:], lambda b, i: idx, pipeline_mode=pl.Buffered(1))
        return pl.BlockSpec(a.shape, lambda b, i: (0,) * a.ndim, pipeline_mode=pl.Buffered(1))

    unwrap = lambda a: a.arr if isinstance(a, _Layer) else a
    tab = pl.BlockSpec((tm, LANES), lambda b, i: (i, 0))
    in_specs, args, out_specs, out_shape, scratch = [], [], [], [], []
    if out_ops is not None:
        o, sg, w, lg, lb = out_ops
        in_specs += [row(D_MODEL), row(D_MODEL), row(D_MODEL), full(w), full(lg), full(lb)]
        args += [o, sg, x, unwrap(w), unwrap(lg), unwrap(lb)]
        out_specs.append(row(D_MODEL))
        out_shape.append(jax.ShapeDtypeStruct((B, S, D_MODEL), F32))
    else:
        in_specs.append(row(D_MODEL))
        args.append(x)
    if kind:
        _, n_ops, widths, n_tab = _PROJ[kind]
        assert len(proj_ops) == n_ops
        in_specs += [full(a) for a in proj_ops[:n_ops - n_tab]] + [tab] * n_tab
        args += [unwrap(a) for a in proj_ops]
        for w in widths:
            if w == VT:
                assert TK % tm == 0
                out_specs.append(pl.BlockSpec((None, None, PADW, tm),
                                              lambda b, i: (b, i * tm // TK, 0, i % (TK // tm))))
                out_shape.append(jax.ShapeDtypeStruct((B, S // TK, PADW, TK), BF16))
            else:
                out_specs.append(row(w))
                out_shape.append(jax.ShapeDtypeStruct((B, S, w), BF16))
        if kind == "fox":
            scratch.append(pltpu.VMEM((1, LANES), F32))
    sem = ("parallel", "arbitrary" if kind == "fox" else "parallel")
    return pl.pallas_call(
        functools.partial(_boundary_kernel, with_out=out_ops is not None, kind=kind),
        grid=(B, S // tm),
        in_specs=in_specs,
        out_specs=out_specs,
        out_shape=out_shape,
        scratch_shapes=scratch,
        compiler_params=_params(sem),
        name="boundary_" + ("out_" if out_ops is not None else "") + (kind or "final"),
    )(*args)


def _normalize_pair(ext_even, ext_odd, extra_even=None, extra_odd=None):
    lo = _lane_lt_half(ext_even.shape)
    den = jnp.where(lo, ext_odd, ext_even)
    if extra_even is not None:
        den = den + jnp.where(lo, extra_odd, extra_even)
    return jnp.where(lo, ext_even, ext_odd) * pltpu.roll(1.0 / den, HALF, 1)


def _causal_attn_kernel(q_ref, k_ref, vt_ref, o_ref, m_ref, acc_ref):
    S = q_ref.shape[0]
    heads = range(ATT_HEADS)
    lanes = [slice(h * LANES, (h + 1) * LANES) for h in heads]

    def tile(q0, kt, jobs, diagonal):
        chains = [(h,) + job for job in jobs for h in heads]
        k0 = kt * TK
        s_all = [lax.dot_general(q_ref[pl.ds(q0 + roff, nq), lanes[h]], k_ref[pl.ds(k0, nk), lanes[h]],
                                 NT_DIMS, preferred_element_type=F32) for h, roff, nq, nk in chains]
        p_all, alpha_all = [], []
        for (h, roff, nq, nk), s in zip(chains, s_all):
            st = slice(roff, roff + nq)
            blocks = [s[:, j * LANES:(j + 1) * LANES] for j in range(nk // LANES)]
            if diagonal:
                row = lax.broadcasted_iota(jnp.int32, (nq, LANES), 0) + roff
                col = lax.broadcasted_iota(jnp.int32, (nq, LANES), 1)
                blocks = [b if (j + 1) * LANES - 1 <= roff else jnp.where(col + j * LANES <= row, b, NEG_INF)
                          for j, b in enumerate(blocks)]
            m_blk = jnp.max(functools.reduce(jnp.maximum, blocks), axis=-1, keepdims=True)
            if diagonal:
                m_new = jnp.broadcast_to(m_blk, (nq, LANES))
                alpha_all.append(None)
            else:
                m_old = m_ref[h, st, :]
                m_new = jnp.maximum(m_old, m_blk)
                alpha_all.append(jnp.exp2(m_old - m_new))
            p_all.append(jnp.concatenate([jnp.exp2(b - m_new) for b in blocks], axis=1).astype(BF16))
            m_ref[h, st, :] = m_new
        for (h, roff, nq, nk), p, alpha in zip(chains, p_all, alpha_all):
            st = slice(roff, roff + nq)
            if nk <= TK:
                vt = vt_ref[kt, lanes[h], 0:nk]
            else:
                vt = jnp.concatenate([vt_ref[kt + t, lanes[h], :] for t in range(nk // TK)], axis=1)
            pv = lax.dot_general(p, vt, NT_DIMS, preferred_element_type=F32)
            acc_ref[h, st, :] = pv if alpha is None else acc_ref[h, st, :] * alpha + pv

    half = TQ // 2
    for qi in range(S // TQ):
        q0 = qi * TQ
        tile(q0, qi, [(0, half, half), (half, half, TQ)], True)
        if qi:
            tile(q0, 0, [(0, TQ, qi * TK)], False)
        for p in range(ATT_HEADS // 2):
            o_ref[q0:q0 + TQ, p * LANES:(p + 1) * LANES] = _normalize_pair(
                acc_ref[2 * p], acc_ref[2 * p + 1]).astype(BF16)


def _causal_attn(q, k, vt):
    B, S, _ = q.shape
    spec = pl.BlockSpec((None, S, ATT_HEADS * LANES), lambda b, p: (b, 0, p))
    return pl.pallas_call(
        _causal_attn_kernel,
        grid=(B, HEADS // ATT_HEADS),
        in_specs=[spec, spec, pl.BlockSpec((None, S // TK, ATT_HEADS * LANES, TK), lambda b, p: (b, 0, p, 0))],
        out_specs=pl.BlockSpec((None, S, ATT_HEADS * HALF), lambda b, p: (b, 0, p)),
        out_shape=jax.ShapeDtypeStruct((B, S, HEADS * HALF), BF16),
        scratch_shapes=[pltpu.VMEM((ATT_HEADS, TQ, LANES), F32), pltpu.VMEM((ATT_HEADS, TQ, LANES), F32)],
        compiler_params=_params(("parallel", "parallel")),
        name="causal_attn",
    )(q, k, vt)


def _swa_attn_kernel(sink_ref, q_ref, kp_ref, kc_ref, vp_ref, vc_ref, bias0_ref, bias_ref, o_ref):
    nqb = q_ref.shape[0] // BLOCK
    lo = _lane_lt_half((BLOCK, LANES))
    zero = jnp.zeros((), BF16)

    def band(prev_ref, cur_ref, i, group):
        cols = slice(group * LANES, (group + 1) * LANES)
        if i == 0:
            return jnp.concatenate([prev_ref[:, cols], cur_ref[0:BLOCK, cols]], axis=0)
        return cur_ref[(i - 1) * BLOCK:(i + 1) * BLOCK, cols]

    def qk(i, g):
        rows = slice(i * BLOCK, (i + 1) * BLOCK)
        kb = band(kp_ref, kc_ref, i, g)
        qa = q_ref[rows, (2 * g) * LANES:(2 * g + 1) * LANES]
        qb = q_ref[rows, (2 * g + 1) * LANES:(2 * g + 2) * LANES]
        out = []
        for par in range(2):
            keep = lo if par == 0 else jnp.logical_not(lo)
            stack = jnp.concatenate([jnp.where(keep, qa, zero), jnp.where(keep, qb, zero)], axis=0)
            out.append(lax.dot_general(stack, kb, NT_DIMS, preferred_element_type=F32))
        return out

    def finish(i, g, s_pair):
        b_ref = bias0_ref if i == 0 else bias_ref
        ext, extra = [], []
        for par, s in enumerate(s_pair):
            ha, hb = 4 * g + par, 4 * g + 2 + par
            blocks = [s[:, j * LANES:(j + 1) * LANES]
                      + jnp.concatenate([b_ref[ha, :, j * LANES:(j + 1) * LANES],
                                         b_ref[hb, :, j * LANES:(j + 1) * LANES]], axis=0) for j in range(2)]
            sink = jnp.concatenate([jnp.full((BLOCK, LANES), sink_ref[ha], F32),
                                    jnp.full((BLOCK, LANES), sink_ref[hb], F32)], axis=0)
            m = jnp.maximum(sink, jnp.max(jnp.maximum(blocks[0], blocks[1]), axis=-1, keepdims=True))
            p = jnp.concatenate([jnp.exp2(b - m) for b in blocks], axis=1).astype(BF16)
            vb = band(vp_ref, vc_ref, i, 2 * g + par)
            ext.append(jnp.dot(p, vb, preferred_element_type=F32))
            extra.append(jnp.exp2(sink - m))
        for j in range(2):
            rows = slice(j * BLOCK, (j + 1) * BLOCK)
            o_ref[i * BLOCK:(i + 1) * BLOCK, (2 * g + j) * LANES:(2 * g + j + 1) * LANES] = _normalize_pair(
                ext[0][rows], ext[1][rows], extra[0][rows], extra[1][rows]).astype(BF16)

    units = [(i, g) for i in range(nqb) for g in range(SWA_KV_HEADS)]
    s_cur = qk(*units[0])
    for u, (i, g) in enumerate(units):
        s_next = qk(*units[u + 1]) if u + 1 < len(units) else None
        finish(i, g, s_cur)
        s_cur = s_next


def _swa_attn(sinks2, q, k, v, bias):
    B, S, _ = q.shape
    rows = SWA_QB * BLOCK
    cur = lambda w: pl.BlockSpec((None, rows, w), lambda b, n: (b, n, 0))
    prev = lambda w: pl.BlockSpec((None, BLOCK, w), lambda b, n: (b, jnp.maximum(n * SWA_QB - 1, 0), 0))
    bias_shape = (None, SWA_Q_HEADS, BLOCK, 2 * BLOCK)
    kw, vw = k.shape[-1], v.shape[-1]
    return pl.pallas_call(
        _swa_attn_kernel,
        grid=(B, S // rows),
        in_specs=[pl.BlockSpec(memory_space=pltpu.SMEM), cur(SWA_WIDTH), prev(kw), cur(kw), prev(vw), cur(vw),
                  pl.BlockSpec(bias_shape, lambda b, n: (jnp.minimum(n, 1), 0, 0, 0)),
                  pl.BlockSpec(bias_shape, lambda b, n: (1, 0, 0, 0))],
        out_specs=cur(SWA_WIDTH),
        out_shape=jax.ShapeDtypeStruct((B, S, SWA_WIDTH), BF16),
        compiler_params=_params(("parallel", "parallel")),
        name="swa_attn",
    )(sinks2, q, k, k, v, v, bias, bias)


def _rot(w):
    half = w.shape[-1] // 2
    return jnp.concatenate([-w[..., half:], w[..., :half]], axis=-1)


def _rope_tables(seq, c0):
    inv = ROPE_THETA ** (-np.arange(0, MLA_ROPE_DIM, 2, dtype=np.float64) / MLA_ROPE_DIM)
    ang = np.arange(seq, dtype=np.float64)[:, None] * inv[None, :]
    cos = np.concatenate([np.cos(ang)] * 2, axis=-1)
    sin = np.concatenate([np.sin(ang)] * 2, axis=-1)
    z32 = np.zeros((seq, LANES - HALF - MLA_ROPE_DIM))
    z64 = np.zeros((seq, HALF))
    c64 = np.full((seq, HALF), c0)
    aq_even = np.concatenate([c64, c0 * cos, z32], axis=-1)
    bq_even = np.concatenate([z64, c0 * sin, z32], axis=-1)
    aq_odd = np.concatenate([c0 * cos, z32, c64], axis=-1)
    bq_odd = np.concatenate([c0 * sin, z32, z64], axis=-1)
    ak = np.concatenate([z64, cos, z32], axis=-1)
    bk = np.concatenate([z64, sin, z32], axis=-1)
    return tuple(t.astype(np.float32) for t in (aq_even, bq_even, aq_odd, bq_odd, ak, bk))


def _mla_operands(w_in, q_norm, w_q_up, kv_norm, w_kv_up):
    L = w_in.shape[0]
    c2 = MLA_Q_RANK + MLA_KV_RANK
    c3 = c2 + MLA_ROPE_DIM
    kr = w_in[..., c2:c3]
    krg = jnp.concatenate([jnp.zeros((L, D_MODEL, HALF), F32), kr, _rot(kr)], axis=-1)
    win = jnp.concatenate([w_in[..., :c2], krg, w_in[..., c3:]], axis=-1).astype(BF16)
    odd = (jnp.arange(MLA_HEADS) % 2 == 1)[:, None]
    wq = w_q_up.reshape(L, MLA_Q_RANK, MLA_HEADS, MLA_NOPE_DIM + MLA_ROPE_DIM)
    nope, rope = wq[..., :MLA_NOPE_DIM], wq[..., MLA_NOPE_DIM:]
    wq = jnp.where(odd, jnp.concatenate([rope, _rot(rope), nope], axis=-1),
                   jnp.concatenate([nope, rope, _rot(rope)], axis=-1))
    wq = wq.reshape(L, MLA_Q_RANK, PADW).astype(BF16)
    wkv = w_kv_up.reshape(L, MLA_KV_RANK, MLA_HEADS, MLA_NOPE_DIM + MLA_V_DIM)
    wk = wkv[..., :MLA_NOPE_DIM].reshape(L, MLA_KV_RANK, MLA_HEADS * MLA_NOPE_DIM).astype(BF16)
    wvt = jnp.swapaxes(wkv[..., MLA_NOPE_DIM:].reshape(L, MLA_KV_RANK, MLA_WIDTH), 1, 2).astype(BF16)
    return win, q_norm.reshape(L, 1, -1), kv_norm.reshape(L, 1, -1), wq, wk, wvt


def _fox_layer_operands(w_in, b_f):
    W = FOX_WIDTH
    wf = jnp.concatenate([w_in[:, 3 * W:3 * W + FOX_HEADS], jnp.zeros((D_MODEL, LANES - FOX_HEADS), F32)], axis=-1)
    win = jnp.concatenate([w_in[:, :2 * W], w_in[:, 3 * W + FOX_HEADS:], wf], axis=-1).astype(BF16)
    wvt = w_in[:, 2 * W:3 * W].T.astype(BF16)
    bf = jnp.concatenate([b_f, jnp.zeros((LANES - FOX_HEADS,), F32)]).reshape(1, LANES)
    h = np.arange(FOX_HEADS)
    base = (h // 2) * LANES + np.where(h % 2 == 0, HALF, 0)
    scat = np.zeros((LANES, 2 * W), np.float32)
    onesq = np.zeros((1, W), np.float32)
    onesk = np.zeros((1, W), np.float32)
    for j in range(3):
        scat[j * FOX_HEADS + h, base + j] = 1.0
        scat[j * FOX_HEADS + h, W + base + 3 + j] = -1.0
        onesq[0, base + 3 + j] = 1.0
        onesk[0, base + j] = 1.0
    return win, wvt, bf, jnp.asarray(scat, dtype=BF16), onesq, onesk


def _swa_bias():
    qi = np.arange(BLOCK)[:, None]
    kj = np.arange(2 * BLOCK)[None, :]
    dist = qi + BLOCK - kj
    valid = (dist >= 0) & (dist < WINDOW)
    slopes = 2.0 ** (-8.0 * np.arange(1, SWA_Q_HEADS + 1, dtype=np.float64) / SWA_Q_HEADS)
    ali = -(slopes[:, None, None] * dist.astype(np.float64)[None]) * LOG2E
    bias_rest = np.where(valid[None], ali, NEG_INF)
    bias_first = np.where((valid & (kj >= BLOCK))[None], ali, NEG_INF)
    return np.stack([bias_first, bias_rest]).astype(np.float32)


def kernel(x, ln_g, ln_b, mla_w_in, mla_q_norm, mla_w_q_up, mla_kv_norm, mla_w_kv_up, mla_w_out,
           swa_w_in, swa_sinks, swa_w_out, fox_w_in, fox_b_f, fox_w_out):
    seq = x.shape[1]
    mla_ops = _mla_operands(mla_w_in, mla_q_norm, mla_w_q_up, mla_kv_norm, mla_w_kv_up)
    mla_tables = _rope_tables(seq, (MLA_NOPE_DIM + MLA_ROPE_DIM) ** -0.5 * LOG2E)
    w_outs = {"mla": mla_w_out.astype(BF16), "swa": swa_w_out.astype(BF16), "fox": fox_w_out.astype(BF16)}
    ln_g3, ln_b3 = ln_g.reshape(DEPTH, 1, D_MODEL), ln_b.reshape(DEPTH, 1, D_MODEL)
    out_ops = None
    for i in range(DEPTH + 1):
        j = i // N_MIXERS
        kind = ("mla", "swa", "fox")[i % N_MIXERS] if i < DEPTH else None
        if kind == "mla":
            proj_ops = tuple(_Layer(a, j) for a in mla_ops) + mla_tables
        elif kind == "swa":
            proj_ops = (swa_w_in[j].astype(BF16),)
        elif kind == "fox":
            proj_ops = _fox_layer_operands(fox_w_in[j], fox_b_f[j])
        else:
            proj_ops = ()
        res = _boundary(x, out_ops, kind, proj_ops)
        if out_ops is not None:
            x, res = res[0], res[1:]
        if kind is None:
            return x
        q, k, v, sg = res
        if kind == "swa":
            o = _swa_attn(swa_sinks[j].astype(F32) * LOG2E, q, k, v, _swa_bias())
        else:
            o = _causal_attn(q, k, v)
        out_ops = (o, sg, _Layer(w_outs[kind], j), _Layer(ln_g3, i), _Layer(ln_b3, i))
```

```python
import functools
import math

import jax
import jax.numpy as jnp
import numpy as np
from jax import lax
from jax.experimental import pallas as pl
from jax.experimental.pallas import tpu as pltpu

D_MODEL = 1024
DEPTH = 4
N_MIXERS = 3
BLOCK = 128
NEG_INF = -1e30
RMS_EPS = 1e-6
LN_EPS = 1e-5
DEEPNORM_ALPHA = (2 * DEPTH) ** 0.25
LOG2E = math.log2(math.e)

MLA_HEADS = 16
MLA_NOPE_DIM = 64
MLA_ROPE_DIM = 32
MLA_V_DIM = 64
MLA_Q_RANK = 384
MLA_KV_RANK = 256
ROPE_THETA = 10000.0
MLA_WIDTH = MLA_HEADS * MLA_V_DIM

SWA_Q_HEADS = 16
SWA_KV_HEADS = 4
SWA_HEAD_DIM = 64
WINDOW = 128
SWA_WIDTH = SWA_Q_HEADS * SWA_HEAD_DIM
SWA_KV_WIDTH = SWA_KV_HEADS * SWA_HEAD_DIM

FOX_HEADS = 16
FOX_HEAD_DIM = 64
FOX_WIDTH = FOX_HEADS * FOX_HEAD_DIM

LANES = 128
HALF = LANES // 2
HEADS = 16
PADW = HEADS * LANES

ROW_TILE = 512
SUB_ROWS = 256
TQ = 512
TK = 512
ATT_HEADS = 4
SWA_QB = 4
VMEM_LIMIT = 56 * 1024 * 1024

F32 = jnp.float32
BF16 = jnp.bfloat16
NT_DIMS = (((1,), (1,)), ((), ()))


def _params(sem):
    return pltpu.CompilerParams(dimension_semantics=sem, vmem_limit_bytes=VMEM_LIMIT)


def _lane_lt_half(shape):
    return lax.broadcasted_iota(jnp.int32, shape, len(shape) - 1) < HALF


def _silu(g):
    return g / (1.0 + jnp.exp(-g))


def _write_vt(vt, vt_out):
    top = lax.broadcasted_iota(jnp.int32, (LANES, vt.shape[1]), 0) < HALF
    for p in range(HEADS // 2):
        blk = vt[p * LANES:(p + 1) * LANES, :]
        vt_out[(2 * p) * LANES:(2 * p + 1) * LANES, :] = jnp.where(top, blk, 1.0).astype(BF16)
        vt_out[(2 * p + 1) * LANES:(2 * p + 2) * LANES, :] = jnp.where(top, 1.0, blk).astype(BF16)


def _mla_body(x, win_ref, qg_ref, kg_ref, wq_ref, wk_ref, wvt_ref, aqe_ref, bqe_ref, aqo_ref, bqo_ref,
              ak_ref, bk_ref, q_out, k_out, vt_out, g_out):
    h = jnp.dot(x.astype(BF16), win_ref[...], preferred_element_type=F32)
    c1 = MLA_Q_RANK
    c2 = c1 + MLA_KV_RANK
    c3 = c2 + LANES
    cq, ckv, krg, gate = h[:, :c1], h[:, c1:c2], h[:, c2:c3], h[:, c3:]
    g_out[...] = _silu(gate).astype(BF16)

    qn = cq * lax.rsqrt(jnp.mean(cq * cq, axis=-1, keepdims=True) + RMS_EPS) * qg_ref[...]
    kn = (ckv * lax.rsqrt(jnp.mean(ckv * ckv, axis=-1, keepdims=True) + RMS_EPS) * kg_ref[...]).astype(BF16)
    q = jnp.dot(qn.astype(BF16), wq_ref[...], preferred_element_type=F32)
    kc = jnp.dot(kn, wk_ref[...], preferred_element_type=F32)
    _write_vt(lax.dot_general(wvt_ref[...], kn, NT_DIMS, preferred_element_type=F32), vt_out)

    shift = LANES - MLA_ROPE_DIM
    aq = (aqe_ref[...], aqo_ref[...])
    bq = (bqe_ref[...], bqo_ref[...])
    kr_hi = krg * ak_ref[...] + pltpu.roll(krg, shift, 1) * bk_ref[...]
    kr_lo = pltpu.roll(kr_hi, HALF, 1)
    lo = _lane_lt_half((x.shape[0], LANES))
    for p in range(HEADS // 2):
        e = slice((2 * p) * LANES, (2 * p + 1) * LANES)
        o = slice((2 * p + 1) * LANES, (2 * p + 2) * LANES)
        kblk = kc[:, p * LANES:(p + 1) * LANES]
        q_out[:, e] = (q[:, e] * aq[0] + pltpu.roll(q[:, e], shift, 1) * bq[0]).astype(BF16)
        q_out[:, o] = (q[:, o] * aq[1] + pltpu.roll(q[:, o], shift, 1) * bq[1]).astype(BF16)
        k_out[:, e] = jnp.where(lo, kblk, kr_hi).astype(BF16)
        k_out[:, o] = jnp.where(lo, kr_lo, kblk).astype(BF16)


def _split_pairs(nat, aug, out_ref):
    lo = _lane_lt_half((nat.shape[0], LANES))
    for p in range(HEADS // 2):
        blk = nat[:, p * LANES:(p + 1) * LANES]
        fill = aug[:, p * LANES:(p + 1) * LANES]
        out_ref[:, (2 * p) * LANES:(2 * p + 1) * LANES] = jnp.where(lo, blk, fill).astype(BF16)
        out_ref[:, (2 * p + 1) * LANES:(2 * p + 2) * LANES] = jnp.where(lo, fill, blk).astype(BF16)


def _fox_body(x, win_ref, wvt_ref, bf_ref, scat_ref, onesq_ref, onesk_ref, q_out, k_out, vt_out, g_out, carry_ref):
    tm = x.shape[0]
    xb = x.astype(BF16)
    h = jnp.dot(xb, win_ref[...], preferred_element_type=F32)
    W = FOX_WIDTH
    g_out[...] = _silu(h[:, 2 * W:3 * W]).astype(BF16)
    _write_vt(lax.dot_general(wvt_ref[...], xb, NT_DIMS, preferred_element_type=F32), vt_out)

    f = h[:, 3 * W:3 * W + LANES] + bf_ref[...]
    logf = jnp.minimum(f, 0.0) - jnp.log(1.0 + jnp.exp(-jnp.abs(f)))

    def split3(a):
        hi = a.astype(BF16)
        r = a - hi.astype(F32)
        mid = r.astype(BF16)
        lo = (r - mid.astype(F32)).astype(BF16)
        return hi, mid, lo

    ri = lax.broadcasted_iota(jnp.int32, (tm, tm), 0)
    ci = lax.broadcasted_iota(jnp.int32, (tm, tm), 1)
    tri = (ci <= ri).astype(BF16)
    l_hi, l_mid, l_lo = split3(logf)
    cum = (jnp.dot(tri, l_hi, preferred_element_type=F32) + jnp.dot(tri, l_mid, preferred_element_type=F32)
           + jnp.dot(tri, l_lo, preferred_element_type=F32)) + carry_ref[...]
    carry_ref[...] = cum[tm - 1:tm, :]

    c_hi, c_mid, c_lo = split3(cum * LOG2E)
    lane = lax.broadcasted_iota(jnp.int32, (tm, LANES), 1)
    packed = jnp.where(lane < HEADS, c_hi.astype(F32),
                       jnp.where(lane < 2 * HEADS, pltpu.roll(c_mid.astype(F32), HEADS, 1),
                                 pltpu.roll(c_lo.astype(F32), 2 * HEADS, 1)))
    packed = jnp.where(lane < 3 * HEADS, packed, 0.0).astype(BF16)
    aug = jnp.dot(packed, scat_ref[...], preferred_element_type=F32)
    augq = aug[:, :W] + onesq_ref[...]
    augk = aug[:, W:] + onesk_ref[...]

    _split_pairs(h[:, :W] * (FOX_HEAD_DIM ** -0.5 * LOG2E), augq, q_out)
    _split_pairs(h[:, W:2 * W], augk, k_out)


def _swa_body(x, win_ref, q_out, k_out, v_out, g_out):
    h = jnp.dot(x.astype(BF16), win_ref[...], preferred_element_type=F32)
    c1 = SWA_WIDTH
    c2 = c1 + SWA_KV_WIDTH
    c3 = c2 + SWA_KV_WIDTH
    q_out[...] = (h[:, :c1] * (SWA_HEAD_DIM ** -0.5 * LOG2E)).astype(BF16)
    g_out[...] = _silu(h[:, c3:]).astype(BF16)
    lo = _lane_lt_half((x.shape[0], LANES))
    for pair in range(SWA_KV_HEADS // 2):
        kp = h[:, c1 + pair * LANES:c1 + (pair + 1) * LANES]
        vp = h[:, c2 + pair * LANES:c2 + (pair + 1) * LANES]
        ks, vs = pltpu.roll(kp, HALF, 1), pltpu.roll(vp, HALF, 1)
        for t, (k_h, v_h) in enumerate(((jnp.where(lo, kp, ks), jnp.where(lo, vp, vs)),
                                        (jnp.where(lo, ks, kp), jnp.where(lo, vs, vp)))):
            g = 2 * pair + t
            k_out[:, g * LANES:(g + 1) * LANES] = k_h.astype(BF16)
            v_out[:, (2 * g) * LANES:(2 * g + 1) * LANES] = jnp.where(lo, v_h, 1.0).astype(BF16)
            v_out[:, (2 * g + 1) * LANES:(2 * g + 2) * LANES] = jnp.where(lo, 1.0, v_h).astype(BF16)


VT = "vt"
_PROJ = {
    "mla": (_mla_body, 12, (PADW, PADW, VT, MLA_WIDTH), 6),
    "fox": (_fox_body, 6, (PADW, PADW, VT, FOX_WIDTH), 0),
    "swa": (_swa_body, 1, (SWA_WIDTH, SWA_KV_HEADS * LANES, SWA_KV_HEADS * 2 * LANES, SWA_WIDTH), 0),
}


def _out_body(o_ref, g_ref, x_ref, w_ref, lg_ref, lb_ref):
    a = (o_ref[...].astype(F32) * g_ref[...].astype(F32)).astype(BF16)
    y = jnp.dot(a, w_ref[...], preferred_element_type=F32)
    z = DEEPNORM_ALPHA * x_ref[...] + y
    mu = jnp.mean(z, axis=-1, keepdims=True)
    zc = z - mu
    var = jnp.mean(zc * zc, axis=-1, keepdims=True)
    return zc * lax.rsqrt(var + LN_EPS) * lg_ref[...] + lb_ref[...]


def _boundary_kernel(*refs, with_out, kind):
    refs = list(refs)
    if with_out:
        out_in, refs = refs[:6], refs[6:]
    else:
        x_ref, refs = refs[0], refs[1:]
    n_ops, widths, n_tab = _PROJ[kind][1:] if kind else (0, (), 0)
    proj_in, refs = refs[:n_ops], refs[n_ops:]
    if with_out:
        y_ref, refs = refs[0], refs[1:]
    outs, scratch = refs[:len(widths)], refs[len(widths):]
    if kind == "fox":
        @pl.when(pl.program_id(1) == 0)
        def _():
            scratch[0][...] = jnp.zeros_like(scratch[0])

    tm = (out_in[2] if with_out else x_ref).shape[0]
    blocks = [pl.ds(r * SUB_ROWS, SUB_ROWS) for r in range(tm // SUB_ROWS)]
    xs = []
    for rows in blocks:
        if with_out:
            xs.append(_out_body(*(ref.at[rows] for ref in out_in[:3]), *out_in[3:]))
            y_ref[rows, :] = xs[-1]
        else:
            xs.append(x_ref[rows, :])
    for rows, x in zip(blocks, xs):
        if kind:
            _PROJ[kind][0](x, *proj_in[:n_ops - n_tab], *(t.at[rows] for t in proj_in[n_ops - n_tab:]),
                           *(o.at[:, rows] if w == VT else o.at[rows] for o, w in zip(outs, widths)), *scratch)


class _Layer:
    def __init__(self, arr, layer):
        self.arr, self.layer = arr, layer


def _boundary(x, out_ops, kind, proj_ops):
    B, S, _ = x.shape
    tm = ROW_TILE if kind else 2 * ROW_TILE
    row = lambda w: pl.BlockSpec((None, tm, w), lambda b, i: (b, i, 0))

    def full(a):
        if isinstance(a, _Layer):
            idx = (a.layer,) + (0,) * (a.arr.ndim - 1)
            return pl.BlockSpec((None,) + a.arr.shape[1:], lambda b, i: idx, pipeline_mode=pl.Buffered(1))
        return pl.BlockSpec(a.shape, lambda b, i: (0,) * a.ndim, pipeline_mode=pl.Buffered(1))

    unwrap = lambda a: a.arr if isinstance(a, _Layer) else a
    tab = pl.BlockSpec((tm, LANES), lambda b, i: (i, 0))
    in_specs, args, out_specs, out_shape, scratch = [], [], [], [], []
    if out_ops is not None:
        o, sg, w, lg, lb = out_ops
        in_specs += [row(D_MODEL), row(D_MODEL), row(D_MODEL), full(w), full(lg), full(lb)]
        args += [o, sg, x, unwrap(w), unwrap(lg), unwrap(lb)]
        out_specs.append(row(D_MODEL))
        out_shape.append(jax.ShapeDtypeStruct((B, S, D_MODEL), F32))
    else:
        in_specs.append(row(D_MODEL))
        args.append(x)
    if kind:
        _, n_ops, widths, n_tab = _PROJ[kind]
        assert len(proj_ops) == n_ops
        in_specs += [full(a) for a in proj_ops[:n_ops - n_tab]] + [tab] * n_tab
        args += [unwrap(a) for a in proj_ops]
        for w in widths:
            if w == VT:
                assert TK % tm == 0
                out_specs.append(pl.BlockSpec((None, None, PADW, tm),
                                              lambda b, i: (b, i * tm // TK, 0, i % (TK // tm))))
                out_shape.append(jax.ShapeDtypeStruct((B, S // TK, PADW, TK), BF16))
            else:
                out_specs.append(row(w))
                out_shape.append(jax.ShapeDtypeStruct((B, S, w), BF16))
        if kind == "fox":
            scratch.append(pltpu.VMEM((1, LANES), F32))
    sem = ("parallel", "arbitrary" if kind == "fox" else "parallel")
    return pl.pallas_call(
        functools.partial(_boundary_kernel, with_out=out_ops is not None, kind=kind),
        grid=(B, S // tm),
        in_specs=in_specs,
        out_specs=out_specs,
        out_shape=out_shape,
        scratch_shapes=scratch,
        compiler_params=_params(sem),
        name="boundary_" + ("out_" if out_ops is not None else "") + (kind or "final"),
    )(*args)


def _normalize_pair(ext_even, ext_odd, extra_even=None, extra_odd=None):
    lo = _lane_lt_half(ext_even.shape)
    den = jnp.where(lo, ext_odd, ext_even)
    if extra_even is not None:
        den = den + jnp.where(lo, extra_odd, extra_even)
    return jnp.where(lo, ext_even, ext_odd) * pltpu.roll(1.0 / den, HALF, 1)


def _causal_attn_kernel(q_ref, k_ref, vt_ref, o_ref, m_ref, acc_ref):
    S = q_ref.shape[0]
    heads = range(ATT_HEADS)
    lanes = [slice(h * LANES, (h + 1) * LANES) for h in heads]

    def tile(q0, kt, jobs, diagonal, qrel=0):
        chains = [(h,) + job for job in jobs for h in heads]
        k0 = kt * TK
        s_all = [lax.dot_general(q_ref[pl.ds(q0 + roff, nq), lanes[h]], k_ref[pl.ds(k0, nk), lanes[h]],
                                 NT_DIMS, preferred_element_type=F32) for h, roff, nq, nk in chains]
        p_all, alpha_all = [], []
        for (h, roff, nq, nk), s in zip(chains, s_all):
            st = slice(roff, roff + nq)
            blocks = [s[:, j * LANES:(j + 1) * LANES] for j in range(nk // LANES)]
            if diagonal:
                row = lax.broadcasted_iota(jnp.int32, (nq, LANES), 0) + (roff + qrel)
                col = lax.broadcasted_iota(jnp.int32, (nq, LANES), 1)
                blocks = [b if (j + 1) * LANES - 1 <= roff + qrel else jnp.where(col + j * LANES <= row, b, NEG_INF)
                          for j, b in enumerate(blocks)]
            m_blk = jnp.max(functools.reduce(jnp.maximum, blocks), axis=-1, keepdims=True)
            if diagonal:
                m_new = jnp.broadcast_to(m_blk, (nq, LANES))
                alpha_all.append(None)
            else:
                m_old = m_ref[h, st, :]
                m_new = jnp.maximum(m_old, m_blk)
                alpha_all.append(jnp.exp2(m_old - m_new))
            p_all.append(jnp.concatenate([jnp.exp2(b - m_new) for b in blocks], axis=1).astype(BF16))
            m_ref[h, st, :] = m_new
        for (h, roff, nq, nk), p, alpha in zip(chains, p_all, alpha_all):
            st = slice(roff, roff + nq)
            parts = [vt_ref[kt + t, lanes[h], :] for t in range(nk // TK)]
            if nk % TK:
                parts.append(vt_ref[kt + nk // TK, lanes[h], 0:nk % TK])
            vt = parts[0] if len(parts) == 1 else jnp.concatenate(parts, axis=1)
            pv = lax.dot_general(p, vt, NT_DIMS, preferred_element_type=F32)
            acc_ref[h, st, :] = pv if alpha is None else acc_ref[h, st, :] * alpha + pv

    half = TQ // 2
    for qi in range(S // TQ):
        q0 = qi * TQ
        tile(q0, 0, [(0, half, q0 + half), (half, half, q0 + TQ)], True, qrel=q0)
        for p in range(ATT_HEADS // 2):
            o_ref[q0:q0 + TQ, p * LANES:(p + 1) * LANES] = _normalize_pair(
                acc_ref[2 * p], acc_ref[2 * p + 1]).astype(BF16)


def _causal_attn(q, k, vt):
    B, S, _ = q.shape
    spec = pl.BlockSpec((None, S, ATT_HEADS * LANES), lambda b, p: (b, 0, p))
    return pl.pallas_call(
        _causal_attn_kernel,
        grid=(B, HEADS // ATT_HEADS),
        in_specs=[spec, spec, pl.BlockSpec((None, S // TK, ATT_HEADS * LANES, TK), lambda b, p: (b, 0, p, 0))],
        out_specs=pl.BlockSpec((None, S, ATT_HEADS * HALF), lambda b, p: (b, 0, p)),
        out_shape=jax.ShapeDtypeStruct((B, S, HEADS * HALF), BF16),
        scratch_shapes=[pltpu.VMEM((ATT_HEADS, TQ, LANES), F32), pltpu.VMEM((ATT_HEADS, TQ, LANES), F32)],
        compiler_params=_params(("parallel", "parallel")),
        name="causal_attn",
    )(q, k, vt)


def _swa_attn_kernel(sink_ref, q_ref, kp_ref, kc_ref, vp_ref, vc_ref, bias0_ref, bias_ref, o_ref):
    nqb = q_ref.shape[0] // BLOCK
    lo = _lane_lt_half((BLOCK, LANES))
    zero = jnp.zeros((), BF16)

    def band(prev_ref, cur_ref, i, group):
        cols = slice(group * LANES, (group + 1) * LANES)
        if i == 0:
            return jnp.concatenate([prev_ref[:, cols], cur_ref[0:BLOCK, cols]], axis=0)
        return cur_ref[(i - 1) * BLOCK:(i + 1) * BLOCK, cols]

    def qk(i, g):
        rows = slice(i * BLOCK, (i + 1) * BLOCK)
        kb = band(kp_ref, kc_ref, i, g)
        qa = q_ref[rows, (2 * g) * LANES:(2 * g + 1) * LANES]
        qb = q_ref[rows, (2 * g + 1) * LANES:(2 * g + 2) * LANES]
        out = []
        for par in range(2):
            keep = lo if par == 0 else jnp.logical_not(lo)
            stack = jnp.concatenate([jnp.where(keep, qa, zero), jnp.where(keep, qb, zero)], axis=0)
            out.append(lax.dot_general(stack, kb, NT_DIMS, preferred_element_type=F32))
        return out

    def finish(i, g, s_pair):
        b_ref = bias0_ref if i == 0 else bias_ref
        ext, extra = [], []
        for par, s in enumerate(s_pair):
            ha, hb = 4 * g + par, 4 * g + 2 + par
            blocks = [s[:, j * LANES:(j + 1) * LANES]
                      + jnp.concatenate([b_ref[ha, :, j * LANES:(j + 1) * LANES],
                                         b_ref[hb, :, j * LANES:(j + 1) * LANES]], axis=0) for j in range(2)]
            sink = jnp.concatenate([jnp.full((BLOCK, LANES), sink_ref[ha], F32),
                                    jnp.full((BLOCK, LANES), sink_ref[hb], F32)], axis=0)
            m = jnp.maximum(sink, jnp.max(jnp.maximum(blocks[0], blocks[1]), axis=-1, keepdims=True))
            p = jnp.concatenate([jnp.exp2(b - m) for b in blocks], axis=1).astype(BF16)
            vb = band(vp_ref, vc_ref, i, 2 * g + par)
            ext.append(jnp.dot(p, vb, preferred_element_type=F32))
            extra.append(jnp.exp2(sink - m))
        for j in range(2):
            rows = slice(j * BLOCK, (j + 1) * BLOCK)
            o_ref[i * BLOCK:(i + 1) * BLOCK, (2 * g + j) * LANES:(2 * g + j + 1) * LANES] = _normalize_pair(
                ext[0][rows], ext[1][rows], extra[0][rows], extra[1][rows]).astype(BF16)

    units = [(i, g) for i in range(nqb) for g in range(SWA_KV_HEADS)]
    s_cur = qk(*units[0])
    for u, (i, g) in enumerate(units):
        s_next = qk(*units[u + 1]) if u + 1 < len(units) else None
        finish(i, g, s_cur)
        s_cur = s_next


def _swa_attn(sinks2, q, k, v, bias):
    B, S, _ = q.shape
    rows = SWA_QB * BLOCK
    cur = lambda w: pl.BlockSpec((None, rows, w), lambda b, n: (b, n, 0))
    prev = lambda w: pl.BlockSpec((None, BLOCK, w), lambda b, n: (b, jnp.maximum(n * SWA_QB - 1, 0), 0))
    bias_shape = (None, SWA_Q_HEADS, BLOCK, 2 * BLOCK)
    kw, vw = k.shape[-1], v.shape[-1]
    return pl.pallas_call(
        _swa_attn_kernel,
        grid=(B, S // rows),
        in_specs=[pl.BlockSpec(memory_space=pltpu.SMEM), cur(SWA_WIDTH), prev(kw), cur(kw), prev(vw), cur(vw),
                  pl.BlockSpec(bias_shape, lambda b, n: (jnp.minimum(n, 1), 0, 0, 0)),
                  pl.BlockSpec(bias_shape, lambda b, n: (1, 0, 0, 0))],
        out_specs=cur(SWA_WIDTH),
        out_shape=jax.ShapeDtypeStruct((B, S, SWA_WIDTH), BF16),
        compiler_params=_params(("parallel", "parallel")),
        name="swa_attn",
    )(sinks2, q, k, k, v, v, bias, bias)


def _rot(w):
    half = w.shape[-1] // 2
    return jnp.concatenate([-w[..., half:], w[..., :half]], axis=-1)


def _rope_tables(seq, c0):
    inv = ROPE_THETA ** (-np.arange(0, MLA_ROPE_DIM, 2, dtype=np.float64) / MLA_ROPE_DIM)
    ang = np.arange(seq, dtype=np.float64)[:, None] * inv[None, :]
    cos = np.concatenate([np.cos(ang)] * 2, axis=-1)
    sin = np.concatenate([np.sin(ang)] * 2, axis=-1)
    z32 = np.zeros((seq, LANES - HALF - MLA_ROPE_DIM))
    z64 = np.zeros((seq, HALF))
    c64 = np.full((seq, HALF), c0)
    aq_even = np.concatenate([c64, c0 * cos, z32], axis=-1)
    bq_even = np.concatenate([z64, c0 * sin, z32], axis=-1)
    aq_odd = np.concatenate([c0 * cos, z32, c64], axis=-1)
    bq_odd = np.concatenate([c0 * sin, z32, z64], axis=-1)
    ak = np.concatenate([z64, cos, z32], axis=-1)
    bk = np.concatenate([z64, sin, z32], axis=-1)
    return tuple(t.astype(np.float32) for t in (aq_even, bq_even, aq_odd, bq_odd, ak, bk))


def _mla_operands(w_in, q_norm, w_q_up, kv_norm, w_kv_up):
    L = w_in.shape[0]
    c2 = MLA_Q_RANK + MLA_KV_RANK
    c3 = c2 + MLA_ROPE_DIM
    kr = w_in[..., c2:c3]
    krg = jnp.concatenate([jnp.zeros((L, D_MODEL, HALF), F32), kr, _rot(kr)], axis=-1)
    win = jnp.concatenate([w_in[..., :c2], krg, w_in[..., c3:]], axis=-1).astype(BF16)
    odd = (jnp.arange(MLA_HEADS) % 2 == 1)[:, None]
    wq = w_q_up.reshape(L, MLA_Q_RANK, MLA_HEADS, MLA_NOPE_DIM + MLA_ROPE_DIM)
    nope, rope = wq[..., :MLA_NOPE_DIM], wq[..., MLA_NOPE_DIM:]
    wq = jnp.where(odd, jnp.concatenate([rope, _rot(rope), nope], axis=-1),
                   jnp.concatenate([nope, rope, _rot(rope)], axis=-1))
    wq = wq.reshape(L, MLA_Q_RANK, PADW).astype(BF16)
    wkv = w_kv_up.reshape(L, MLA_KV_RANK, MLA_HEADS, MLA_NOPE_DIM + MLA_V_DIM)
    wk = wkv[..., :MLA_NOPE_DIM].reshape(L, MLA_KV_RANK, MLA_HEADS * MLA_NOPE_DIM).astype(BF16)
    wvt = jnp.swapaxes(wkv[..., MLA_NOPE_DIM:].reshape(L, MLA_KV_RANK, MLA_WIDTH), 1, 2).astype(BF16)
    return win, q_norm.reshape(L, 1, -1), kv_norm.reshape(L, 1, -1), wq, wk, wvt


def _fox_layer_operands(w_in, b_f):
    W = FOX_WIDTH
    wf = jnp.concatenate([w_in[:, 3 * W:3 * W + FOX_HEADS], jnp.zeros((D_MODEL, LANES - FOX_HEADS), F32)], axis=-1)
    win = jnp.concatenate([w_in[:, :2 * W], w_in[:, 3 * W + FOX_HEADS:], wf], axis=-1).astype(BF16)
    wvt = w_in[:, 2 * W:3 * W].T.astype(BF16)
    bf = jnp.concatenate([b_f, jnp.zeros((LANES - FOX_HEADS,), F32)]).reshape(1, LANES)
    h = np.arange(FOX_HEADS)
    base = (h // 2) * LANES + np.where(h % 2 == 0, HALF, 0)
    scat = np.zeros((LANES, 2 * W), np.float32)
    onesq = np.zeros((1, W), np.float32)
    onesk = np.zeros((1, W), np.float32)
    for j in range(3):
        scat[j * FOX_HEADS + h, base + j] = 1.0
        scat[j * FOX_HEADS + h, W + base + 3 + j] = -1.0
        onesq[0, base + 3 + j] = 1.0
        onesk[0, base + j] = 1.0
    return win, wvt, bf, jnp.asarray(scat, dtype=BF16), onesq, onesk


def _swa_bias():
    qi = np.arange(BLOCK)[:, None]
    kj = np.arange(2 * BLOCK)[None, :]
    dist = qi + BLOCK - kj
    valid = (dist >= 0) & (dist < WINDOW)
    slopes = 2.0 ** (-8.0 * np.arange(1, SWA_Q_HEADS + 1, dtype=np.float64) / SWA_Q_HEADS)
    ali = -(slopes[:, None, None] * dist.astype(np.float64)[None]) * LOG2E
    bias_rest = np.where(valid[None], ali, NEG_INF)
    bias_first = np.where((valid & (kj >= BLOCK))[None], ali, NEG_INF)
    return np.stack([bias_first, bias_rest]).astype(np.float32)


def kernel(x, ln_g, ln_b, mla_w_in, mla_q_norm, mla_w_q_up, mla_kv_norm, mla_w_kv_up, mla_w_out,
           swa_w_in, swa_sinks, swa_w_out, fox_w_in, fox_b_f, fox_w_out):
    seq = x.shape[1]
    mla_ops = _mla_operands(mla_w_in, mla_q_norm, mla_w_q_up, mla_kv_norm, mla_w_kv_up)
    mla_tables = _rope_tables(seq, (MLA_NOPE_DIM + MLA_ROPE_DIM) ** -0.5 * LOG2E)
    w_outs = {"mla": mla_w_out.astype(BF16), "swa": swa_w_out.astype(BF16), "fox": fox_w_out.astype(BF16)}
    ln_g3, ln_b3 = ln_g.reshape(DEPTH, 1, D_MODEL), ln_b.reshape(DEPTH, 1, D_MODEL)
    out_ops = None
    for i in range(DEPTH + 1):
        j = i // N_MIXERS
        kind = ("mla", "swa", "fox")[i % N_MIXERS] if i < DEPTH else None
        if kind == "mla":
            proj_ops = tuple(_Layer(a, j) for a in mla_ops) + mla_tables
        elif kind == "swa":
            proj_ops = (swa_w_in[j].astype(BF16),)
        elif kind == "fox":
            proj_ops = _fox_layer_operands(fox_w_in[j], fox_b_f[j])
        else:
            proj_ops = ()
        res = _boundary(x, out_ops, kind, proj_ops)
        if out_ops is not None:
            x, res = res[0], res[1:]
        if kind is None:
            return x
        q, k, v, sg = res
        if kind == "swa":
            o = _swa_attn(swa_sinks[j].astype(F32) * LOG2E, q, k, v, _swa_bias())
        else:
            o = _causal_attn(q, k, v)
        out_ops = (o, sg, _Layer(w_outs[kind], j), _Layer(ln_g3, i), _Layer(ln_b3, i))
```

```python
import functools
import math

import jax
import jax.numpy as jnp
import numpy as np
from jax import lax
from jax.experimental import pallas as pl
from jax.experimental.pallas import tpu as pltpu

D_MODEL = 1024
DEPTH = 4
N_MIXERS = 3
BLOCK = 128
NEG_INF = -1e30
RMS_EPS = 1e-6
LN_EPS = 1e-5
DEEPNORM_ALPHA = (2 * DEPTH) ** 0.25
LOG2E = math.log2(math.e)

MLA_HEADS = 16
MLA_NOPE_DIM = 64
MLA_ROPE_DIM = 32
MLA_V_DIM = 64
MLA_Q_RANK = 384
MLA_KV_RANK = 256
ROPE_THETA = 10000.0
MLA_WIDTH = MLA_HEADS * MLA_V_DIM

SWA_Q_HEADS = 16
SWA_KV_HEADS = 4
SWA_HEAD_DIM = 64
WINDOW = 128
SWA_WIDTH = SWA_Q_HEADS * SWA_HEAD_DIM
SWA_KV_WIDTH = SWA_KV_HEADS * SWA_HEAD_DIM

FOX_HEADS = 16
FOX_HEAD_DIM = 64
FOX_WIDTH = FOX_HEADS * FOX_HEAD_DIM

LANES = 128
HALF = LANES // 2
HEADS = 16
PADW = HEADS * LANES

ROW_TILE = 512
SUB_ROWS = 256
TQ = 512
TK = 512
ATT_HEADS = 4
SWA_QB = 4
VMEM_LIMIT = 56 * 1024 * 1024

F32 = jnp.float32
BF16 = jnp.bfloat16
NT_DIMS = (((1,), (1,)), ((), ()))


def _params(sem):
    return pltpu.CompilerParams(dimension_semantics=sem, vmem_limit_bytes=VMEM_LIMIT)


def _lane_lt_half(shape):
    return lax.broadcasted_iota(jnp.int32, shape, len(shape) - 1) < HALF


def _silu(g):
    return g / (1.0 + jnp.exp(-g))


def _write_vt(vt, vt_out):
    top = lax.broadcasted_iota(jnp.int32, (LANES, vt.shape[1]), 0) < HALF
    for p in range(HEADS // 2):
        blk = vt[p * LANES:(p + 1) * LANES, :]
        vt_out[(2 * p) * LANES:(2 * p + 1) * LANES, :] = jnp.where(top, blk, 1.0).astype(BF16)
        vt_out[(2 * p + 1) * LANES:(2 * p + 2) * LANES, :] = jnp.where(top, 1.0, blk).astype(BF16)


def _mla_body(x, win_ref, qg_ref, kg_ref, wq_ref, wk_ref, wvt_ref, aqe_ref, bqe_ref, aqo_ref, bqo_ref,
              ak_ref, bk_ref, q_out, k_out, vt_out, g_out):
    h = jnp.dot(x.astype(BF16), win_ref[...], preferred_element_type=F32)
    c1 = MLA_Q_RANK
    c2 = c1 + MLA_KV_RANK
    c3 = c2 + LANES
    cq, ckv, krg, gate = h[:, :c1], h[:, c1:c2], h[:, c2:c3], h[:, c3:]
    g_out[...] = _silu(gate).astype(BF16)

    qn = cq * lax.rsqrt(jnp.mean(cq * cq, axis=-1, keepdims=True) + RMS_EPS) * qg_ref[...]
    kn = (ckv * lax.rsqrt(jnp.mean(ckv * ckv, axis=-1, keepdims=True) + RMS_EPS) * kg_ref[...]).astype(BF16)
    q = jnp.dot(qn.astype(BF16), wq_ref[...], preferred_element_type=F32)
    kc = jnp.dot(kn, wk_ref[...], preferred_element_type=F32)
    _write_vt(lax.dot_general(wvt_ref[...], kn, NT_DIMS, preferred_element_type=F32), vt_out)

    shift = LANES - MLA_ROPE_DIM
    aq = (aqe_ref[...], aqo_ref[...])
    bq = (bqe_ref[...], bqo_ref[...])
    kr_hi = krg * ak_ref[...] + pltpu.roll(krg, shift, 1) * bk_ref[...]
    kr_lo = pltpu.roll(kr_hi, HALF, 1)
    lo = _lane_lt_half((x.shape[0], LANES))
    for p in range(HEADS // 2):
        e = slice((2 * p) * LANES, (2 * p + 1) * LANES)
        o = slice((2 * p + 1) * LANES, (2 * p + 2) * LANES)
        kblk = kc[:, p * LANES:(p + 1) * LANES]
        q_out[:, e] = (q[:, e] * aq[0] + pltpu.roll(q[:, e], shift, 1) * bq[0]).astype(BF16)
        q_out[:, o] = (q[:, o] * aq[1] + pltpu.roll(q[:, o], shift, 1) * bq[1]).astype(BF16)
        k_out[:, e] = jnp.where(lo, kblk, kr_hi).astype(BF16)
        k_out[:, o] = jnp.where(lo, kr_lo, kblk).astype(BF16)


def _split_pairs(nat, aug, out_ref):
    lo = _lane_lt_half((nat.shape[0], LANES))
    for p in range(HEADS // 2):
        blk = nat[:, p * LANES:(p + 1) * LANES]
        fill = aug[:, p * LANES:(p + 1) * LANES]
        out_ref[:, (2 * p) * LANES:(2 * p + 1) * LANES] = jnp.where(lo, blk, fill).astype(BF16)
        out_ref[:, (2 * p + 1) * LANES:(2 * p + 2) * LANES] = jnp.where(lo, fill, blk).astype(BF16)


def _fox_body(x, win_ref, wvt_ref, bf_ref, scat_ref, onesq_ref, onesk_ref, q_out, k_out, vt_out, g_out, carry_ref):
    tm = x.shape[0]
    xb = x.astype(BF16)
    h = jnp.dot(xb, win_ref[...], preferred_element_type=F32)
    W = FOX_WIDTH
    g_out[...] = _silu(h[:, 2 * W:3 * W]).astype(BF16)
    _write_vt(lax.dot_general(wvt_ref[...], xb, NT_DIMS, preferred_element_type=F32), vt_out)

    f = h[:, 3 * W:3 * W + LANES] + bf_ref[...]
    logf = jnp.minimum(f, 0.0) - jnp.log(1.0 + jnp.exp(-jnp.abs(f)))

    def split3(a):
        hi = a.astype(BF16)
        r = a - hi.astype(F32)
        mid = r.astype(BF16)
        lo = (r - mid.astype(F32)).astype(BF16)
        return hi, mid, lo

    ri = lax.broadcasted_iota(jnp.int32, (tm, tm), 0)
    ci = lax.broadcasted_iota(jnp.int32, (tm, tm), 1)
    tri = (ci <= ri).astype(BF16)
    l_hi, l_mid, l_lo = split3(logf)
    cum = (jnp.dot(tri, l_hi, preferred_element_type=F32) + jnp.dot(tri, l_mid, preferred_element_type=F32)
           + jnp.dot(tri, l_lo, preferred_element_type=F32)) + carry_ref[...]
    carry_ref[...] = cum[tm - 1:tm, :]

    c_hi, c_mid, c_lo = split3(cum * LOG2E)
    lane = lax.broadcasted_iota(jnp.int32, (tm, LANES), 1)
    packed = jnp.where(lane < HEADS, c_hi.astype(F32),
                       jnp.where(lane < 2 * HEADS, pltpu.roll(c_mid.astype(F32), HEADS, 1),
                                 pltpu.roll(c_lo.astype(F32), 2 * HEADS, 1)))
    packed = jnp.where(lane < 3 * HEADS, packed, 0.0).astype(BF16)
    aug = jnp.dot(packed, scat_ref[...], preferred_element_type=F32)
    augq = aug[:, :W] + onesq_ref[...]
    augk = aug[:, W:] + onesk_ref[...]

    _split_pairs(h[:, :W] * (FOX_HEAD_DIM ** -0.5 * LOG2E), augq, q_out)
    _split_pairs(h[:, W:2 * W], augk, k_out)


def _swa_body(x, win_ref, q_out, k_out, v_out, g_out):
    h = jnp.dot(x.astype(BF16), win_ref[...], preferred_element_type=F32)
    c1 = SWA_WIDTH
    c2 = c1 + SWA_KV_WIDTH
    c3 = c2 + SWA_KV_WIDTH
    q_out[...] = (h[:, :c1] * (SWA_HEAD_DIM ** -0.5 * LOG2E)).astype(BF16)
    g_out[...] = _silu(h[:, c3:]).astype(BF16)
    lo = _lane_lt_half((x.shape[0], LANES))
    for pair in range(SWA_KV_HEADS // 2):
        kp = h[:, c1 + pair * LANES:c1 + (pair + 1) * LANES]
        vp = h[:, c2 + pair * LANES:c2 + (pair + 1) * LANES]
        ks, vs = pltpu.roll(kp, HALF, 1), pltpu.roll(vp, HALF, 1)
        for t, (k_h, v_h) in enumerate(((jnp.where(lo, kp, ks), jnp.where(lo, vp, vs)),
                                        (jnp.where(lo, ks, kp), jnp.where(lo, vs, vp)))):
            g = 2 * pair + t
            k_out[:, g * LANES:(g + 1) * LANES] = k_h.astype(BF16)
            v_out[:, (2 * g) * LANES:(2 * g + 1) * LANES] = jnp.where(lo, v_h, 1.0).astype(BF16)
            v_out[:, (2 * g + 1) * LANES:(2 * g + 2) * LANES] = jnp.where(lo, 1.0, v_h).astype(BF16)


VT = "vt"
_PROJ = {
    "mla": (_mla_body, 12, (PADW, PADW, VT, MLA_WIDTH), 6),
    "fox": (_fox_body, 6, (PADW, PADW, VT, FOX_WIDTH), 0),
    "swa": (_swa_body, 1, (SWA_WIDTH, SWA_KV_HEADS * LANES, SWA_KV_HEADS * 2 * LANES, SWA_WIDTH), 0),
}


def _out_body(o_ref, g_ref, x_ref, w_ref, lg_ref, lb_ref):
    a = (o_ref[...].astype(F32) * g_ref[...].astype(F32)).astype(BF16)
    y = jnp.dot(a, w_ref[...], preferred_element_type=F32)
    z = DEEPNORM_ALPHA * x_ref[...] + y
    mu = jnp.mean(z, axis=-1, keepdims=True)
    zc = z - mu
    var = jnp.mean(zc * zc, axis=-1, keepdims=True)
    return zc * lax.rsqrt(var + LN_EPS) * lg_ref[...] + lb_ref[...]


def _boundary_kernel(*refs, with_out, kind):
    refs = list(refs)
    if with_out:
        out_in, refs = refs[:6], refs[6:]
    else:
        x_ref, refs = refs[0], refs[1:]
    n_ops, widths, n_tab = _PROJ[kind][1:] if kind else (0, (), 0)
    proj_in, refs = refs[:n_ops], refs[n_ops:]
    if with_out:
        y_ref, refs = refs[0], refs[1:]
    outs, scratch = refs[:len(widths)], refs[len(widths):]
    if kind == "fox":
        @pl.when(pl.program_id(1) == 0)
        def _():
            scratch[0][...] = jnp.zeros_like(scratch[0])

    tm = (out_in[2] if with_out else x_ref).shape[0]
    blocks = [pl.ds(r * SUB_ROWS, SUB_ROWS) for r in range(tm // SUB_ROWS)]
    xs = []
    for rows in blocks:
        if with_out:
            xs.append(_out_body(*(ref.at[rows] for ref in out_in[:3]), *out_in[3:]))
            y_ref[rows, :] = xs[-1]
        else:
            xs.append(x_ref[rows, :])
    for rows, x in zip(blocks, xs):
        if kind:
            _PROJ[kind][0](x, *proj_in[:n_ops - n_tab], *(t.at[rows] for t in proj_in[n_ops - n_tab:]),
                           *(o.at[:, rows] if w == VT else o.at[rows] for o, w in zip(outs, widths)), *scratch)


class _Layer:
    def __init__(self, arr, layer):
        self.arr, self.layer = arr, layer


def _boundary(x, out_ops, kind, proj_ops):
    B, S, _ = x.shape
    tm = ROW_TILE if kind else 2 * ROW_TILE
    row = lambda w: pl.BlockSpec((None, tm, w), lambda b, i: (b, i, 0))

    def full(a):
        if isinstance(a, _Layer):
            idx = (a.layer,) + (0,) * (a.arr.ndim - 1)
            return pl.BlockSpec((None,) + a.arr.shape[1:], lambda b, i: idx, pipeline_mode=pl.Buffered(1))
        return pl.BlockSpec(a.shape, lambda b, i: (0,) * a.ndim, pipeline_mode=pl.Buffered(1))

    unwrap = lambda a: a.arr if isinstance(a, _Layer) else a
    tab = pl.BlockSpec((tm, LANES), lambda b, i: (i, 0))
    in_specs, args, out_specs, out_shape, scratch = [], [], [], [], []
    if out_ops is not None:
        o, sg, w, lg, lb = out_ops
        in_specs += [row(D_MODEL), row(D_MODEL), row(D_MODEL), full(w), full(lg), full(lb)]
        args += [o, sg, x, unwrap(w), unwrap(lg), unwrap(lb)]
        out_specs.append(row(D_MODEL))
        out_shape.append(jax.ShapeDtypeStruct((B, S, D_MODEL), F32))
    else:
        in_specs.append(row(D_MODEL))
        args.append(x)
    if kind:
        _, n_ops, widths, n_tab = _PROJ[kind]
        assert len(proj_ops) == n_ops
        in_specs += [full(a) for a in proj_ops[:n_ops - n_tab]] + [tab] * n_tab
        args += [unwrap(a) for a in proj_ops]
        for w in widths:
            if w == VT:
                assert TK % tm == 0
                out_specs.append(pl.BlockSpec((None, None, PADW, tm),
                                              lambda b, i: (b, i * tm // TK, 0, i % (TK // tm))))
                out_shape.append(jax.ShapeDtypeStruct((B, S // TK, PADW, TK), BF16))
            else:
                out_specs.append(row(w))
                out_shape.append(jax.ShapeDtypeStruct((B, S, w), BF16))
        if kind == "fox":
            scratch.append(pltpu.VMEM((1, LANES), F32))
    sem = ("parallel", "arbitrary" if kind == "fox" else "parallel")
    return pl.pallas_call(
        functools.partial(_boundary_kernel, with_out=out_ops is not None, kind=kind),
        grid=(B, S // tm),
        in_specs=in_specs,
        out_specs=out_specs,
        out_shape=out_shape,
        scratch_shapes=scratch,
        compiler_params=_params(sem),
        name="boundary_" + ("out_" if out_ops is not None else "") + (kind or "final"),
    )(*args)


def _normalize_pair(ext_even, ext_odd, extra_even=None, extra_odd=None):
    lo = _lane_lt_half(ext_even.shape)
    den = jnp.where(lo, ext_odd, ext_even)
    if extra_even is not None:
        den = den + jnp.where(lo, extra_odd, extra_even)
    return jnp.where(lo, ext_even, ext_odd) * pltpu.roll(1.0 / den, HALF, 1)


def _causal_attn_kernel(q_ref, k_ref, vt_ref, o_ref, acc_ref):
    S = q_ref.shape[0]
    heads = range(ATT_HEADS)
    lanes = [slice(h * LANES, (h + 1) * LANES) for h in heads]

    def tile(q0, jobs):
        chains = [(h,) + job for job in jobs for h in heads]
        s_all = [lax.dot_general(q_ref[pl.ds(q0 + roff, nq), lanes[h]], k_ref[pl.ds(0, nk), lanes[h]],
                                 NT_DIMS, preferred_element_type=F32) for h, roff, nq, nk in chains]
        p_all = []
        for (h, roff, nq, nk), s in zip(chains, s_all):
            blocks = [s[:, j * LANES:(j + 1) * LANES] for j in range(nk // LANES)]
            row = lax.broadcasted_iota(jnp.int32, (nq, LANES), 0) + (roff + q0)
            col = lax.broadcasted_iota(jnp.int32, (nq, LANES), 1)
            blocks = [b if (j + 1) * LANES - 1 <= roff + q0 else jnp.where(col + j * LANES <= row, b, NEG_INF)
                      for j, b in enumerate(blocks)]
            m = jnp.max(functools.reduce(jnp.maximum, blocks), axis=-1, keepdims=True)
            p_all.append(jnp.concatenate([jnp.exp2(b - m) for b in blocks], axis=1).astype(BF16))
        for (h, roff, nq, nk), p in zip(chains, p_all):
            parts = [vt_ref[t, lanes[h], :] for t in range(nk // TK)]
            if nk % TK:
                parts.append(vt_ref[nk // TK, lanes[h], 0:nk % TK])
            vt = parts[0] if len(parts) == 1 else jnp.concatenate(parts, axis=1)
            acc_ref[h, roff:roff + nq, :] = lax.dot_general(p, vt, NT_DIMS, preferred_element_type=F32)

    half = TQ // 2
    for qi in range(S // TQ):
        q0 = qi * TQ
        tile(q0, [(0, half, q0 + half), (half, half, q0 + TQ)])
        for p in range(ATT_HEADS // 2):
            o_ref[q0:q0 + TQ, p * LANES:(p + 1) * LANES] = _normalize_pair(
                acc_ref[2 * p], acc_ref[2 * p + 1]).astype(BF16)


def _causal_attn(q, k, vt):
    B, S, _ = q.shape
    spec = pl.BlockSpec((None, S, ATT_HEADS * LANES), lambda b, p: (b, 0, p))
    return pl.pallas_call(
        _causal_attn_kernel,
        grid=(B, HEADS // ATT_HEADS),
        in_specs=[spec, spec, pl.BlockSpec((None, S // TK, ATT_HEADS * LANES, TK), lambda b, p: (b, 0, p, 0))],
        out_specs=pl.BlockSpec((None, S, ATT_HEADS * HALF), lambda b, p: (b, 0, p)),
        out_shape=jax.ShapeDtypeStruct((B, S, HEADS * HALF), BF16),
        scratch_shapes=[pltpu.VMEM((ATT_HEADS, TQ, LANES), F32)],
        compiler_params=_params(("parallel", "parallel")),
        name="causal_attn",
    )(q, k, vt)


def _swa_attn_kernel(sink_ref, q_ref, kp_ref, kc_ref, vp_ref, vc_ref, bias0_ref, bias_ref, o_ref):
    nqb = q_ref.shape[0] // BLOCK
    lo = _lane_lt_half((BLOCK, LANES))
    zero = jnp.zeros((), BF16)

    def band(prev_ref, cur_ref, i, group):
        cols = slice(group * LANES, (group + 1) * LANES)
        if i == 0:
            return jnp.concatenate([prev_ref[:, cols], cur_ref[0:BLOCK, cols]], axis=0)
        return cur_ref[(i - 1) * BLOCK:(i + 1) * BLOCK, cols]

    def qk(i, g):
        rows = slice(i * BLOCK, (i + 1) * BLOCK)
        kb = band(kp_ref, kc_ref, i, g)
        qa = q_ref[rows, (2 * g) * LANES:(2 * g + 1) * LANES]
        qb = q_ref[rows, (2 * g + 1) * LANES:(2 * g + 2) * LANES]
        out = []
        for par in range(2):
            keep = lo if par == 0 else jnp.logical_not(lo)
            stack = jnp.concatenate([jnp.where(keep, qa, zero), jnp.where(keep, qb, zero)], axis=0)
            out.append(lax.dot_general(stack, kb, NT_DIMS, preferred_element_type=F32))
        return out

    def finish(i, g, s_pair):
        b_ref = bias0_ref if i == 0 else bias_ref
        ext, extra = [], []
        for par, s in enumerate(s_pair):
            ha, hb = 4 * g + par, 4 * g + 2 + par
            blocks = [s[:, j * LANES:(j + 1) * LANES]
                      + jnp.concatenate([b_ref[ha, :, j * LANES:(j + 1) * LANES],
                                         b_ref[hb, :, j * LANES:(j + 1) * LANES]], axis=0) for j in range(2)]
            sink = jnp.concatenate([jnp.full((BLOCK, LANES), sink_ref[ha], F32),
                                    jnp.full((BLOCK, LANES), sink_ref[hb], F32)], axis=0)
            m = jnp.maximum(sink, jnp.max(jnp.maximum(blocks[0], blocks[1]), axis=-1, keepdims=True))
            p = jnp.concatenate([jnp.exp2(b - m) for b in blocks], axis=1).astype(BF16)
            vb = band(vp_ref, vc_ref, i, 2 * g + par)
            ext.append(jnp.dot(p, vb, preferred_element_type=F32))
            extra.append(jnp.exp2(sink - m))
        for j in range(2):
            rows = slice(j * BLOCK, (j + 1) * BLOCK)
            o_ref[i * BLOCK:(i + 1) * BLOCK, (2 * g + j) * LANES:(2 * g + j + 1) * LANES] = _normalize_pair(
                ext[0][rows], ext[1][rows], extra[0][rows], extra[1][rows]).astype(BF16)

    units = [(i, g) for i in range(nqb) for g in range(SWA_KV_HEADS)]
    s_cur = qk(*units[0])
    for u, (i, g) in enumerate(units):
        s_next = qk(*units[u + 1]) if u + 1 < len(units) else None
        finish(i, g, s_cur)
        s_cur = s_next


def _swa_attn(sinks2, q, k, v, bias):
    B, S, _ = q.shape
    rows = SWA_QB * BLOCK
    cur = lambda w: pl.BlockSpec((None, rows, w), lambda b, n: (b, n, 0))
    prev = lambda w: pl.BlockSpec((None, BLOCK, w), lambda b, n: (b, jnp.maximum(n * SWA_QB - 1, 0), 0))
    bias_shape = (None, SWA_Q_HEADS, BLOCK, 2 * BLOCK)
    kw, vw = k.shape[-1], v.shape[-1]
    return pl.pallas_call(
        _swa_attn_kernel,
        grid=(B, S // rows),
        in_specs=[pl.BlockSpec(memory_space=pltpu.SMEM), cur(SWA_WIDTH), prev(kw), cur(kw), prev(vw), cur(vw),
                  pl.BlockSpec(bias_shape, lambda b, n: (jnp.minimum(n, 1), 0, 0, 0)),
                  pl.BlockSpec(bias_shape, lambda b, n: (1, 0, 0, 0))],
        out_specs=cur(SWA_WIDTH),
        out_shape=jax.ShapeDtypeStruct((B, S, SWA_WIDTH), BF16),
        compiler_params=_params(("parallel", "parallel")),
        name="swa_attn",
    )(sinks2, q, k, k, v, v, bias, bias)


def _rot(w):
    half = w.shape[-1] // 2
    return jnp.concatenate([-w[..., half:], w[..., :half]], axis=-1)


def _rope_tables(seq, c0):
    inv = ROPE_THETA ** (-np.arange(0, MLA_ROPE_DIM, 2, dtype=np.float64) / MLA_ROPE_DIM)
    ang = np.arange(seq, dtype=np.float64)[:, None] * inv[None, :]
    cos = np.concatenate([np.cos(ang)] * 2, axis=-1)
    sin = np.concatenate([np.sin(ang)] * 2, axis=-1)
    z32 = np.zeros((seq, LANES - HALF - MLA_ROPE_DIM))
    z64 = np.zeros((seq, HALF))
    c64 = np.full((seq, HALF), c0)
    aq_even = np.concatenate([c64, c0 * cos, z32], axis=-1)
    bq_even = np.concatenate([z64, c0 * sin, z32], axis=-1)
    aq_odd = np.concatenate([c0 * cos, z32, c64], axis=-1)
    bq_odd = np.concatenate([c0 * sin, z32, z64], axis=-1)
    ak = np.concatenate([z64, cos, z32], axis=-1)
    bk = np.concatenate([z64, sin, z32], axis=-1)
    return tuple(t.astype(np.float32) for t in (aq_even, bq_even, aq_odd, bq_odd, ak, bk))


def _mla_operands(w_in, q_norm, w_q_up, kv_norm, w_kv_up):
    L = w_in.shape[0]
    c2 = MLA_Q_RANK + MLA_KV_RANK
    c3 = c2 + MLA_ROPE_DIM
    kr = w_in[..., c2:c3]
    krg = jnp.concatenate([jnp.zeros((L, D_MODEL, HALF), F32), kr, _rot(kr)], axis=-1)
    win = jnp.concatenate([w_in[..., :c2], krg, w_in[..., c3:]], axis=-1).astype(BF16)
    odd = (jnp.arange(MLA_HEADS) % 2 == 1)[:, None]
    wq = w_q_up.reshape(L, MLA_Q_RANK, MLA_HEADS, MLA_NOPE_DIM + MLA_ROPE_DIM)
    nope, rope = wq[..., :MLA_NOPE_DIM], wq[..., MLA_NOPE_DIM:]
    wq = jnp.where(odd, jnp.concatenate([rope, _rot(rope), nope], axis=-1),
                   jnp.concatenate([nope, rope, _rot(rope)], axis=-1))
    wq = wq.reshape(L, MLA_Q_RANK, PADW).astype(BF16)
    wkv = w_kv_up.reshape(L, MLA_KV_RANK, MLA_HEADS, MLA_NOPE_DIM + MLA_V_DIM)
    wk = wkv[..., :MLA_NOPE_DIM].reshape(L, MLA_KV_RANK, MLA_HEADS * MLA_NOPE_DIM).astype(BF16)
    wvt = jnp.swapaxes(wkv[..., MLA_NOPE_DIM:].reshape(L, MLA_KV_RANK, MLA_WIDTH), 1, 2).astype(BF16)
    return win, q_norm.reshape(L, 1, -1), kv_norm.reshape(L, 1, -1), wq, wk, wvt


def _fox_layer_operands(w_in, b_f):
    W = FOX_WIDTH
    wf = jnp.concatenate([w_in[:, 3 * W:3 * W + FOX_HEADS], jnp.zeros((D_MODEL, LANES - FOX_HEADS), F32)], axis=-1)
    win = jnp.concatenate([w_in[:, :2 * W], w_in[:, 3 * W + FOX_HEADS:], wf], axis=-1).astype(BF16)
    wvt = w_in[:, 2 * W:3 * W].T.astype(BF16)
    bf = jnp.concatenate([b_f, jnp.zeros((LANES - FOX_HEADS,), F32)]).reshape(1, LANES)
    h = np.arange(FOX_HEADS)
    base = (h // 2) * LANES + np.where(h % 2 == 0, HALF, 0)
    scat = np.zeros((LANES, 2 * W), np.float32)
    onesq = np.zeros((1, W), np.float32)
    onesk = np.zeros((1, W), np.float32)
    for j in range(3):
        scat[j * FOX_HEADS + h, base + j] = 1.0
        scat[j * FOX_HEADS + h, W + base + 3 + j] = -1.0
        onesq[0, base + 3 + j] = 1.0
        onesk[0, base + j] = 1.0
    return win, wvt, bf, jnp.asarray(scat, dtype=BF16), onesq, onesk


def _swa_bias():
    qi = np.arange(BLOCK)[:, None]
    kj = np.arange(2 * BLOCK)[None, :]
    dist = qi + BLOCK - kj
    valid = (dist >= 0) & (dist < WINDOW)
    slopes = 2.0 ** (-8.0 * np.arange(1, SWA_Q_HEADS + 1, dtype=np.float64) / SWA_Q_HEADS)
    ali = -(slopes[:, None, None] * dist.astype(np.float64)[None]) * LOG2E
    bias_rest = np.where(valid[None], ali, NEG_INF)
    bias_first = np.where((valid & (kj >= BLOCK))[None], ali, NEG_INF)
    return np.stack([bias_first, bias_rest]).astype(np.float32)


def kernel(x, ln_g, ln_b, mla_w_in, mla_q_norm, mla_w_q_up, mla_kv_norm, mla_w_kv_up, mla_w_out,
           swa_w_in, swa_sinks, swa_w_out, fox_w_in, fox_b_f, fox_w_out):
    seq = x.shape[1]
    mla_ops = _mla_operands(mla_w_in, mla_q_norm, mla_w_q_up, mla_kv_norm, mla_w_kv_up)
    mla_tables = _rope_tables(seq, (MLA_NOPE_DIM + MLA_ROPE_DIM) ** -0.5 * LOG2E)
    w_outs = {"mla": mla_w_out.astype(BF16), "swa": swa_w_out.astype(BF16), "fox": fox_w_out.astype(BF16)}
    ln_g3, ln_b3 = ln_g.reshape(DEPTH, 1, D_MODEL), ln_b.reshape(DEPTH, 1, D_MODEL)
    out_ops = None
    for i in range(DEPTH + 1):
        j = i // N_MIXERS
        kind = ("mla", "swa", "fox")[i % N_MIXERS] if i < DEPTH else None
        if kind == "mla":
            proj_ops = tuple(_Layer(a, j) for a in mla_ops) + mla_tables
        elif kind == "swa":
            proj_ops = (swa_w_in[j].astype(BF16),)
        elif kind == "fox":
            proj_ops = _fox_layer_operands(fox_w_in[j], fox_b_f[j])
        else:
            proj_ops = ()
        res = _boundary(x, out_ops, kind, proj_ops)
        if out_ops is not None:
            x, res = res[0], res[1:]
        if kind is None:
            return x
        q, k, v, sg = res
        if kind == "swa":
            o = _swa_attn(swa_sinks[j].astype(F32) * LOG2E, q, k, v, _swa_bias())
        else:
            o = _causal_attn(q, k, v)
        out_ops = (o, sg, _Layer(w_outs[kind], j), _Layer(ln_g3, i), _Layer(ln_b3, i))
```
